```python
import math
import jax, jax.numpy as jnp
from jax import lax
import numpy as np

D_MODEL = 1024
BATCH = 16
SEQ = 2048
DEPTH = 2

DN_ALPHA = (2.0 * DEPTH) ** 0.25
DN_BETA = (8.0 * DEPTH) ** -0.25
LN_EPS = 1e-5

A_HEAD_DIM = 64
A_HEADS = (D_MODEL // 2) // A_HEAD_DIM
A_KV_HEADS = 2
IDX_HEADS = 4
IDX_DIM = 64
TOPK_MAX = 256
Q_BLOCK = 128

REL_BUCKETS = 32
REL_MAX_DIST = 128

B_HEADS = 4
B_VAL_DIM = (D_MODEL // 2) // B_HEADS
B_KEY_DIM = B_VAL_DIM // 2
GATE_RANK = 16
GATE_TAU = 16.0
GLA_CHUNK = 64

C_GROUP = 16
C_GROUPS = D_MODEL // C_GROUP
C_STATE = 64

MOE_GROUPS = 4
MOE_EXPERTS = 4
MOE_HIDDEN = 512
MOE_TOPK = 2

N_EVEN = (DEPTH + 1) // 2
N_ODD = DEPTH // 2

AB_SPLITS = (A_HEADS * A_HEAD_DIM, A_KV_HEADS * A_HEAD_DIM, A_KV_HEADS * A_HEAD_DIM,
             IDX_HEADS * IDX_DIM, IDX_DIM, IDX_HEADS,
             B_HEADS * B_KEY_DIM, B_HEADS * B_KEY_DIM, B_HEADS * B_VAL_DIM, GATE_RANK,
             B_HEADS * B_VAL_DIM)
AB_PROJ = 2644
AB_OUT = A_HEADS * A_HEAD_DIM + B_HEADS * B_VAL_DIM

kernel_name = "hybrid_dsa_gla_s5_hiermoe_deepnorm"


def layer_norm(x, g, b):
    xf = x.astype(jnp.float32)
    mu = jnp.mean(xf, -1, keepdims=True)
    var = jnp.mean(jnp.square(xf - mu), -1, keepdims=True)
    return ((xf - mu) * lax.rsqrt(var + LN_EPS) * g + b).astype(x.dtype)


def t5_bucket(rel):
    n = jnp.maximum(rel, 0)
    max_exact = REL_BUCKETS // 2
    nf = jnp.maximum(n, max_exact).astype(jnp.float32)
    large = max_exact + (jnp.log(nf / max_exact) / math.log(REL_MAX_DIST / max_exact)
                         * (REL_BUCKETS - max_exact)).astype(jnp.int32)
    large = jnp.minimum(large, REL_BUCKETS - 1)
    return jnp.where(n < max_exact, n, large)


def dsa_attention(q, k, v, iq, ik, iw, rel_bias):
    bsz, L = q.shape[0], q.shape[1]
    k_sel = min(TOPK_MAX, L // 4)
    nb = L // Q_BLOCK
    rep = A_HEADS // A_KV_HEADS
    s_pos = jnp.arange(L, dtype=jnp.int32)
    ikf = ik.astype(jnp.float32)

    def to_blocks(t):
        return jnp.swapaxes(t.reshape((bsz, nb, Q_BLOCK) + t.shape[2:]), 0, 1)

    def block(args):
        bi, qb, iqb, iwb = args
        t_pos = bi * Q_BLOCK + jnp.arange(Q_BLOCK, dtype=jnp.int32)
        dots = jax.nn.relu(jnp.einsum('bthd,bsd->bths', iqb.astype(jnp.float32), ikf) * IDX_DIM ** -0.5)
        score = jnp.einsum('bths,bth->bts', dots, iwb.astype(jnp.float32) * IDX_HEADS ** -0.5)
        causal = s_pos[None, None, :] <= t_pos[None, :, None]
        score = jnp.where(causal, score, -jnp.inf)
        _, idx = lax.top_k(score, k_sel)
        valid = idx <= t_pos[None, :, None]
        kg = jax.vmap(lambda kk, ii: kk[ii])(k, idx)
        vg = jax.vmap(lambda vv, ii: vv[ii])(v, idx)
        bias = rel_bias[t5_bucket(t_pos[None, :, None] - idx)]
        bias = jnp.moveaxis(bias.reshape(bsz, Q_BLOCK, k_sel, A_KV_HEADS, rep), 2, 4)
        qg = qb.reshape(bsz, Q_BLOCK, A_KV_HEADS, rep, A_HEAD_DIM)
        logits = jnp.einsum('btgrd,btkgd->btgrk', qg, kg).astype(jnp.float32) * A_HEAD_DIM ** -0.5
        logits = logits + bias.astype(jnp.float32)
        logits = jnp.where(valid[:, :, None, None, :], logits, -jnp.inf)
        p = jax.nn.softmax(logits, axis=-1).astype(vg.dtype)
        o = jnp.einsum('btgrk,btkgd->btgrd', p, vg)
        return o.reshape(bsz, Q_BLOCK, A_HEADS * A_HEAD_DIM).astype(jnp.float32)

    out = lax.map(block, (jnp.arange(nb, dtype=jnp.int32), to_blocks(q), to_blocks(iq), to_blocks(iw)))
    return jnp.swapaxes(out, 0, 1).reshape(bsz, L, A_HEADS * A_HEAD_DIM)


def gla_attention(q, k, v, log_a):
    bsz, L, H, dk = q.shape
    dv = v.shape[-1]
    nc = L // GLA_CHUNK

    def chunk(t):
        return t.reshape(bsz, nc, GLA_CHUNK, H, t.shape[-1])

    q, k, v, log_a = chunk(q), chunk(k), chunk(v), chunk(log_a)
    cum = lax.cumsum(log_a, axis=2)
    last = cum[:, :, -1:]
    q_dec = q * jnp.exp(cum)
    k_inv = k * jnp.exp(-cum)
    k_end = k * jnp.exp(last - cum)
    mask = jnp.tril(jnp.ones((GLA_CHUNK, GLA_CHUNK), dtype=bool))
    scores = jnp.where(mask, jnp.einsum('bnihd,bnjhd->bnhij', q_dec, k_inv), 0.0)
    o_intra = jnp.einsum('bnhij,bnjhe->bnihe', scores, v)
    upd = jnp.einsum('bnjhd,bnjhe->bnhde', k_end, v)
    decay = jnp.exp(last[:, :, 0])

    def step(S, inp):
        dec, u = inp
        return dec[..., None] * S + u, S

    S0 = jnp.zeros((bsz, H, dk, dv), dtype=q.dtype)
    _, S_prev = lax.scan(step, S0, (jnp.moveaxis(decay, 1, 0), jnp.moveaxis(upd, 1, 0)))
    S_prev = jnp.moveaxis(S_prev, 0, 1)
    o_inter = jnp.einsum('bnihd,bnhde->bnihe', q_dec, S_prev)
    return (o_intra + o_inter).reshape(bsz, L, H, dv)


def even_mixer(x, w_in, rel_bias, gate_w2, gate_b, norm_g, w_out):
    f32 = jnp.float32
    bsz, L, _ = x.shape
    h = x @ w_in
    offsets = np.cumsum(AB_SPLITS)[:-1].tolist()
    aq, ak, av, iq, ik, iw, bq, bk, bv, bg, br = jnp.split(h, offsets, axis=-1)
    o_a = dsa_attention(aq.reshape(bsz, L, A_HEADS, A_HEAD_DIM),
                        ak.reshape(bsz, L, A_KV_HEADS, A_HEAD_DIM),
                        av.reshape(bsz, L, A_KV_HEADS, A_HEAD_DIM),
                        iq.reshape(bsz, L, IDX_HEADS, IDX_DIM), ik, iw, rel_bias)
    log_a = jax.nn.log_sigmoid((bg @ gate_w2 + gate_b).astype(f32)) / GATE_TAU
    o_b = gla_attention(bq.astype(f32).reshape(bsz, L, B_HEADS, B_KEY_DIM) * B_KEY_DIM ** -0.5,
                        bk.astype(f32).reshape(bsz, L, B_HEADS, B_KEY_DIM),
                        bv.astype(f32).reshape(bsz, L, B_HEADS, B_VAL_DIM),
                        log_a.reshape(bsz, L, B_HEADS, B_KEY_DIM))
    mu = jnp.mean(o_b, -1, keepdims=True)
    var = jnp.mean(jnp.square(o_b - mu), -1, keepdims=True)
    o_b = ((o_b - mu) * lax.rsqrt(var + LN_EPS)).reshape(bsz, L, B_HEADS * B_VAL_DIM) * norm_g
    o_b = o_b * jax.nn.silu(br.astype(f32))
    o = jnp.concatenate([o_a, o_b.astype(f32)], axis=-1).astype(x.dtype)
    return o @ w_out


def s5_mixer(x, w_in, lam_re, lam_im, log_dt, b_re, b_im, c_re, c_im, d_skip, glu_w1, glu_w2, w_out):
    f32 = jnp.float32
    bsz, L, _ = x.shape
    u = (x @ w_in).astype(f32).reshape(bsz, L, C_GROUPS, C_GROUP)
    lr = jnp.minimum(lam_re.astype(f32), -1e-4)
    li = lam_im.astype(f32)
    dt = jnp.exp(log_dt.astype(f32))[:, None]
    mag = jnp.exp(lr * dt)
    ab_re = mag * jnp.cos(li * dt)
    ab_im = mag * jnp.sin(li * dt)
    den = lr * lr + li * li
    nr = ab_re - 1.0
    coef_re = (nr * lr + ab_im * li) / den
    coef_im = (ab_im * lr - nr * li) / den
    bb_re = coef_re[..., None] * b_re - coef_im[..., None] * b_im
    bb_im = coef_re[..., None] * b_im + coef_im[..., None] * b_re
    bu_re = jnp.einsum('blgc,gpc->blgp', u, bb_re).astype(f32)
    bu_im = jnp.einsum('blgc,gpc->blgp', u, bb_im).astype(f32)
    a_re = jnp.broadcast_to(ab_re, (1, L, C_GROUPS, C_STATE))
    a_im = jnp.broadcast_to(ab_im, (1, L, C_GROUPS, C_STATE))

    def combine(e1, e2):
        a1r, a1i, b1r, b1i = e1
        a2r, a2i, b2r, b2i = e2
        return (a2r * a1r - a2i * a1i, a2r * a1i + a2i * a1r,
                a2r * b1r - a2i * b1i + b2r, a2r * b1i + a2i * b1r + b2i)

    _, _, s_re, s_im = lax.associative_scan(combine, (a_re, a_im, bu_re, bu_im), axis=1)
    y = (jnp.einsum('blgp,gcp->blgc', s_re, c_re) - jnp.einsum('blgp,gcp->blgc', s_im, c_im)
         + d_skip.reshape(C_GROUPS, C_GROUP) * u)
    g = jax.nn.gelu(y.reshape(bsz, L, D_MODEL).astype(f32))
    z = (g @ glu_w1) * jax.nn.sigmoid(g @ glu_w2)
    return z.astype(x.dtype) @ w_out


def hier_moe(x, r_coarse, rb_coarse, r_fine, rb_fine, w_gate, w_up, w_down):
    f32 = jnp.float32
    bsz, L, d = x.shape
    xt = x.reshape(-1, d)
    gl = (xt @ r_coarse + rb_coarse).astype(f32)
    pg = jax.nn.softmax(gl, axis=-1)
    gsel = jnp.argmax(gl, axis=-1)
    g_onehot = jax.nn.one_hot(gsel, MOE_GROUPS, dtype=f32)
    g_w = jnp.sum(pg * g_onehot, axis=-1)
    fl = (jnp.einsum('nd,gde->nge', xt, r_fine) + rb_fine).astype(f32)
    fl_sel = jnp.take_along_axis(fl, gsel[:, None, None], axis=1)[:, 0]
    top_v, top_i = lax.top_k(fl_sel, MOE_TOPK)
    top_w = jax.nn.softmax(top_v, axis=-1) * g_w[:, None]
    e_gate = jnp.sum(jax.nn.one_hot(top_i, MOE_EXPERTS, dtype=f32) * top_w[..., None], axis=1)
    gate = g_onehot[:, :, None] * e_gate[:, None, :]
    y = jnp.zeros(xt.shape, dtype=f32)
    for g in range(MOE_GROUPS):
        hg = jax.nn.silu(jnp.einsum('nd,edf->nef', xt, w_gate[g])) * jnp.einsum('nd,edf->nef', xt, w_up[g])
        y = y + jnp.einsum('nef,efd->nd', hg * gate[:, g, :, None].astype(hg.dtype), w_down[g])
    return y.reshape(bsz, L, d).astype(x.dtype)


def setup_inputs(seed: int = 0) -> dict:
    key = jax.random.key(seed)
    ks = iter(jax.random.split(key, 40))
    nrm = lambda shape, s: jax.random.normal(next(ks), shape, jnp.float32) * s
    D = D_MODEL
    E, F, G = MOE_EXPERTS, MOE_HIDDEN, MOE_GROUPS
    n_idx = jnp.arange(C_STATE, dtype=jnp.float32)
    return {
        "x": nrm((BATCH, SEQ, D), 1.0),
        "rel_bias": nrm((REL_BUCKETS, A_HEADS), 0.3),
        "ab_w_in": nrm((N_EVEN, D, AB_PROJ), D ** -0.5),
        "gla_gate_w2": nrm((N_EVEN, GATE_RANK, B_HEADS * B_KEY_DIM), GATE_RANK ** -0.5),
        "gla_gate_b": nrm((N_EVEN, B_HEADS * B_KEY_DIM), 0.1),
        "gla_norm_g": 1.0 + nrm((N_EVEN, B_HEADS * B_VAL_DIM), 0.01),
        "ab_w_out": nrm((N_EVEN, AB_OUT, D), AB_OUT ** -0.5 * DN_BETA),
        "s5_w_in": nrm((N_ODD, D, D), D ** -0.5),
        "s5_lam_re": -0.5 + nrm((N_ODD, C_GROUPS, C_STATE), 0.01),
        "s5_lam_im": math.pi * n_idx + nrm((N_ODD, C_GROUPS, C_STATE), 0.01),
        "s5_log_dt": jax.random.uniform(next(ks), (N_ODD, C_GROUPS), jnp.float32,
                                        math.log(1e-3), math.log(1e-1)),
        "s5_b_re": nrm((N_ODD, C_GROUPS, C_STATE, C_GROUP), (2 * C_GROUP) ** -0.5),
        "s5_b_im": nrm((N_ODD, C_GROUPS, C_STATE, C_GROUP), (2 * C_GROUP) ** -0.5),
        "s5_c_re": nrm((N_ODD, C_GROUPS, C_GROUP, C_STATE), C_STATE ** -0.5),
        "s5_c_im": nrm((N_ODD, C_GROUPS, C_GROUP, C_STATE), C_STATE ** -0.5),
        "s5_d": nrm((N_ODD, D), 1.0),
        "s5_glu_w1": nrm((N_ODD, D, D), D ** -0.5),
        "s5_glu_w2": nrm((N_ODD, D, D), D ** -0.5),
        "s5_w_out": nrm((N_ODD, D, D), D ** -0.5 * DN_BETA),
        "ln_mix_g": 1.0 + nrm((DEPTH, D), 0.01),
        "ln_mix_b": nrm((DEPTH, D), 0.01),
        "ln_ffn_g": 1.0 + nrm((DEPTH, D), 0.01),
        "ln_ffn_b": nrm((DEPTH, D), 0.01),
        "moe_r_coarse": nrm((DEPTH, D, G), D ** -0.5),
        "moe_rb_coarse": nrm((DEPTH, G), 0.01),
        "moe_r_fine": nrm((DEPTH, G, D, E), D ** -0.5),
        "moe_rb_fine": nrm((DEPTH, G, E), 0.01),
        "moe_w_gate": nrm((DEPTH, G, E, D, F), D ** -0.5),
        "moe_w_up": nrm((DEPTH, G, E, D, F), D ** -0.5),
        "moe_w_down": nrm((DEPTH, G, E, F, D), F ** -0.5 * DN_BETA),
    }


def reference(x, rel_bias, ab_w_in, gla_gate_w2, gla_gate_b, gla_norm_g, ab_w_out,
              s5_w_in, s5_lam_re, s5_lam_im, s5_log_dt, s5_b_re, s5_b_im, s5_c_re, s5_c_im,
              s5_d, s5_glu_w1, s5_glu_w2, s5_w_out,
              ln_mix_g, ln_mix_b, ln_ffn_g, ln_ffn_b,
              moe_r_coarse, moe_rb_coarse, moe_r_fine, moe_rb_fine,
              moe_w_gate, moe_w_up, moe_w_down):
    h = x
    for layer in range(DEPTH):
        i = layer // 2
        if layer % 2 == 0:
            m = even_mixer(h, ab_w_in[i], rel_bias, gla_gate_w2[i], gla_gate_b[i],
                           gla_norm_g[i], ab_w_out[i])
        else:
            m = s5_mixer(h, s5_w_in[i], s5_lam_re[i], s5_lam_im[i], s5_log_dt[i],
                         s5_b_re[i], s5_b_im[i], s5_c_re[i], s5_c_im[i], s5_d[i],
                         s5_glu_w1[i], s5_glu_w2[i], s5_w_out[i])
        h = layer_norm(DN_ALPHA * h + m.astype(h.dtype), ln_mix_g[layer], ln_mix_b[layer])
        f = hier_moe(h, moe_r_coarse[layer], moe_rb_coarse[layer], moe_r_fine[layer],
                     moe_rb_fine[layer], moe_w_gate[layer], moe_w_up[layer], moe_w_down[layer])
        h = layer_norm(DN_ALPHA * h + f, ln_ffn_g[layer], ln_ffn_b[layer])
    return h.astype(x.dtype)
```

```python
import functools
import math

import numpy as np
import jax
import jax.numpy as jnp
from jax import lax
from jax.experimental import pallas as pl
from jax.experimental.pallas import tpu as pltpu

F32 = jnp.float32
BF16 = jnp.bfloat16
I32 = jnp.int32

D_MODEL = 1024
DEPTH = 2
DN_ALPHA = (2.0 * DEPTH) ** 0.25
LN_EPS = 1e-5
A_HEAD_DIM = 64
A_HEADS = 8
A_KV_HEADS = 2
IDX_HEADS = 4
IDX_DIM = 64
TOPK_MAX = 256
REL_BUCKETS = 32
REL_MAX_DIST = 128
B_HEADS = 4
B_VAL_DIM = 128
B_KEY_DIM = 64
GATE_RANK = 16
GATE_TAU = 16.0
GLA_CHUNK = 64
C_GROUP = 16
C_GROUPS = 64
C_STATE = 64
MOE_GROUPS = 4
MOE_EXPERTS = 4
MOE_HIDDEN = 512
N_EXPERTS = MOE_GROUPS * MOE_EXPERTS

LANE = 128
VMEM_LIMIT = 52 * 1024 * 1024
NEG_INF = float("-inf")
INT_MIN = -(2 ** 31)

_NT = (((1,), (1,)), ((), ()))
_TN = (((0,), (0,)), ((), ()))


def _dot(a, b):
    return jnp.dot(a, b, preferred_element_type=F32)


def _dot_nt(a, b):
    return lax.dot_general(a, b, _NT, preferred_element_type=F32)


def _dot_tn(a, b):
    return lax.dot_general(a, b, _TN, preferred_element_type=F32)


def _split3(x):
    hi = x.astype(BF16)
    r1 = x - hi.astype(F32)
    mid = r1.astype(BF16)
    lo = (r1 - mid.astype(F32)).astype(BF16)
    return hi, mid, lo


def _params(sem):
    return pltpu.CompilerParams(dimension_semantics=sem, vmem_limit_bytes=VMEM_LIMIT)


def _layer_norm(y, g, b):
    mu = jnp.mean(y, axis=-1, keepdims=True)
    d = y - mu
    var = jnp.mean(d * d, axis=-1, keepdims=True)
    return d * lax.rsqrt(var + LN_EPS) * g + b


def _sigmoid(x):
    return 1.0 / (1.0 + jnp.exp(-x))


def _bucket_thresholds():
    max_exact = REL_BUCKETS // 2
    nf = np.arange(max_exact, 4 * REL_MAX_DIST).astype(np.float32)
    large = max_exact + (np.log(nf / np.float32(max_exact))
                         / np.float32(math.log(REL_MAX_DIST / max_exact))
                         * np.float32(REL_BUCKETS - max_exact)).astype(np.int32)
    large = np.minimum(large, REL_BUCKETS - 1)
    return [int(nf[np.argmax(large >= max_exact + j)]) for j in range(1, REL_BUCKETS - max_exact)]


def _bias_kernel(rb_ref, out_ref):
    max_exact = REL_BUCKETS // 2
    srow = lax.broadcasted_iota(I32, (LANE, LANE), 0)
    tcol = lax.broadcasted_iota(I32, (LANE, LANE), 1)
    thr = _bucket_thresholds()
    for band in range(2):
        n = jnp.maximum(band * LANE + tcol - srow, 0)
        large = jnp.full((LANE, LANE), max_exact, I32)
        for t in thr:
            large = large + jnp.where(n >= t, 1, 0)
        bucket = jnp.where(n < max_exact, n, large)
        for h in range(A_HEADS):
            acc = jnp.zeros((LANE, LANE), F32)
            for bk in range(REL_BUCKETS):
                acc = acc + jnp.where(bucket == bk, rb_ref[bk, h], 0.0)
            out_ref[h, band] = acc
    for h in range(A_HEADS):
        out_ref[h, 2] = jnp.full((LANE, LANE), rb_ref[REL_BUCKETS - 1, h], F32)


def _bias_tiles(rel_bias):
    return pl.pallas_call(
        _bias_kernel,
        out_shape=jax.ShapeDtypeStruct((A_HEADS, 3, LANE, LANE), F32),
        in_specs=[pl.BlockSpec(memory_space=pltpu.SMEM)],
        name="rel_bias_tiles",
    )(rel_bias)


def _proj_kernel(x_ref, w_ref, *out_refs, widths, chunk):
    xb = x_ref[...].astype(BF16)
    c0 = 0
    for o_ref, width in zip(out_refs, widths):
        for c in range(0, width, chunk):
            ce = min(c + chunk, width)
            o_ref[:, c:ce] = _dot(xb, w_ref[:, c0 + c:c0 + ce]).astype(o_ref.dtype)
        c0 += width


def _in_proj(x2, w, widths, dtypes, tm=512, chunk=256):
    n, d = x2.shape
    return pl.pallas_call(
        functools.partial(_proj_kernel, widths=widths, chunk=chunk),
        out_shape=[jax.ShapeDtypeStruct((n, wd), dt) for wd, dt in zip(widths, dtypes)],
        grid=(n // tm,),
        in_specs=[pl.BlockSpec((tm, d), lambda i: (i, 0)),
                  pl.BlockSpec(w.shape, lambda i: (0, 0))],
        out_specs=[pl.BlockSpec((tm, wd), lambda i: (i, 0)) for wd in widths],
        compiler_params=_params(("arbitrary",)),
        name="in_proj",
    )(x2, w)


def _sortable(x):
    bits = lax.bitcast_convert_type(x, I32)
    return jnp.where(bits < 0, bits ^ jnp.int32(0x7FFFFFFF), bits)


def _dsa_kernel(q_ref, k_ref, v_ref, iq_ref, slab_ref, slabq_ref, bias_ref, o_ref,
                key_ref, am_ref, vt_ref, ot_ref, *, k_sel, seq):
    i = pl.program_id(1)
    t0 = i * LANE
    nkt = i + 1
    n_tiles = seq // LANE
    srow_l = lax.broadcasted_iota(I32, (LANE, LANE), 0)
    tcol = t0 + lax.broadcasted_iota(I32, (LANE, LANE), 1)

    @pl.when(i == 0)
    def _():
        for jt in range(n_tiles):
            vt_ref[jt] = v_ref[jt * LANE:(jt + 1) * LANE, :].astype(F32).T.astype(BF16)

    slab_t = slabq_ref[...].T
    w_t = slab_t[IDX_DIM:IDX_DIM + IDX_HEADS, :] * (IDX_HEADS ** -0.5)
    iqb = iq_ref[...].astype(BF16)

    def score_tile(j, carry):
        s0 = pl.multiple_of(j * LANE, LANE)
        ikt = slab_ref[pl.ds(s0, LANE), :].astype(BF16)
        acc = jnp.zeros((LANE, LANE), F32)
        for h in range(IDX_HEADS):
            d = _dot_nt(ikt, iqb[:, h * LANE:(h + 1) * LANE]) * (IDX_DIM ** -0.5)
            acc = acc + jnp.maximum(d, 0.0) * w_t[h:h + 1, :]
        acc = jnp.where(s0 + srow_l <= tcol, acc, NEG_INF)
        key_ref[pl.ds(s0, LANE), :] = _sortable(acc)
        return carry

    lax.fori_loop(0, nkt, score_tile, 0)

    def count(pred_fn):
        def body(j, c):
            s0 = pl.multiple_of(j * LANE, LANE)
            kt = key_ref[pl.ds(s0, LANE), :]
            m = jnp.where(pred_fn(kt, s0 + srow_l), 1, 0)
            return c + jnp.sum(m.reshape(LANE // 8, 8, LANE), axis=0)
        c = lax.fori_loop(0, nkt, body, jnp.zeros((8, LANE), I32))
        return jnp.sum(c, axis=0, keepdims=True)

    def bcast(v):
        return jnp.broadcast_to(v, (LANE, LANE))

    def search():
        c0 = count(lambda kt, s: kt >= 0)
        ans0 = jnp.where(c0 >= k_sel, 0, INT_MIN).astype(I32)

        def bit_body(bi, ans):
            cand = ans | lax.shift_left(jnp.int32(1), 30 - bi)
            cb = bcast(cand)
            cnt = count(lambda kt, s: kt >= cb)
            return jnp.where(cnt >= k_sel, cand, ans)

        ans = lax.fori_loop(0, 31, bit_body, ans0)
        ab = bcast(ans)
        cnt_ge = count(lambda kt, s: kt >= ab)
        cnt_gt = count(lambda kt, s: kt > ab)
        need = k_sel - cnt_gt

        def tie_search():
            def idx_body(bi, x):
                cand = x | lax.shift_left(jnp.int32(1), (seq.bit_length() - 2) - bi)
                cb = bcast(cand)
                f = count(lambda kt, s: (kt == ab) & (s < cb))
                return jnp.where(f < need, cand, x)
            return lax.fori_loop(0, seq.bit_length() - 1, idx_body, jnp.zeros((1, LANE), I32))

        cut = lax.cond(jnp.max(cnt_ge) > k_sel, tie_search,
                       lambda: jnp.full((1, LANE), seq - 1, I32))
        return ans, cut

    ans, cut = lax.cond(t0 >= k_sel, search,
                        lambda: (jnp.full((1, LANE), INT_MIN, I32),
                                 jnp.full((1, LANE), seq - 1, I32)))
    ans_b = bcast(ans)
    cut_b = bcast(cut)

    def mask_tile(j, carry):
        s0 = pl.multiple_of(j * LANE, LANE)
        kt = key_ref[pl.ds(s0, LANE), :]
        srow = s0 + srow_l
        sel = (kt > ans_b) | ((kt == ans_b) & (srow <= cut_b))
        am_ref[pl.ds(s0, LANE), :] = jnp.where(sel & (srow <= tcol), 0.0, NEG_INF)
        return carry

    lax.fori_loop(0, nkt, mask_tile, 0)

    rep = A_HEADS // A_KV_HEADS
    for h in range(A_HEADS):
        g = h // rep
        qh = q_ref[:, h * LANE:(h + 1) * LANE] * jnp.asarray(A_HEAD_DIM ** -0.5, BF16)

        def tile_body(j, carry, g=g, h=h, qh=qh):
            m, l, acc = carry
            s0 = pl.multiple_of(j * LANE, LANE)
            kt = k_ref[pl.ds(s0, LANE), g * LANE:(g + 1) * LANE]
            lg = _dot_nt(kt, qh)
            lg = lg + bias_ref[h, jnp.minimum(i - j, 2)] + am_ref[pl.ds(s0, LANE), :]
            m_new = jnp.maximum(m, jnp.max(lg, axis=0, keepdims=True))
            m_safe = jnp.where(m_new == NEG_INF, 0.0, m_new)
            alpha = jnp.exp(m - m_safe)
            p = jnp.exp(lg - m_safe)
            l = alpha * l + jnp.sum(p, axis=0, keepdims=True)
            vt = vt_ref[j, g * A_HEAD_DIM:(g + 1) * A_HEAD_DIM, :]
            acc = alpha * acc + _dot(vt, p.astype(BF16))
            return m_new, l, acc

        m, l, acc = lax.fori_loop(
            0, nkt, tile_body,
            (jnp.full((1, LANE), NEG_INF, F32), jnp.zeros((1, LANE), F32),
             jnp.zeros((A_HEAD_DIM, LANE), F32)))
        ot_ref[h * A_HEAD_DIM:(h + 1) * A_HEAD_DIM, :] = acc / l
    o_ref[...] = ot_ref[...].T.astype(o_ref.dtype)


def _dsa(h_a, h_f, bias_tiles, bsz, seq):
    n = bsz * seq
    nq = seq // LANE
    k_sel = min(TOPK_MAX, seq // 4)
    assert k_sel % LANE == 0 and seq % LANE == 0
    qw = A_HEADS * LANE
    slab_blk = (h_f.shape[1] - LANE) // LANE
    return pl.pallas_call(
        functools.partial(_dsa_kernel, k_sel=k_sel, seq=seq),
        out_shape=jax.ShapeDtypeStruct((n, A_HEADS * A_HEAD_DIM), BF16),
        grid=(bsz, nq),
        in_specs=[
            pl.BlockSpec((LANE, qw), lambda b, i: (b * nq + i, 0)),
            pl.BlockSpec((seq, A_KV_HEADS * LANE), lambda b, i: (b, qw // (A_KV_HEADS * LANE))),
            pl.BlockSpec((seq, LANE), lambda b, i: (b, (qw + A_KV_HEADS * LANE) // LANE)),
            pl.BlockSpec((LANE, IDX_HEADS * LANE), lambda b, i: (b * nq + i, 0)),
            pl.BlockSpec((seq, LANE), lambda b, i: (b, slab_blk)),
            pl.BlockSpec((LANE, LANE), lambda b, i: (b * nq + i, slab_blk)),
            pl.BlockSpec((A_HEADS, 3, LANE, LANE), lambda b, i: (0, 0, 0, 0)),
        ],
        out_specs=pl.BlockSpec((LANE, A_HEADS * A_HEAD_DIM), lambda b, i: (b * nq + i, 0)),
        scratch_shapes=[
            pltpu.VMEM((seq, LANE), I32),
            pltpu.VMEM((seq, LANE), F32),
            pltpu.VMEM((nq, LANE, LANE), BF16),
            pltpu.VMEM((A_HEADS * A_HEAD_DIM, LANE), F32),
        ],
        compiler_params=_params(("arbitrary", "arbitrary")),
        name="dsa_attention",
    )(h_a, h_a, h_a, h_f, h_f, h_f, bias_tiles)


def _gla_kernel(bq_ref, bk_ref, bv_ref, br_ref, slab_ref, w2_ref, gb_ref, ng_ref, o_ref,
                st_ref, *, n_chunks):
    @pl.when(pl.program_id(1) == 0)
    def _():
        st_ref[...] = jnp.zeros(st_ref.shape, F32)

    ch = GLA_CHUNK
    row = lax.broadcasted_iota(I32, (ch, ch), 0)
    col = lax.broadcasted_iota(I32, (ch, ch), 1)
    tril = row >= col
    tri = jnp.where(tril, 1.0, 0.0).astype(BF16)
    w2 = w2_ref[...]
    for c in range(n_chunks):
        rows = slice(c * ch, (c + 1) * ch)
        gate = _dot(slab_ref[rows, :].astype(BF16), w2) + gb_ref[...]
        log_a = (jnp.minimum(gate, 0.0) - jnp.log1p(jnp.exp(-jnp.abs(gate)))) / GATE_TAU
        hi, mid, lo = _split3(log_a)
        cum = _dot(tri, hi) + _dot(tri, mid) + _dot(tri, lo)
        last = cum[ch - 1:ch, :]
        q = bq_ref[rows, :] * (B_KEY_DIM ** -0.5)
        k = bk_ref[rows, :]
        q_dec = (q * jnp.exp(cum)).astype(BF16)
        k_inv = (k * jnp.exp(-cum)).astype(BF16)
        k_end = (k * jnp.exp(last - cum)).astype(BF16)
        decay = jnp.exp(last)
        for h in range(B_HEADS):
            cs = slice(h * LANE, (h + 1) * LANE)
            v = bv_ref[rows, cs].astype(BF16)
            sc = jnp.where(tril, _dot_nt(q_dec[:, cs], k_inv[:, cs]), 0.0)
            st = st_ref[h]
            o = _dot(sc.astype(BF16), v) + _dot_nt(q_dec[:, cs], st.astype(BF16))
            st_ref[h] = decay[:, cs] * st + _dot_tn(v, k_end[:, cs])
            mu = jnp.mean(o, axis=-1, keepdims=True)
            dlt = o - mu
            var = jnp.mean(dlt * dlt, axis=-1, keepdims=True)
            on = dlt * lax.rsqrt(var + LN_EPS) * ng_ref[:, cs]
            r = br_ref[rows, cs]
            o_ref[rows, cs] = (on * (r * _sigmoid(r))).astype(o_ref.dtype)


def _gla(h_f, w2p, gbp, norm_g, bsz, seq, rows_per_step=256):
    n = bsz * seq
    hw = B_HEADS * LANE
    steps = seq // rows_per_step
    slab_blk = (h_f.shape[1] - LANE) // LANE
    blk = lambda cb: pl.BlockSpec((rows_per_step, hw), lambda b, s, cb=cb: (b * steps + s, cb))
    return pl.pallas_call(
        functools.partial(_gla_kernel, n_chunks=rows_per_step // GLA_CHUNK),
        out_shape=jax.ShapeDtypeStruct((n, hw), BF16),
        grid=(bsz, steps),
        in_specs=[blk(1), blk(2), blk(3), blk(4),
                  pl.BlockSpec((rows_per_step, LANE), lambda b, s: (b * steps + s, slab_blk)),
                  pl.BlockSpec((LANE, hw), lambda b, s: (0, 0)),
                  pl.BlockSpec((1, hw), lambda b, s: (0, 0)),
                  pl.BlockSpec((1, hw), lambda b, s: (0, 0))],
        out_specs=pl.BlockSpec((rows_per_step, hw), lambda b, s: (b * steps + s, 0)),
        scratch_shapes=[pltpu.VMEM((B_HEADS, LANE, LANE), F32)],
        compiler_params=_params(("arbitrary", "arbitrary")),
        name="gla_attention",
    )(h_f, h_f, h_f, h_f, h_f, w2p, gbp, norm_g)


def _outproj_ln_kernel(oa_ref, ob_ref, x_ref, w_ref, g_ref, b_ref, out_ref):
    ka = oa_ref.shape[1]
    m = _dot(oa_ref[...], w_ref[0:ka, :]) + _dot(ob_ref[...], w_ref[ka:, :])
    out_ref[...] = _layer_norm(DN_ALPHA * x_ref[...] + m, g_ref[...], b_ref[...])


def _outproj_ln(o_a, o_b, x2, w, g, b, tm=512):
    n, d = x2.shape
    row = lambda width: pl.BlockSpec((tm, width), lambda i: (i, 0))
    full = lambda a: pl.BlockSpec(a.shape, lambda i: (0, 0))
    return pl.pallas_call(
        _outproj_ln_kernel,
        out_shape=jax.ShapeDtypeStruct((n, d), F32),
        grid=(n // tm,),
        in_specs=[row(o_a.shape[1]), row(o_b.shape[1]), row(d), full(w), full(g), full(b)],
        out_specs=row(d),
        compiler_params=_params(("arbitrary",)),
        name="out_proj_ln",
    )(o_a, o_b, x2, w, g, b)


def _route(lt):
    gl = [lt[g:g + 1, :] for g in range(MOE_GROUPS)]
    best, gsel = gl[0], jnp.zeros_like(gl[0], dtype=I32)
    for g in range(1, MOE_GROUPS):
        better = gl[g] > best
        gsel = jnp.where(better, g, gsel)
        best = jnp.where(better, gl[g], best)
    denom = sum(jnp.exp(x - best) for x in gl)
    g_w = 1.0 / denom
    fl = []
    for e in range(MOE_EXPERTS):
        acc = jnp.zeros_like(best)
        for g in range(MOE_GROUPS):
            r = MOE_GROUPS + g * MOE_EXPERTS + e
            acc = jnp.where(gsel == g, lt[r:r + 1, :], acc)
        fl.append(acc)
    v1, i1 = fl[0], jnp.zeros_like(gsel)
    for e in range(1, MOE_EXPERTS):
        better = fl[e] > v1
        i1 = jnp.where(better, e, i1)
        v1 = jnp.where(better, fl[e], v1)
    v2, i2 = jnp.full_like(v1, NEG_INF), jnp.zeros_like(gsel)
    for e in range(MOE_EXPERTS):
        better = (fl[e] > v2) & (i1 != e)
        i2 = jnp.where(better, e, i2)
        v2 = jnp.where(better, fl[e], v2)
    e2 = jnp.exp(v2 - v1)
    w1 = (1.0 / (1.0 + e2)) * g_w
    w2 = (e2 / (1.0 + e2)) * g_w
    gates = []
    for g in range(MOE_GROUPS):
        for e in range(MOE_EXPERTS):
            ge = jnp.where(i1 == e, w1, 0.0) + jnp.where(i2 == e, w2, 0.0)
            gates.append(jnp.where(gsel == g, ge, 0.0))
    return gates


def _moe_dense_kernel(x_ref, wr_ref, rb_ref, wg_ref, wu_ref, wd_ref, lg_ref, lb_ref, out_ref,
                      gate_ref, acc_ref):
    e = pl.program_id(1)
    tm = x_ref.shape[0]

    @pl.when(e == 0)
    def _():
        x = x_ref[...]
        xs = _split3(x)
        ws = _split3(wr_ref[...])
        lt = jnp.zeros((wr_ref.shape[0], tm), F32)
        for a, b in ((2, 0), (0, 2), (1, 1), (1, 0), (0, 1), (0, 0)):
            lt = lt + _dot_nt(ws[a], xs[b])
        lt = lt + rb_ref[...]
        gates = _route(lt)
        erow = lax.broadcasted_iota(I32, (LANE, tm), 0)
        gt = jnp.zeros((LANE, tm), F32)
        for ee in range(N_EXPERTS):
            gt = jnp.where(erow == ee, gates[ee], gt)
        gt = gt.T
        for ee in range(N_EXPERTS):
            gate_ref[ee] = jnp.broadcast_to(gt[:, ee:ee + 1], (tm, LANE))
        acc_ref[...] = jnp.zeros(acc_ref.shape, F32)

    xb = x_ref[...].astype(BF16)
    hg = _dot(xb, wg_ref[...])
    hu = _dot(xb, wu_ref[...])
    gcol = gate_ref[e]
    hid = hg * _sigmoid(hg) * hu * jnp.concatenate([gcol] * (hg.shape[1] // LANE), axis=1)
    acc_ref[...] += _dot(hid.astype(BF16), wd_ref[...])

    @pl.when(e == N_EXPERTS - 1)
    def _():
        out_ref[...] = _layer_norm(DN_ALPHA * x_ref[...] + acc_ref[...], lg_ref[...], lb_ref[...])


def _moe(h2, wr, rb, wg, wu, wd, lg, lb, tm=1024):
    n, d = h2.shape
    f = wg.shape[2]
    tm = min(tm, n)
    full = lambda a: pl.BlockSpec(a.shape, lambda i, e: (0,) * a.ndim)
    return pl.pallas_call(
        _moe_dense_kernel,
        out_shape=jax.ShapeDtypeStruct((n, d), F32),
        grid=(n // tm, N_EXPERTS),
        in_specs=[pl.BlockSpec((tm, d), lambda i, e: (i, 0)), full(wr), full(rb),
                  pl.BlockSpec((None, d, f), lambda i, e: (e, 0, 0)),
                  pl.BlockSpec((None, d, f), lambda i, e: (e, 0, 0)),
                  pl.BlockSpec((None, f, d), lambda i, e: (e, 0, 0)),
                  full(lg), full(lb)],
        out_specs=pl.BlockSpec((tm, d), lambda i, e: (i, 0)),
        scratch_shapes=[pltpu.VMEM((N_EXPERTS, tm, LANE), F32), pltpu.VMEM((tm, d), F32)],
        compiler_params=_params(("arbitrary", "arbitrary")),
        name="moe_ffn",
    )(h2, wr, rb, wg, wu, wd, lg, lb)


def _s5_prep_kernel(lre_ref, lim_ref, ldt_ref, bre_ref, bim_ref, are_ref, aim_ref, bbre_ref,
                    bbim_ref):
    lr = jnp.minimum(lre_ref[...], -1e-4)
    li = lim_ref[...]
    dt = jnp.exp(ldt_ref[...])
    mag = jnp.exp(lr * dt)
    ab_re = mag * jnp.cos(li * dt)
    ab_im = mag * jnp.sin(li * dt)
    den = lr * lr + li * li
    nr = ab_re - 1.0
    coef_re = (nr * lr + ab_im * li) / den
    coef_im = (ab_im * lr - nr * li) / den
    are_ref[...] = ab_re
    aim_ref[...] = ab_im
    bbre_ref[...] = coef_re * bre_ref[...] - coef_im * bim_ref[...]
    bbim_ref[...] = coef_re * bim_ref[...] + coef_im * bre_ref[...]


def _s5_prep(lam_re, lam_im, log_dt, b_re, b_im):
    gp = C_GROUPS * C_STATE
    col = lambda a: a.reshape(gp, 1)
    ldt = jnp.broadcast_to(log_dt[:, None], (C_GROUPS, C_STATE))
    outs = pl.pallas_call(
        _s5_prep_kernel,
        out_shape=[jax.ShapeDtypeStruct((gp, 1), F32), jax.ShapeDtypeStruct((gp, 1), F32),
                   jax.ShapeDtypeStruct((gp, C_GROUP), F32), jax.ShapeDtypeStruct((gp, C_GROUP), F32)],
        name="s5_discretise",
    )(col(lam_re), col(lam_im), col(ldt), b_re.reshape(gp, C_GROUP), b_im.reshape(gp, C_GROUP))
    ab_re, ab_im, bb_re, bb_im = outs
    return (ab_re.reshape(C_GROUPS, C_STATE), ab_im.reshape(C_GROUPS, C_STATE),
            bb_re.reshape(C_GROUPS, C_STATE, C_GROUP), bb_im.reshape(C_GROUPS, C_STATE, C_GROUP))


def _s5_uproj_kernel(x_ref, w_ref, o_ref):
    o_ref[...] = _dot(x_ref[...].astype(BF16), w_ref[...])


def _s5_uproj(h2, w, bsz, seq, tm=512):
    n, d = h2.shape
    steps = seq // tm
    out = pl.pallas_call(
        _s5_uproj_kernel,
        out_shape=jax.ShapeDtypeStruct((seq, bsz * d), F32),
        grid=(bsz, steps),
        in_specs=[pl.BlockSpec((tm, d), lambda b, s: (b * steps + s, 0)),
                  pl.BlockSpec(w.shape, lambda b, s: (0, 0))],
        out_specs=pl.BlockSpec((tm, d), lambda b, s: (s, b)),
        compiler_params=_params(("arbitrary", "arbitrary")),
        name="s5_in_proj",
    )(h2, w)
    return out.reshape(seq * bsz, d)


def _gelu_tanh(x):
    return 0.5 * x * (1.0 + jnp.tanh(math.sqrt(2.0 / math.pi) * (x + 0.044715 * (x * x * x))))


def _s5_scan_kernel(u_ref, bbre_ref, bbim_ref, are_ref, aim_ref, cm_ref, d_ref, g_ref,
                    st_ref, bure_ref, buim_ref, s_ref, *, bsz, tl, n_cb, sub):
    @pl.when(pl.program_id(0) == 0)
    def _():
        st_ref[...] = jnp.zeros(st_ref.shape, F32)

    cw = D_MODEL // n_cb
    sw = bure_ref.shape[1]
    for cb in range(n_cb):
        ch = slice(cb * cw, (cb + 1) * cw)
        u = u_ref[:, ch]
        ub = u.astype(BF16)
        bure_ref[...] = _dot(ub, bbre_ref[cb])
        buim_ref[...] = _dot(ub, bbim_ref[cb])
        for s0 in range(0, sw, sub):
            cs = slice(s0, s0 + sub)
            a_re = jnp.broadcast_to(are_ref[cb, :, cs], (bsz, sub))
            a_im = jnp.broadcast_to(aim_ref[cb, :, cs], (bsz, sub))

            def step(t, carry, cs=cs, a_re=a_re, a_im=a_im):
                s_re, s_im = carry
                r0 = pl.multiple_of(t * bsz, bsz)
                n_re = a_re * s_re - a_im * s_im + bure_ref[pl.ds(r0, bsz), cs]
                n_im = a_re * s_im + a_im * s_re + buim_ref[pl.ds(r0, bsz), cs]
                s_ref[pl.ds(r0, bsz), cs] = n_re.astype(BF16)
                s_ref[pl.ds(r0, bsz), sw + s0:sw + s0 + sub] = n_im.astype(BF16)
                return n_re, n_im

            s_re, s_im = lax.fori_loop(0, tl, step, (st_ref[cb, 0, :, cs], st_ref[cb, 1, :, cs]),
                                       unroll=2)
            st_ref[cb, 0, :, cs] = s_re
            st_ref[cb, 1, :, cs] = s_im
        y = _dot(s_ref[...], cm_ref[cb]) + d_ref[:, ch] * u
        g_ref[:, ch] = _gelu_tanh(y).astype(g_ref.dtype)


def _s5_scan(u_lb, bbre, bbim, a_re, a_im, cmat, d_skip, bsz, seq, tl=32, sub=512):
    n, d = u_lb.shape
    n_cb, cw, sw = bbre.shape
    rows = tl * bsz
    full = lambda a: pl.BlockSpec(a.shape, lambda t: (0,) * a.ndim)
    return pl.pallas_call(
        functools.partial(_s5_scan_kernel, bsz=bsz, tl=tl, n_cb=n_cb, sub=sub),
        out_shape=jax.ShapeDtypeStruct((n, d), BF16),
        grid=(seq // tl,),
        in_specs=[pl.BlockSpec((rows, d), lambda t: (t, 0)), full(bbre), full(bbim), full(a_re),
                  full(a_im), full(cmat), full(d_skip)],
        out_specs=pl.BlockSpec((rows, d), lambda t: (t, 0)),
        scratch_shapes=[pltpu.VMEM((n_cb, 2, bsz, sw), F32), pltpu.VMEM((rows, sw), F32),
                        pltpu.VMEM((rows, sw), F32), pltpu.VMEM((rows, 2 * sw), BF16)],
        compiler_params=_params(("arbitrary",)),
        name="s5_scan",
    )(u_lb, bbre, bbim, a_re, a_im, cmat, d_skip)


def _s5_glu_kernel(g_ref, x_ref, w1_ref, w2_ref, wo_ref, lg_ref, lb_ref, out_ref):
    g = g_ref[...]
    z = _dot(g, w1_ref[...]) * _sigmoid(_dot(g, w2_ref[...]))
    m = _dot(z.astype(BF16), wo_ref[...])
    out_ref[...] = _layer_norm(DN_ALPHA * x_ref[...] + m, lg_ref[...], lb_ref[...])


def _s5_glu(g_lb, h2, w1, w2, wo, lg, lb, bsz, seq, tm=512):
    n, d = h2.shape
    steps = seq // tm
    full = lambda a: pl.BlockSpec(a.shape, lambda b, s: (0, 0))
    return pl.pallas_call(
        _s5_glu_kernel,
        out_shape=jax.ShapeDtypeStruct((n, d), F32),
        grid=(bsz, steps),
        in_specs=[pl.BlockSpec((tm, d), lambda b, s: (s, b)),
                  pl.BlockSpec((tm, d), lambda b, s: (b * steps + s, 0)),
                  full(w1), full(w2), full(wo), full(lg), full(lb)],
        out_specs=pl.BlockSpec((tm, d), lambda b, s: (b * steps + s, 0)),
        compiler_params=_params(("arbitrary", "arbitrary")),
        name="s5_glu_out_ln",
    )(g_lb.reshape(seq, bsz * d), h2, w1, w2, wo, lg, lb)


def _pad_heads(w, heads, dim):
    d = w.shape[0]
    w = w.reshape(d, heads, dim)
    return jnp.pad(w, ((0, 0), (0, 0), (0, LANE - dim))).reshape(d, heads * LANE)


def _even_layer_weights(w_in, gate_w2, gate_b):
    splits = (A_HEADS * A_HEAD_DIM, A_KV_HEADS * A_HEAD_DIM, A_KV_HEADS * A_HEAD_DIM,
              IDX_HEADS * IDX_DIM, IDX_DIM, IDX_HEADS,
              B_HEADS * B_KEY_DIM, B_HEADS * B_KEY_DIM, B_HEADS * B_VAL_DIM, GATE_RANK,
              B_HEADS * B_VAL_DIM)
    offs = np.cumsum((0,) + splits)
    aq, ak, av, iq, ik, iw, bq, bk, bv, bg, br = [w_in[:, offs[k]:offs[k + 1]] for k in range(11)]
    d = w_in.shape[0]
    slab = jnp.concatenate(
        [ik, iw, bg, jnp.zeros((d, LANE - IDX_DIM - IDX_HEADS - GATE_RANK), w_in.dtype)], axis=1)
    w_a = jnp.concatenate([_pad_heads(aq, A_HEADS, A_HEAD_DIM),
                           _pad_heads(ak, A_KV_HEADS, A_HEAD_DIM), av], axis=1)
    w_f = jnp.concatenate([_pad_heads(iq, IDX_HEADS, IDX_DIM), _pad_heads(bq, B_HEADS, B_KEY_DIM),
                           _pad_heads(bk, B_HEADS, B_KEY_DIM), bv, br, slab], axis=1)
    w_all = jnp.concatenate([w_a, w_f], axis=1).astype(BF16)
    g0 = IDX_DIM + IDX_HEADS
    w2p = jnp.zeros((LANE, B_HEADS * LANE), F32).at[g0:g0 + GATE_RANK].set(
        _pad_heads(gate_w2, B_HEADS, B_KEY_DIM)).astype(BF16)
    gbp = _pad_heads(gate_b[None, :], B_HEADS, B_KEY_DIM)
    return w_all, (w_a.shape[1], w_f.shape[1]), w2p, gbp


def _even_mixer_ln(x2, bsz, seq, w_in, bias_tiles, gate_w2, gate_b, norm_g, w_out, ln_g, ln_b):
    w_all, widths, w2p, gbp = _even_layer_weights(w_in, gate_w2, gate_b)
    h_a, h_f = _in_proj(x2, w_all, widths, (BF16, F32))
    o_a = _dsa(h_a, h_f, bias_tiles, bsz, seq)
    o_b = _gla(h_f, w2p, gbp, norm_g[None, :], bsz, seq)
    return _outproj_ln(o_a, o_b, x2, w_out.astype(BF16), ln_g[None, :], ln_b[None, :])


def _block_diag(m, per):
    g, r, c = m.shape
    eye = jnp.eye(per, dtype=m.dtype)
    m = m.reshape(g // per, per, r, c)
    bd = m[:, :, :, None, :] * eye[None, :, None, :, None]
    return bd.reshape(g // per, per * r, per * c)


def _s5_mixer_ln(h2, bsz, seq, w_in, lam_re, lam_im, log_dt, b_re, b_im, c_re, c_im, d_skip,
                 glu_w1, glu_w2, w_out, ln_g, ln_b, groups_per_block=16):
    ab_re, ab_im, bb_re, bb_im = _s5_prep(lam_re, lam_im, log_dt, b_re, b_im)
    per = groups_per_block
    n_cb = C_GROUPS // per
    bbre = _block_diag(jnp.swapaxes(bb_re, 1, 2), per).astype(BF16)
    bbim = _block_diag(jnp.swapaxes(bb_im, 1, 2), per).astype(BF16)
    a_re = ab_re.reshape(n_cb, 1, per * C_STATE)
    a_im = ab_im.reshape(n_cb, 1, per * C_STATE)
    cre = _block_diag(jnp.swapaxes(c_re, 1, 2), per)
    cim = _block_diag(jnp.swapaxes(c_im, 1, 2), per)
    cmat = jnp.concatenate([cre, -cim], axis=1).astype(BF16)
    u_lb = _s5_uproj(h2, w_in.astype(BF16), bsz, seq)
    g_lb = _s5_scan(u_lb, bbre, bbim, a_re, a_im, cmat, d_skip[None, :], bsz, seq)
    return _s5_glu(g_lb, h2, glu_w1.astype(BF16), glu_w2.astype(BF16), w_out.astype(BF16),
                   ln_g[None, :], ln_b[None, :], bsz, seq)


def _moe_ln(h, r_coarse, rb_coarse, r_fine, rb_fine, w_gate, w_up, w_down, ln_g, ln_b):
    d = h.shape[1]
    f = w_gate.shape[-1]
    rows = MOE_GROUPS + N_EXPERTS
    wr = jnp.concatenate([r_coarse.T, jnp.transpose(r_fine, (0, 2, 1)).reshape(N_EXPERTS, d),
                          jnp.zeros((32 - rows, d), F32)], axis=0)
    rb = jnp.concatenate([rb_coarse, rb_fine.reshape(N_EXPERTS), jnp.zeros((32 - rows,), F32)])
    return _moe(h, wr, rb[:, None], w_gate.reshape(N_EXPERTS, d, f).astype(BF16),
                w_up.reshape(N_EXPERTS, d, f).astype(BF16),
                w_down.reshape(N_EXPERTS, f, d).astype(BF16), ln_g[None, :], ln_b[None, :])


def kernel(x, rel_bias, ab_w_in, gla_gate_w2, gla_gate_b, gla_norm_g, ab_w_out, s5_w_in, s5_lam_re, s5_lam_im, s5_log_dt, s5_b_re, s5_b_im, s5_c_re, s5_c_im, s5_d, s5_glu_w1, s5_glu_w2, s5_w_out, ln_mix_g, ln_mix_b, ln_ffn_g, ln_ffn_b, moe_r_coarse, moe_rb_coarse, moe_r_fine, moe_rb_fine, moe_w_gate, moe_w_up, moe_w_down):
    bsz, seq, d = x.shape
    h = x.reshape(bsz * seq, d)
    bias_tiles = _bias_tiles(rel_bias)
    for layer in range(DEPTH):
        i = layer // 2
        if layer % 2 == 0:
            h = _even_mixer_ln(h, bsz, seq, ab_w_in[i], bias_tiles, gla_gate_w2[i], gla_gate_b[i],
                               gla_norm_g[i], ab_w_out[i], ln_mix_g[layer], ln_mix_b[layer])
        else:
            h = _s5_mixer_ln(h, bsz, seq, s5_w_in[i], s5_lam_re[i], s5_lam_im[i], s5_log_dt[i],
                             s5_b_re[i], s5_b_im[i], s5_c_re[i], s5_c_im[i], s5_d[i],
                             s5_glu_w1[i], s5_glu_w2[i], s5_w_out[i],
                             ln_mix_g[layer], ln_mix_b[layer])
        h = _moe_ln(h, moe_r_coarse[layer], moe_rb_coarse[layer], moe_r_fine[layer],
                    moe_rb_fine[layer], moe_w_gate[layer], moe_w_up[layer], moe_w_down[layer],
                    ln_ffn_g[layer], ln_ffn_b[layer])
    return h.reshape(bsz, seq, d).astype(x.dtype)
```

```python
import functools
import math

import numpy as np
import jax
import jax.numpy as jnp
from jax import lax
from jax.experimental import pallas as pl
from jax.experimental.pallas import tpu as pltpu

F32 = jnp.float32
BF16 = jnp.bfloat16
I32 = jnp.int32

D_MODEL = 1024
DEPTH = 2
DN_ALPHA = (2.0 * DEPTH) ** 0.25
LN_EPS = 1e-5
A_HEAD_DIM = 64
A_HEADS = 8
A_KV_HEADS = 2
IDX_HEADS = 4
IDX_DIM = 64
TOPK_MAX = 256
REL_BUCKETS = 32
REL_MAX_DIST = 128
B_HEADS = 4
B_VAL_DIM = 128
B_KEY_DIM = 64
GATE_RANK = 16
GATE_TAU = 16.0
GLA_CHUNK = 64
C_GROUP = 16
C_GROUPS = 64
C_STATE = 64
MOE_GROUPS = 4
MOE_EXPERTS = 4
MOE_HIDDEN = 512
N_EXPERTS = MOE_GROUPS * MOE_EXPERTS

LANE = 128
VMEM_LIMIT = 52 * 1024 * 1024
NEG_INF = float("-inf")
INT_MIN = -(2 ** 31)

_NT = (((1,), (1,)), ((), ()))
_TN = (((0,), (0,)), ((), ()))


def _dot(a, b):
    return jnp.dot(a, b, preferred_element_type=F32)


def _dot_nt(a, b):
    return lax.dot_general(a, b, _NT, preferred_element_type=F32)


def _dot_tn(a, b):
    return lax.dot_general(a, b, _TN, preferred_element_type=F32)


def _split3(x):
    hi = x.astype(BF16)
    r1 = x - hi.astype(F32)
    mid = r1.astype(BF16)
    lo = (r1 - mid.astype(F32)).astype(BF16)
    return hi, mid, lo


def _params(sem):
    return pltpu.CompilerParams(dimension_semantics=sem, vmem_limit_bytes=VMEM_LIMIT)


def _layer_norm(y, g, b):
    mu = jnp.mean(y, axis=-1, keepdims=True)
    d = y - mu
    var = jnp.mean(d * d, axis=-1, keepdims=True)
    return d * lax.rsqrt(var + LN_EPS) * g + b


def _sigmoid(x):
    return 1.0 / (1.0 + jnp.exp(-x))


def _bucket_thresholds():
    max_exact = REL_BUCKETS // 2
    nf = np.arange(max_exact, 4 * REL_MAX_DIST).astype(np.float32)
    large = max_exact + (np.log(nf / np.float32(max_exact))
                         / np.float32(math.log(REL_MAX_DIST / max_exact))
                         * np.float32(REL_BUCKETS - max_exact)).astype(np.int32)
    large = np.minimum(large, REL_BUCKETS - 1)
    return [int(nf[np.argmax(large >= max_exact + j)]) for j in range(1, REL_BUCKETS - max_exact)]


def _bias_kernel(rb_ref, out_ref):
    max_exact = REL_BUCKETS // 2
    srow = lax.broadcasted_iota(I32, (LANE, LANE), 0)
    tcol = lax.broadcasted_iota(I32, (LANE, LANE), 1)
    thr = _bucket_thresholds()
    for band in range(2):
        n = jnp.maximum(band * LANE + tcol - srow, 0)
        large = jnp.full((LANE, LANE), max_exact, I32)
        for t in thr:
            large = large + jnp.where(n >= t, 1, 0)
        bucket = jnp.where(n < max_exact, n, large)
        for h in range(A_HEADS):
            acc = jnp.zeros((LANE, LANE), F32)
            for bk in range(REL_BUCKETS):
                acc = acc + jnp.where(bucket == bk, rb_ref[bk, h], 0.0)
            out_ref[h, band] = acc
    for h in range(A_HEADS):
        out_ref[h, 2] = jnp.full((LANE, LANE), rb_ref[REL_BUCKETS - 1, h], F32)


def _bias_tiles(rel_bias):
    return pl.pallas_call(
        _bias_kernel,
        out_shape=jax.ShapeDtypeStruct((A_HEADS, 3, LANE, LANE), F32),
        in_specs=[pl.BlockSpec(memory_space=pltpu.SMEM)],
        name="rel_bias_tiles",
    )(rel_bias)


def _proj_kernel(x_ref, w_ref, *out_refs, widths, chunk):
    xb = x_ref[...].astype(BF16)
    c0 = 0
    for o_ref, width in zip(out_refs, widths):
        for c in range(0, width, chunk):
            ce = min(c + chunk, width)
            o_ref[:, c:ce] = _dot(xb, w_ref[:, c0 + c:c0 + ce]).astype(o_ref.dtype)
        c0 += width


def _in_proj(x2, w, widths, dtypes, tm=512, chunk=256):
    n, d = x2.shape
    return pl.pallas_call(
        functools.partial(_proj_kernel, widths=widths, chunk=chunk),
        out_shape=[jax.ShapeDtypeStruct((n, wd), dt) for wd, dt in zip(widths, dtypes)],
        grid=(n // tm,),
        in_specs=[pl.BlockSpec((tm, d), lambda i: (i, 0)),
                  pl.BlockSpec(w.shape, lambda i: (0, 0))],
        out_specs=[pl.BlockSpec((tm, wd), lambda i: (i, 0)) for wd in widths],
        compiler_params=_params(("arbitrary",)),
        name="in_proj",
    )(x2, w)


def _sortable(x):
    bits = lax.bitcast_convert_type(x, I32)
    return jnp.where(bits < 0, bits ^ jnp.int32(0x7FFFFFFF), bits)


def _dsa_kernel(q_ref, k_ref, v_ref, iq_ref, slab_ref, slabq_ref, bias_ref, o_ref,
                key_ref, am_ref, vt_ref, acc_ref, ot_ref, *, k_sel, seq):
    i = pl.program_id(1)
    t0 = i * LANE
    nkt = i + 1
    kt2 = 2 * LANE
    n2 = lax.shift_right_logical(nkt + 1, 1)
    rep = A_HEADS // A_KV_HEADS
    srow_2 = lax.broadcasted_iota(I32, (kt2, LANE), 0)
    tcol = t0 + lax.broadcasted_iota(I32, (kt2, LANE), 1)

    @pl.when(i == 0)
    def _():
        for jt in range(seq // kt2):
            vt_ref[jt] = v_ref[jt * kt2:(jt + 1) * kt2, :].astype(F32).T.astype(BF16)

    slab_t = slabq_ref[...].T
    w_t = slab_t[IDX_DIM:IDX_DIM + IDX_HEADS, :] * (IDX_HEADS ** -0.5)
    iqb = iq_ref[...].astype(BF16)
    iq_stack = jnp.concatenate([iqb[:, h * LANE:(h + 1) * LANE] for h in range(IDX_HEADS)], axis=0)

    def score_tile(j, carry):
        s0 = pl.multiple_of(j * kt2, kt2)
        ikt = slab_ref[pl.ds(s0, kt2), :].astype(BF16)
        d = _dot_nt(ikt, iq_stack) * (IDX_DIM ** -0.5)
        acc = jnp.zeros((kt2, LANE), F32)
        for h in range(IDX_HEADS):
            acc = acc + jnp.maximum(d[:, h * LANE:(h + 1) * LANE], 0.0) * w_t[h:h + 1, :]
        acc = jnp.where(s0 + srow_2 <= tcol, acc, NEG_INF)
        key_ref[pl.ds(s0, kt2), :] = _sortable(acc)
        return carry

    lax.fori_loop(0, n2, score_tile, 0)
    key_ref[pl.ds(pl.multiple_of(nkt * LANE, LANE), LANE), :] = jnp.full((LANE, LANE), INT_MIN, I32)

    def count(pred_fn):
        def body(j, c):
            s0 = pl.multiple_of(j * kt2, kt2)
            kt = key_ref[pl.ds(s0, kt2), :]
            m = jnp.where(pred_fn(kt, s0 + srow_2), 1, 0)
            return c + jnp.sum(m.reshape(kt2 // 8, 8, LANE), axis=0)
        c = lax.fori_loop(0, n2, body, jnp.zeros((8, LANE), I32))
        return jnp.sum(c, axis=0, keepdims=True)

    def bcast(v):
        return jnp.broadcast_to(v, (kt2, LANE))

    def search():
        c0 = count(lambda kt, s: kt >= 0)
        ans0 = jnp.where(c0 >= k_sel, 0, INT_MIN).astype(I32)

        def bit_body(bi, ans):
            cand = ans | lax.shift_left(jnp.int32(1), 30 - bi)
            cb = bcast(cand)
            cnt = count(lambda kt, s: kt >= cb)
            return jnp.where(cnt >= k_sel, cand, ans)

        ans = lax.fori_loop(0, 31, bit_body, ans0)
        ab = bcast(ans)
        cnt_ge = count(lambda kt, s: kt >= ab)
        cnt_gt = count(lambda kt, s: kt > ab)
        need = k_sel - cnt_gt

        def tie_search():
            def idx_body(bi, x):
                cand = x | lax.shift_left(jnp.int32(1), (seq.bit_length() - 2) - bi)
                cb = bcast(cand)
                f = count(lambda kt, s: (kt == ab) & (s < cb))
                return jnp.where(f < need, cand, x)
            return lax.fori_loop(0, seq.bit_length() - 1, idx_body, jnp.zeros((1, LANE), I32))

        cut = lax.cond(jnp.max(cnt_ge) > k_sel, tie_search,
                       lambda: jnp.full((1, LANE), seq - 1, I32))
        return ans, cut

    ans, cut = lax.cond(t0 >= k_sel, search,
                        lambda: (jnp.full((1, LANE), INT_MIN, I32),
                                 jnp.full((1, LANE), seq - 1, I32)))
    ans_b = bcast(ans)
    cut_b = bcast(cut)

    def mask_tile(j, carry):
        s0 = pl.multiple_of(j * kt2, kt2)
        kt = key_ref[pl.ds(s0, kt2), :]
        srow = s0 + srow_2
        sel = (kt > ans_b) | ((kt == ans_b) & (srow <= cut_b))
        am_ref[pl.ds(s0, kt2), :] = jnp.where(sel & (srow <= tcol), 0.0, NEG_INF)
        return carry

    lax.fori_loop(0, n2, mask_tile, 0)

    scale = jnp.asarray(A_HEAD_DIM ** -0.5, BF16)
    q_stacks = [jnp.concatenate([q_ref[:, (g * rep + r) * LANE:(g * rep + r + 1) * LANE]
                                 for r in range(rep)], axis=0) * scale for g in range(A_KV_HEADS)]
    acc_ref[...] = jnp.zeros(acc_ref.shape, F32)

    def tile_body(j, carry):
        ms, ls = carry
        s0 = pl.multiple_of(j * kt2, kt2)
        am = am_ref[pl.ds(s0, kt2), :]
        band0 = jnp.clip(i - 2 * j, 0, 2)
        band1 = jnp.clip(i - 2 * j - 1, 0, 2)
        new_ms, new_ls = [], []
        for g in range(A_KV_HEADS):
            kt = k_ref[pl.ds(s0, kt2), g * LANE:(g + 1) * LANE]
            lg_all = _dot_nt(kt, q_stacks[g])
            ps, alphas = [], []
            for r in range(rep):
                h = g * rep + r
                bias = jnp.concatenate([bias_ref[h, band0], bias_ref[h, band1]], axis=0)
                lg = lg_all[:, r * LANE:(r + 1) * LANE] + bias + am
                m_new = jnp.maximum(ms[h], jnp.max(lg, axis=0, keepdims=True))
                m_safe = jnp.where(m_new == NEG_INF, 0.0, m_new)
                alpha = jnp.exp(ms[h] - m_safe)
                p = jnp.exp(lg - m_safe)
                new_ms.append(m_new)
                new_ls.append(alpha * ls[h] + jnp.sum(p, axis=0, keepdims=True))
                ps.append(p.astype(BF16))
                alphas.append(alpha)
            vt = vt_ref[j, g * A_HEAD_DIM:(g + 1) * A_HEAD_DIM, :]
            pv = _dot(vt, jnp.concatenate(ps, axis=1))
            acc_ref[g] = acc_ref[g] * jnp.concatenate(alphas, axis=1) + pv
        return tuple(new_ms), tuple(new_ls)

    init = (tuple(jnp.full((1, LANE), NEG_INF, F32) for _ in range(A_HEADS)),
            tuple(jnp.zeros((1, LANE), F32) for _ in range(A_HEADS)))
    _, ls = lax.fori_loop(0, n2, tile_body, init)
    for g in range(A_KV_HEADS):
        for r in range(rep):
            h = g * rep + r
            ot_ref[h * A_HEAD_DIM:(h + 1) * A_HEAD_DIM, :] = (
                acc_ref[g, :, r * LANE:(r + 1) * LANE] / ls[h])
    o_ref[...] = ot_ref[...].T.astype(o_ref.dtype)


def _dsa(h_a, h_f, bias_tiles, bsz, seq):
    n = bsz * seq
    nq = seq // LANE
    k_sel = min(TOPK_MAX, seq // 4)
    assert k_sel % LANE == 0 and seq % (2 * LANE) == 0
    qw = A_HEADS * LANE
    slab_blk = (h_f.shape[1] - LANE) // LANE
    return pl.pallas_call(
        functools.partial(_dsa_kernel, k_sel=k_sel, seq=seq),
        out_shape=jax.ShapeDtypeStruct((n, A_HEADS * A_HEAD_DIM), BF16),
        grid=(bsz, nq),
        in_specs=[
            pl.BlockSpec((LANE, qw), lambda b, i: (b * nq + i, 0)),
            pl.BlockSpec((seq, A_KV_HEADS * LANE), lambda b, i: (b, qw // (A_KV_HEADS * LANE))),
            pl.BlockSpec((seq, LANE), lambda b, i: (b, (qw + A_KV_HEADS * LANE) // LANE)),
            pl.BlockSpec((LANE, IDX_HEADS * LANE), lambda b, i: (b * nq + i, 0)),
            pl.BlockSpec((seq, LANE), lambda b, i: (b, slab_blk)),
            pl.BlockSpec((LANE, LANE), lambda b, i: (b * nq + i, slab_blk)),
            pl.BlockSpec((A_HEADS, 3, LANE, LANE), lambda b, i: (0, 0, 0, 0)),
        ],
        out_specs=pl.BlockSpec((LANE, A_HEADS * A_HEAD_DIM), lambda b, i: (b * nq + i, 0)),
        scratch_shapes=[
            pltpu.VMEM((seq + LANE, LANE), I32),
            pltpu.VMEM((seq, LANE), F32),
            pltpu.VMEM((nq // 2, LANE, 2 * LANE), BF16),
            pltpu.VMEM((A_KV_HEADS, A_HEAD_DIM, (A_HEADS // A_KV_HEADS) * LANE), F32),
            pltpu.VMEM((A_HEADS * A_HEAD_DIM, LANE), F32),
        ],
        compiler_params=_params(("arbitrary", "arbitrary")),
        name="dsa_attention",
    )(h_a, h_a, h_a, h_f, h_f, h_f, bias_tiles)


def _gla_kernel(bq_ref, bk_ref, bv_ref, br_ref, slab_ref, w2_ref, gb_ref, ng_ref, o_ref,
                st_ref, *, n_chunks):
    @pl.when(pl.program_id(1) == 0)
    def _():
        st_ref[...] = jnp.zeros(st_ref.shape, F32)

    ch = GLA_CHUNK
    row = lax.broadcasted_iota(I32, (ch, ch), 0)
    col = lax.broadcasted_iota(I32, (ch, ch), 1)
    tril = row >= col
    tri = jnp.where(tril, 1.0, 0.0).astype(BF16)
    w2 = w2_ref[...]
    for c in range(n_chunks):
        rows = slice(c * ch, (c + 1) * ch)
        gate = _dot(slab_ref[rows, :].astype(BF16), w2) + gb_ref[...]
        log_a = (jnp.minimum(gate, 0.0) - jnp.log1p(jnp.exp(-jnp.abs(gate)))) / GATE_TAU
        hi, mid, lo = _split3(log_a)
        cum = _dot(tri, hi) + _dot(tri, mid) + _dot(tri, lo)
        last = cum[ch - 1:ch, :]
        q = bq_ref[rows, :] * (B_KEY_DIM ** -0.5)
        k = bk_ref[rows, :]
        q_dec = (q * jnp.exp(cum)).astype(BF16)
        k_inv = (k * jnp.exp(-cum)).astype(BF16)
        k_end = (k * jnp.exp(last - cum)).astype(BF16)
        decay = jnp.exp(last)
        for h in range(B_HEADS):
            cs = slice(h * LANE, (h + 1) * LANE)
            v = bv_ref[rows, cs].astype(BF16)
            sc = jnp.where(tril, _dot_nt(q_dec[:, cs], k_inv[:, cs]), 0.0)
            st = st_ref[h]
            o = _dot(sc.astype(BF16), v) + _dot_nt(q_dec[:, cs], st.astype(BF16))
            st_ref[h] = decay[:, cs] * st + _dot_tn(v, k_end[:, cs])
            mu = jnp.mean(o, axis=-1, keepdims=True)
            dlt = o - mu
            var = jnp.mean(dlt * dlt, axis=-1, keepdims=True)
            on = dlt * lax.rsqrt(var + LN_EPS) * ng_ref[:, cs]
            r = br_ref[rows, cs]
            o_ref[rows, cs] = (on * (r * _sigmoid(r))).astype(o_ref.dtype)


def _gla(h_f, w2p, gbp, norm_g, bsz, seq, rows_per_step=256):
    n = bsz * seq
    hw = B_HEADS * LANE
    steps = seq // rows_per_step
    slab_blk = (h_f.shape[1] - LANE) // LANE
    blk = lambda cb: pl.BlockSpec((rows_per_step, hw), lambda b, s, cb=cb: (b * steps + s, cb))
    return pl.pallas_call(
        functools.partial(_gla_kernel, n_chunks=rows_per_step // GLA_CHUNK),
        out_shape=jax.ShapeDtypeStruct((n, hw), BF16),
        grid=(bsz, steps),
        in_specs=[blk(1), blk(2), blk(3), blk(4),
                  pl.BlockSpec((rows_per_step, LANE), lambda b, s: (b * steps + s, slab_blk)),
                  pl.BlockSpec((LANE, hw), lambda b, s: (0, 0)),
                  pl.BlockSpec((1, hw), lambda b, s: (0, 0)),
                  pl.BlockSpec((1, hw), lambda b, s: (0, 0))],
        out_specs=pl.BlockSpec((rows_per_step, hw), lambda b, s: (b * steps + s, 0)),
        scratch_shapes=[pltpu.VMEM((B_HEADS, LANE, LANE), F32)],
        compiler_params=_params(("arbitrary", "arbitrary")),
        name="gla_attention",
    )(h_f, h_f, h_f, h_f, h_f, w2p, gbp, norm_g)


def _outproj_ln_kernel(oa_ref, ob_ref, x_ref, w_ref, g_ref, b_ref, out_ref):
    ka = oa_ref.shape[1]
    m = _dot(oa_ref[...], w_ref[0:ka, :]) + _dot(ob_ref[...], w_ref[ka:, :])
    out_ref[...] = _layer_norm(DN_ALPHA * x_ref[...] + m, g_ref[...], b_ref[...])


def _outproj_ln(o_a, o_b, x2, w, g, b, tm=512):
    n, d = x2.shape
    row = lambda width: pl.BlockSpec((tm, width), lambda i: (i, 0))
    full = lambda a: pl.BlockSpec(a.shape, lambda i: (0, 0))
    return pl.pallas_call(
        _outproj_ln_kernel,
        out_shape=jax.ShapeDtypeStruct((n, d), F32),
        grid=(n // tm,),
        in_specs=[row(o_a.shape[1]), row(o_b.shape[1]), row(d), full(w), full(g), full(b)],
        out_specs=row(d),
        compiler_params=_params(("arbitrary",)),
        name="out_proj_ln",
    )(o_a, o_b, x2, w, g, b)


def _route(lt):
    gl = [lt[g:g + 1, :] for g in range(MOE_GROUPS)]
    best, gsel = gl[0], jnp.zeros_like(gl[0], dtype=I32)
    for g in range(1, MOE_GROUPS):
        better = gl[g] > best
        gsel = jnp.where(better, g, gsel)
        best = jnp.where(better, gl[g], best)
    denom = sum(jnp.exp(x - best) for x in gl)
    g_w = 1.0 / denom
    fl = []
    for e in range(MOE_EXPERTS):
        acc = jnp.zeros_like(best)
        for g in range(MOE_GROUPS):
            r = MOE_GROUPS + g * MOE_EXPERTS + e
            acc = jnp.where(gsel == g, lt[r:r + 1, :], acc)
        fl.append(acc)
    v1, i1 = fl[0], jnp.zeros_like(gsel)
    for e in range(1, MOE_EXPERTS):
        better = fl[e] > v1
        i1 = jnp.where(better, e, i1)
        v1 = jnp.where(better, fl[e], v1)
    v2, i2 = jnp.full_like(v1, NEG_INF), jnp.zeros_like(gsel)
    for e in range(MOE_EXPERTS):
        better = (fl[e] > v2) & (i1 != e)
        i2 = jnp.where(better, e, i2)
        v2 = jnp.where(better, fl[e], v2)
    e2 = jnp.exp(v2 - v1)
    w1 = (1.0 / (1.0 + e2)) * g_w
    w2 = (e2 / (1.0 + e2)) * g_w
    gates = []
    for g in range(MOE_GROUPS):
        for e in range(MOE_EXPERTS):
            ge = jnp.where(i1 == e, w1, 0.0) + jnp.where(i2 == e, w2, 0.0)
            gates.append(jnp.where(gsel == g, ge, 0.0))
    return gates


def _moe_dense_kernel(x_ref, wr_ref, rb_ref, wg_ref, wu_ref, wd_ref, lg_ref, lb_ref, out_ref,
                      gate_ref, acc_ref):
    e = pl.program_id(1)
    tm = x_ref.shape[0]

    @pl.when(e == 0)
    def _():
        x = x_ref[...]
        xs = _split3(x)
        ws = _split3(wr_ref[...])
        lt = jnp.zeros((wr_ref.shape[0], tm), F32)
        for a, b in ((2, 0), (0, 2), (1, 1), (1, 0), (0, 1), (0, 0)):
            lt = lt + _dot_nt(ws[a], xs[b])
        lt = lt + rb_ref[...]
        gates = _route(lt)
        erow = lax.broadcasted_iota(I32, (LANE, tm), 0)
        gt = jnp.zeros((LANE, tm), F32)
        for ee in range(N_EXPERTS):
            gt = jnp.where(erow == ee, gates[ee], gt)
        gt = gt.T
        for ee in range(N_EXPERTS):
            gate_ref[ee] = jnp.broadcast_to(gt[:, ee:ee + 1], (tm, LANE))
        acc_ref[...] = jnp.zeros(acc_ref.shape, F32)

    xb = x_ref[...].astype(BF16)
    hg = _dot(xb, wg_ref[...])
    hu = _dot(xb, wu_ref[...])
    gcol = gate_ref[e]
    hid = hg * _sigmoid(hg) * hu * jnp.concatenate([gcol] * (hg.shape[1] // LANE), axis=1)
    acc_ref[...] += _dot(hid.astype(BF16), wd_ref[...])

    @pl.when(e == N_EXPERTS - 1)
    def _():
        out_ref[...] = _layer_norm(DN_ALPHA * x_ref[...] + acc_ref[...], lg_ref[...], lb_ref[...])


def _moe(h2, wr, rb, wg, wu, wd, lg, lb, tm=1024):
    n, d = h2.shape
    f = wg.shape[2]
    tm = min(tm, n)
    full = lambda a: pl.BlockSpec(a.shape, lambda i, e: (0,) * a.ndim)
    return pl.pallas_call(
        _moe_dense_kernel,
        out_shape=jax.ShapeDtypeStruct((n, d), F32),
        grid=(n // tm, N_EXPERTS),
        in_specs=[pl.BlockSpec((tm, d), lambda i, e: (i, 0)), full(wr), full(rb),
                  pl.BlockSpec((None, d, f), lambda i, e: (e, 0, 0)),
                  pl.BlockSpec((None, d, f), lambda i, e: (e, 0, 0)),
                  pl.BlockSpec((None, f, d), lambda i, e: (e, 0, 0)),
                  full(lg), full(lb)],
        out_specs=pl.BlockSpec((tm, d), lambda i, e: (i, 0)),
        scratch_shapes=[pltpu.VMEM((N_EXPERTS, tm, LANE), F32), pltpu.VMEM((tm, d), F32)],
        compiler_params=_params(("arbitrary", "arbitrary")),
        name="moe_ffn",
    )(h2, wr, rb, wg, wu, wd, lg, lb)


def _s5_prep_kernel(lre_ref, lim_ref, ldt_ref, bre_ref, bim_ref, are_ref, aim_ref, bbre_ref,
                    bbim_ref):
    lr = jnp.minimum(lre_ref[...], -1e-4)
    li = lim_ref[...]
    dt = jnp.exp(ldt_ref[...])
    mag = jnp.exp(lr * dt)
    ab_re = mag * jnp.cos(li * dt)
    ab_im = mag * jnp.sin(li * dt)
    den = lr * lr + li * li
    nr = ab_re - 1.0
    coef_re = (nr * lr + ab_im * li) / den
    coef_im = (ab_im * lr - nr * li) / den
    are_ref[...] = ab_re
    aim_ref[...] = ab_im
    bbre_ref[...] = coef_re * bre_ref[...] - coef_im * bim_ref[...]
    bbim_ref[...] = coef_re * bim_ref[...] + coef_im * bre_ref[...]


def _s5_prep(lam_re, lam_im, log_dt, b_re, b_im):
    gp = C_GROUPS * C_STATE
    col = lambda a: a.reshape(gp, 1)
    ldt = jnp.broadcast_to(log_dt[:, None], (C_GROUPS, C_STATE))
    outs = pl.pallas_call(
        _s5_prep_kernel,
        out_shape=[jax.ShapeDtypeStruct((gp, 1), F32), jax.ShapeDtypeStruct((gp, 1), F32),
                   jax.ShapeDtypeStruct((gp, C_GROUP), F32), jax.ShapeDtypeStruct((gp, C_GROUP), F32)],
        name="s5_discretise",
    )(col(lam_re), col(lam_im), col(ldt), b_re.reshape(gp, C_GROUP), b_im.reshape(gp, C_GROUP))
    ab_re, ab_im, bb_re, bb_im = outs
    return (ab_re.reshape(C_GROUPS, C_STATE), ab_im.reshape(C_GROUPS, C_STATE),
            bb_re.reshape(C_GROUPS, C_STATE, C_GROUP), bb_im.reshape(C_GROUPS, C_STATE, C_GROUP))


def _s5_uproj_kernel(x_ref, w_ref, o_ref):
    o_ref[...] = _dot(x_ref[...].astype(BF16), w_ref[...])


def _s5_uproj(h2, w, bsz, seq, tm=512):
    n, d = h2.shape
    steps = seq // tm
    out = pl.pallas_call(
        _s5_uproj_kernel,
        out_shape=jax.ShapeDtypeStruct((seq, bsz * d), F32),
        grid=(bsz, steps),
        in_specs=[pl.BlockSpec((tm, d), lambda b, s: (b * steps + s, 0)),
                  pl.BlockSpec(w.shape, lambda b, s: (0, 0))],
        out_specs=pl.BlockSpec((tm, d), lambda b, s: (s, b)),
        compiler_params=_params(("arbitrary", "arbitrary")),
        name="s5_in_proj",
    )(h2, w)
    return out.reshape(seq * bsz, d)


def _gelu_tanh(x):
    return 0.5 * x * (1.0 + jnp.tanh(math.sqrt(2.0 / math.pi) * (x + 0.044715 * (x * x * x))))


def _s5_scan_kernel(u_ref, bbre_ref, bbim_ref, are_ref, aim_ref, cm_ref, d_ref, g_ref,
                    st_ref, bure_ref, buim_ref, s_ref, *, bsz, tl, n_cb, sub):
    @pl.when(pl.program_id(0) == 0)
    def _():
        st_ref[...] = jnp.zeros(st_ref.shape, F32)

    cw = D_MODEL // n_cb
    sw = bure_ref.shape[1]
    for cb in range(n_cb):
        ch = slice(cb * cw, (cb + 1) * cw)
        u = u_ref[:, ch]
        ub = u.astype(BF16)
        bure_ref[...] = _dot(ub, bbre_ref[cb])
        buim_ref[...] = _dot(ub, bbim_ref[cb])
        for s0 in range(0, sw, sub):
            cs = slice(s0, s0 + sub)
            a_re = jnp.broadcast_to(are_ref[cb, :, cs], (bsz, sub))
            a_im = jnp.broadcast_to(aim_ref[cb, :, cs], (bsz, sub))

            def step(t, carry, cs=cs, a_re=a_re, a_im=a_im):
                s_re, s_im = carry
                r0 = pl.multiple_of(t * bsz, bsz)
                n_re = a_re * s_re - a_im * s_im + bure_ref[pl.ds(r0, bsz), cs]
                n_im = a_re * s_im + a_im * s_re + buim_ref[pl.ds(r0, bsz), cs]
                s_ref[pl.ds(r0, bsz), cs] = n_re.astype(BF16)
                s_ref[pl.ds(r0, bsz), sw + s0:sw + s0 + sub] = n_im.astype(BF16)
                return n_re, n_im

            s_re, s_im = lax.fori_loop(0, tl, step, (st_ref[cb, 0, :, cs], st_ref[cb, 1, :, cs]),
                                       unroll=2)
            st_ref[cb, 0, :, cs] = s_re
            st_ref[cb, 1, :, cs] = s_im
        y = _dot(s_ref[...], cm_ref[cb]) + d_ref[:, ch] * u
        g_ref[:, ch] = _gelu_tanh(y).astype(g_ref.dtype)


def _s5_scan(u_lb, bbre, bbim, a_re, a_im, cmat, d_skip, bsz, seq, tl=32, sub=512):
    n, d = u_lb.shape
    n_cb, cw, sw = bbre.shape
    rows = tl * bsz
    full = lambda a: pl.BlockSpec(a.shape, lambda t: (0,) * a.ndim)
    return pl.pallas_call(
        functools.partial(_s5_scan_kernel, bsz=bsz, tl=tl, n_cb=n_cb, sub=sub),
        out_shape=jax.ShapeDtypeStruct((n, d), BF16),
        grid=(seq // tl,),
        in_specs=[pl.BlockSpec((rows, d), lambda t: (t, 0)), full(bbre), full(bbim), full(a_re),
                  full(a_im), full(cmat), full(d_skip)],
        out_specs=pl.BlockSpec((rows, d), lambda t: (t, 0)),
        scratch_shapes=[pltpu.VMEM((n_cb, 2, bsz, sw), F32), pltpu.VMEM((rows, sw), F32),
                        pltpu.VMEM((rows, sw), F32), pltpu.VMEM((rows, 2 * sw), BF16)],
        compiler_params=_params(("arbitrary",)),
        name="s5_scan",
    )(u_lb, bbre, bbim, a_re, a_im, cmat, d_skip)


def _s5_glu_kernel(g_ref, x_ref, w1_ref, w2_ref, wo_ref, lg_ref, lb_ref, out_ref):
    g = g_ref[...]
    z = _dot(g, w1_ref[...]) * _sigmoid(_dot(g, w2_ref[...]))
    m = _dot(z.astype(BF16), wo_ref[...])
    out_ref[...] = _layer_norm(DN_ALPHA * x_ref[...] + m, lg_ref[...], lb_ref[...])


def _s5_glu(g_lb, h2, w1, w2, wo, lg, lb, bsz, seq, tm=512):
    n, d = h2.shape
    steps = seq // tm
    full = lambda a: pl.BlockSpec(a.shape, lambda b, s: (0, 0))
    return pl.pallas_call(
        _s5_glu_kernel,
        out_shape=jax.ShapeDtypeStruct((n, d), F32),
        grid=(bsz, steps),
        in_specs=[pl.BlockSpec((tm, d), lambda b, s: (s, b)),
                  pl.BlockSpec((tm, d), lambda b, s: (b * steps + s, 0)),
                  full(w1), full(w2), full(wo), full(lg), full(lb)],
        out_specs=pl.BlockSpec((tm, d), lambda b, s: (b * steps + s, 0)),
        compiler_params=_params(("arbitrary", "arbitrary")),
        name="s5_glu_out_ln",
    )(g_lb.reshape(seq, bsz * d), h2, w1, w2, wo, lg, lb)


def _pad_heads(w, heads, dim):
    d = w.shape[0]
    w = w.reshape(d, heads, dim)
    return jnp.pad(w, ((0, 0), (0, 0), (0, LANE - dim))).reshape(d, heads * LANE)


def _even_layer_weights(w_in, gate_w2, gate_b):
    splits = (A_HEADS * A_HEAD_DIM, A_KV_HEADS * A_HEAD_DIM, A_KV_HEADS * A_HEAD_DIM,
              IDX_HEADS * IDX_DIM, IDX_DIM, IDX_HEADS,
              B_HEADS * B_KEY_DIM, B_HEADS * B_KEY_DIM, B_HEADS * B_VAL_DIM, GATE_RANK,
              B_HEADS * B_VAL_DIM)
    offs = np.cumsum((0,) + splits)
    aq, ak, av, iq, ik, iw, bq, bk, bv, bg, br = [w_in[:, offs[k]:offs[k + 1]] for k in range(11)]
    d = w_in.shape[0]
    slab = jnp.concatenate(
        [ik, iw, bg, jnp.zeros((d, LANE - IDX_DIM - IDX_HEADS - GATE_RANK), w_in.dtype)], axis=1)
    w_a = jnp.concatenate([_pad_heads(aq, A_HEADS, A_HEAD_DIM),
                           _pad_heads(ak, A_KV_HEADS, A_HEAD_DIM), av], axis=1)
    w_f = jnp.concatenate([_pad_heads(iq, IDX_HEADS, IDX_DIM), _pad_heads(bq, B_HEADS, B_KEY_DIM),
                           _pad_heads(bk, B_HEADS, B_KEY_DIM), bv, br, slab], axis=1)
    w_all = jnp.concatenate([w_a, w_f], axis=1).astype(BF16)
    g0 = IDX_DIM + IDX_HEADS
    w2p = jnp.zeros((LANE, B_HEADS * LANE), F32).at[g0:g0 + GATE_RANK].set(
        _pad_heads(gate_w2, B_HEADS, B_KEY_DIM)).astype(BF16)
    gbp = _pad_heads(gate_b[None, :], B_HEADS, B_KEY_DIM)
    return w_all, (w_a.shape[1], w_f.shape[1]), w2p, gbp


def _even_mixer_ln(x2, bsz, seq, w_in, bias_tiles, gate_w2, gate_b, norm_g, w_out, ln_g, ln_b):
    w_all, widths, w2p, gbp = _even_layer_weights(w_in, gate_w2, gate_b)
    h_a, h_f = _in_proj(x2, w_all, widths, (BF16, F32))
    o_a = _dsa(h_a, h_f, bias_tiles, bsz, seq)
    o_b = _gla(h_f, w2p, gbp, norm_g[None, :], bsz, seq)
    return _outproj_ln(o_a, o_b, x2, w_out.astype(BF16), ln_g[None, :], ln_b[None, :])


def _block_diag(m, per):
    g, r, c = m.shape
    eye = jnp.eye(per, dtype=m.dtype)
    m = m.reshape(g // per, per, r, c)
    bd = m[:, :, :, None, :] * eye[None, :, None, :, None]
    return bd.reshape(g // per, per * r, per * c)


def _s5_mixer_ln(h2, bsz, seq, w_in, lam_re, lam_im, log_dt, b_re, b_im, c_re, c_im, d_skip,
                 glu_w1, glu_w2, w_out, ln_g, ln_b, groups_per_block=16):
    ab_re, ab_im, bb_re, bb_im = _s5_prep(lam_re, lam_im, log_dt, b_re, b_im)
    per = groups_per_block
    n_cb = C_GROUPS // per
    bbre = _block_diag(jnp.swapaxes(bb_re, 1, 2), per).astype(BF16)
    bbim = _block_diag(jnp.swapaxes(bb_im, 1, 2), per).astype(BF16)
    a_re = ab_re.reshape(n_cb, 1, per * C_STATE)
    a_im = ab_im.reshape(n_cb, 1, per * C_STATE)
    cre = _block_diag(jnp.swapaxes(c_re, 1, 2), per)
    cim = _block_diag(jnp.swapaxes(c_im, 1, 2), per)
    cmat = jnp.concatenate([cre, -cim], axis=1).astype(BF16)
    u_lb = _s5_uproj(h2, w_in.astype(BF16), bsz, seq)
    g_lb = _s5_scan(u_lb, bbre, bbim, a_re, a_im, cmat, d_skip[None, :], bsz, seq)
    return _s5_glu(g_lb, h2, glu_w1.astype(BF16), glu_w2.astype(BF16), w_out.astype(BF16),
                   ln_g[None, :], ln_b[None, :], bsz, seq)


def _moe_ln(h, r_coarse, rb_coarse, r_fine, rb_fine, w_gate, w_up, w_down, ln_g, ln_b):
    d = h.shape[1]
    f = w_gate.shape[-1]
    rows = MOE_GROUPS + N_EXPERTS
    wr = jnp.concatenate([r_coarse.T, jnp.transpose(r_fine, (0, 2, 1)).reshape(N_EXPERTS, d),
                          jnp.zeros((32 - rows, d), F32)], axis=0)
    rb = jnp.concatenate([rb_coarse, rb_fine.reshape(N_EXPERTS), jnp.zeros((32 - rows,), F32)])
    return _moe(h, wr, rb[:, None], w_gate.reshape(N_EXPERTS, d, f).astype(BF16),
                w_up.reshape(N_EXPERTS, d, f).astype(BF16),
                w_down.reshape(N_EXPERTS, f, d).astype(BF16), ln_g[None, :], ln_b[None, :])


def kernel(x, rel_bias, ab_w_in, gla_gate_w2, gla_gate_b, gla_norm_g, ab_w_out, s5_w_in, s5_lam_re, s5_lam_im, s5_log_dt, s5_b_re, s5_b_im, s5_c_re, s5_c_im, s5_d, s5_glu_w1, s5_glu_w2, s5_w_out, ln_mix_g, ln_mix_b, ln_ffn_g, ln_ffn_b, moe_r_coarse, moe_rb_coarse, moe_r_fine, moe_rb_fine, moe_w_gate, moe_w_up, moe_w_down):
    bsz, seq, d = x.shape
    h = x.reshape(bsz * seq, d)
    bias_tiles = _bias_tiles(rel_bias)
    for layer in range(DEPTH):
        i = layer // 2
        if layer % 2 == 0:
            h = _even_mixer_ln(h, bsz, seq, ab_w_in[i], bias_tiles, gla_gate_w2[i], gla_gate_b[i],
                               gla_norm_g[i], ab_w_out[i], ln_mix_g[layer], ln_mix_b[layer])
        else:
            h = _s5_mixer_ln(h, bsz, seq, s5_w_in[i], s5_lam_re[i], s5_lam_im[i], s5_log_dt[i],
                             s5_b_re[i], s5_b_im[i], s5_c_re[i], s5_c_im[i], s5_d[i],
                             s5_glu_w1[i], s5_glu_w2[i], s5_w_out[i],
                             ln_mix_g[layer], ln_mix_b[layer])
        h = _moe_ln(h, moe_r_coarse[layer], moe_rb_coarse[layer], moe_r_fine[layer],
                    moe_rb_fine[layer], moe_w_gate[layer], moe_w_up[layer], moe_w_down[layer],
                    ln_ffn_g[layer], ln_ffn_b[layer])
    return h.reshape(bsz, seq, d).astype(x.dtype)
```

```python
import functools
import math

import numpy as np
import jax
import jax.numpy as jnp
from jax import lax
from jax.experimental import pallas as pl
from jax.experimental.pallas import tpu as pltpu

F32 = jnp.float32
BF16 = jnp.bfloat16
I32 = jnp.int32

D_MODEL = 1024
DEPTH = 2
DN_ALPHA = (2.0 * DEPTH) ** 0.25
LN_EPS = 1e-5
A_HEAD_DIM = 64
A_HEADS = 8
A_KV_HEADS = 2
IDX_HEADS = 4
IDX_DIM = 64
TOPK_MAX = 256
REL_BUCKETS = 32
REL_MAX_DIST = 128
B_HEADS = 4
B_VAL_DIM = 128
B_KEY_DIM = 64
GATE_RANK = 16
GATE_TAU = 16.0
GLA_CHUNK = 64
C_GROUP = 16
C_GROUPS = 64
C_STATE = 64
MOE_GROUPS = 4
MOE_EXPERTS = 4
MOE_HIDDEN = 512
N_EXPERTS = MOE_GROUPS * MOE_EXPERTS

LANE = 128
VMEM_LIMIT = 52 * 1024 * 1024
NEG_INF = float("-inf")
INT_MIN = -(2 ** 31)

_NT = (((1,), (1,)), ((), ()))
_TN = (((0,), (0,)), ((), ()))


def _dot(a, b):
    return jnp.dot(a, b, preferred_element_type=F32)


def _dot_nt(a, b):
    return lax.dot_general(a, b, _NT, preferred_element_type=F32)


def _dot_tn(a, b):
    return lax.dot_general(a, b, _TN, preferred_element_type=F32)


def _split3(x):
    hi = x.astype(BF16)
    r1 = x - hi.astype(F32)
    mid = r1.astype(BF16)
    lo = (r1 - mid.astype(F32)).astype(BF16)
    return hi, mid, lo


def _params(sem):
    return pltpu.CompilerParams(dimension_semantics=sem, vmem_limit_bytes=VMEM_LIMIT)


def _layer_norm(y, g, b):
    mu = jnp.mean(y, axis=-1, keepdims=True)
    d = y - mu
    var = jnp.mean(d * d, axis=-1, keepdims=True)
    return d * lax.rsqrt(var + LN_EPS) * g + b


def _sigmoid(x):
    return 1.0 / (1.0 + jnp.exp(-x))


def _bucket_thresholds():
    max_exact = REL_BUCKETS // 2
    nf = np.arange(max_exact, 4 * REL_MAX_DIST).astype(np.float32)
    large = max_exact + (np.log(nf / np.float32(max_exact))
                         / np.float32(math.log(REL_MAX_DIST / max_exact))
                         * np.float32(REL_BUCKETS - max_exact)).astype(np.int32)
    large = np.minimum(large, REL_BUCKETS - 1)
    return [int(nf[np.argmax(large >= max_exact + j)]) for j in range(1, REL_BUCKETS - max_exact)]


def _bias_kernel(rb_ref, out_ref):
    max_exact = REL_BUCKETS // 2
    srow = lax.broadcasted_iota(I32, (LANE, LANE), 0)
    tcol = lax.broadcasted_iota(I32, (LANE, LANE), 1)
    thr = _bucket_thresholds()
    for band in range(2):
        n = jnp.maximum(band * LANE + tcol - srow, 0)
        large = jnp.full((LANE, LANE), max_exact, I32)
        for t in thr:
            large = large + jnp.where(n >= t, 1, 0)
        bucket = jnp.where(n < max_exact, n, large)
        for h in range(A_HEADS):
            acc = jnp.zeros((LANE, LANE), F32)
            for bk in range(REL_BUCKETS):
                acc = acc + jnp.where(bucket == bk, rb_ref[bk, h], 0.0)
            out_ref[h, band] = acc
    for h in range(A_HEADS):
        out_ref[h, 2] = jnp.full((LANE, LANE), rb_ref[REL_BUCKETS - 1, h], F32)


def _bias_tiles(rel_bias):
    return pl.pallas_call(
        _bias_kernel,
        out_shape=jax.ShapeDtypeStruct((A_HEADS, 3, LANE, LANE), F32),
        in_specs=[pl.BlockSpec(memory_space=pltpu.SMEM)],
        name="rel_bias_tiles",
    )(rel_bias)


def _proj_kernel(x_ref, w_ref, *out_refs, widths, chunk):
    xb = x_ref[...].astype(BF16)
    c0 = 0
    for o_ref, width in zip(out_refs, widths):
        for c in range(0, width, chunk):
            ce = min(c + chunk, width)
            o_ref[:, c:ce] = _dot(xb, w_ref[:, c0 + c:c0 + ce]).astype(o_ref.dtype)
        c0 += width


def _in_proj(x2, w, widths, dtypes, tm=512, chunk=256):
    n, d = x2.shape
    return pl.pallas_call(
        functools.partial(_proj_kernel, widths=widths, chunk=chunk),
        out_shape=[jax.ShapeDtypeStruct((n, wd), dt) for wd, dt in zip(widths, dtypes)],
        grid=(n // tm,),
        in_specs=[pl.BlockSpec((tm, d), lambda i: (i, 0)),
                  pl.BlockSpec(w.shape, lambda i: (0, 0))],
        out_specs=[pl.BlockSpec((tm, wd), lambda i: (i, 0)) for wd in widths],
        compiler_params=_params(("arbitrary",)),
        name="in_proj",
    )(x2, w)


def _sortable(x):
    bits = lax.bitcast_convert_type(x, I32)
    return jnp.where(bits < 0, bits ^ jnp.int32(0x7FFFFFFF), bits)


def _dsa_kernel(q_ref, k_ref, v_ref, iq_ref, slab_ref, slabq_ref, bias_ref, o_ref,
                key_ref, am_ref, vt_ref, acc_ref, ot_ref, *, k_sel, seq):
    i = pl.program_id(1)
    t0 = i * LANE
    nkt = i + 1
    kt2 = 2 * LANE
    n2 = lax.shift_right_logical(nkt + 1, 1)
    rep = A_HEADS // A_KV_HEADS
    srow_2 = lax.broadcasted_iota(I32, (kt2, LANE), 0)
    tcol = t0 + lax.broadcasted_iota(I32, (kt2, LANE), 1)

    @pl.when(i == 0)
    def _():
        for jt in range(seq // kt2):
            vt_ref[jt] = v_ref[jt * kt2:(jt + 1) * kt2, :].astype(F32).T.astype(BF16)

    slab_t = slabq_ref[...].T
    w_t = slab_t[IDX_DIM:IDX_DIM + IDX_HEADS, :] * (IDX_HEADS ** -0.5)
    iqb = iq_ref[...].astype(BF16)
    iq_stack = jnp.concatenate([iqb[:, h * LANE:(h + 1) * LANE] for h in range(IDX_HEADS)], axis=0)

    def score_tile(j, carry):
        s0 = pl.multiple_of(j * kt2, kt2)
        ikt = slab_ref[pl.ds(s0, kt2), :].astype(BF16)
        d = _dot_nt(ikt, iq_stack) * (IDX_DIM ** -0.5)
        acc = jnp.zeros((kt2, LANE), F32)
        for h in range(IDX_HEADS):
            acc = acc + jnp.maximum(d[:, h * LANE:(h + 1) * LANE], 0.0) * w_t[h:h + 1, :]
        acc = jnp.where(s0 + srow_2 <= tcol, acc, NEG_INF)
        key_ref[pl.ds(s0, kt2), :] = _sortable(acc)
        return carry

    lax.fori_loop(0, n2, score_tile, 0)
    key_ref[pl.ds(pl.multiple_of(nkt * LANE, LANE), LANE), :] = jnp.full((LANE, LANE), INT_MIN, I32)

    def count(pred_fn):
        def body(j, c):
            s0 = pl.multiple_of(j * kt2, kt2)
            kt = key_ref[pl.ds(s0, kt2), :]
            m = jnp.where(pred_fn(kt, s0 + srow_2), 1, 0)
            return c + jnp.sum(m.reshape(kt2 // 8, 8, LANE), axis=0)
        c = lax.fori_loop(0, n2, body, jnp.zeros((8, LANE), I32))
        return jnp.sum(c, axis=0, keepdims=True)

    def bcast(v):
        return jnp.broadcast_to(v, (kt2, LANE))

    def search():
        c0 = count(lambda kt, s: kt >= 0)
        ans0 = jnp.where(c0 >= k_sel, 0, INT_MIN).astype(I32)

        def bit_body(bi, ans):
            cand = ans | lax.shift_left(jnp.int32(1), 30 - bi)
            cb = bcast(cand)
            cnt = count(lambda kt, s: kt >= cb)
            return jnp.where(cnt >= k_sel, cand, ans)

        ans = lax.fori_loop(0, 31, bit_body, ans0)
        ab = bcast(ans)
        cnt_ge = count(lambda kt, s: kt >= ab)
        cnt_gt = count(lambda kt, s: kt > ab)
        need = k_sel - cnt_gt

        def tie_search():
            def idx_body(bi, x):
                cand = x | lax.shift_left(jnp.int32(1), (seq.bit_length() - 2) - bi)
                cb = bcast(cand)
                f = count(lambda kt, s: (kt == ab) & (s < cb))
                return jnp.where(f < need, cand, x)
            return lax.fori_loop(0, seq.bit_length() - 1, idx_body, jnp.zeros((1, LANE), I32))

        cut = lax.cond(jnp.max(cnt_ge) > k_sel, tie_search,
                       lambda: jnp.full((1, LANE), seq - 1, I32))
        return ans, cut

    ans, cut = lax.cond(t0 >= k_sel, search,
                        lambda: (jnp.full((1, LANE), INT_MIN, I32),
                                 jnp.full((1, LANE), seq - 1, I32)))
    ans_b = bcast(ans)
    cut_b = bcast(cut)

    def mask_tile(j, carry):
        s0 = pl.multiple_of(j * kt2, kt2)
        kt = key_ref[pl.ds(s0, kt2), :]
        srow = s0 + srow_2
        sel = (kt > ans_b) | ((kt == ans_b) & (srow <= cut_b))
        am_ref[pl.ds(s0, kt2), :] = jnp.where(sel & (srow <= tcol), 0.0, NEG_INF)
        return carry

    lax.fori_loop(0, n2, mask_tile, 0)

    scale = jnp.asarray(A_HEAD_DIM ** -0.5, BF16)
    q_stacks = [jnp.concatenate([q_ref[:, (g * rep + r) * LANE:(g * rep + r + 1) * LANE]
                                 for r in range(rep)], axis=0) * scale for g in range(A_KV_HEADS)]
    acc_ref[...] = jnp.zeros(acc_ref.shape, F32)

    def tile_body(j, carry):
        ms, ls = carry
        s0 = pl.multiple_of(j * kt2, kt2)
        am = am_ref[pl.ds(s0, kt2), :]
        band0 = jnp.clip(i - 2 * j, 0, 2)
        band1 = jnp.clip(i - 2 * j - 1, 0, 2)
        new_ms, new_ls = [], []
        for g in range(A_KV_HEADS):
            kt = k_ref[pl.ds(s0, kt2), g * LANE:(g + 1) * LANE]
            lg_all = _dot_nt(kt, q_stacks[g])
            ps, alphas = [], []
            for r in range(rep):
                h = g * rep + r
                bias = jnp.concatenate([bias_ref[h, band0], bias_ref[h, band1]], axis=0)
                lg = lg_all[:, r * LANE:(r + 1) * LANE] + bias + am
                m_new = jnp.maximum(ms[h], jnp.max(lg, axis=0, keepdims=True))
                m_safe = jnp.where(m_new == NEG_INF, 0.0, m_new)
                alpha = jnp.exp(ms[h] - m_safe)
                p = jnp.exp(lg - m_safe)
                new_ms.append(m_new)
                new_ls.append(alpha * ls[h] + jnp.sum(p, axis=0, keepdims=True))
                ps.append(p.astype(BF16))
                alphas.append(alpha)
            vt = vt_ref[j, g * A_HEAD_DIM:(g + 1) * A_HEAD_DIM, :]
            pv = _dot(vt, jnp.concatenate(ps, axis=1))
            acc_ref[g] = acc_ref[g] * jnp.concatenate(alphas, axis=1) + pv
        return tuple(new_ms), tuple(new_ls)

    init = (tuple(jnp.full((1, LANE), NEG_INF, F32) for _ in range(A_HEADS)),
            tuple(jnp.zeros((1, LANE), F32) for _ in range(A_HEADS)))
    _, ls = lax.fori_loop(0, n2, tile_body, init)
    for g in range(A_KV_HEADS):
        for r in range(rep):
            h = g * rep + r
            ot_ref[h * A_HEAD_DIM:(h + 1) * A_HEAD_DIM, :] = (
                acc_ref[g, :, r * LANE:(r + 1) * LANE] / ls[h])
    o_ref[...] = ot_ref[...].T.astype(o_ref.dtype)


def _dsa(h_a, h_f, bias_tiles, bsz, seq):
    n = bsz * seq
    nq = seq // LANE
    k_sel = min(TOPK_MAX, seq // 4)
    assert k_sel % LANE == 0 and seq % (2 * LANE) == 0
    qw = A_HEADS * LANE
    slab_blk = (h_f.shape[1] - LANE) // LANE
    return pl.pallas_call(
        functools.partial(_dsa_kernel, k_sel=k_sel, seq=seq),
        out_shape=jax.ShapeDtypeStruct((n, A_HEADS * A_HEAD_DIM), BF16),
        grid=(bsz, nq),
        in_specs=[
            pl.BlockSpec((LANE, qw), lambda b, i: (b * nq + i, 0)),
            pl.BlockSpec((seq, A_KV_HEADS * LANE), lambda b, i: (b, qw // (A_KV_HEADS * LANE))),
            pl.BlockSpec((seq, LANE), lambda b, i: (b, (qw + A_KV_HEADS * LANE) // LANE)),
            pl.BlockSpec((LANE, IDX_HEADS * LANE), lambda b, i: (b * nq + i, 0)),
            pl.BlockSpec((seq, LANE), lambda b, i: (b, slab_blk)),
            pl.BlockSpec((LANE, LANE), lambda b, i: (b * nq + i, slab_blk)),
            pl.BlockSpec((A_HEADS, 3, LANE, LANE), lambda b, i: (0, 0, 0, 0)),
        ],
        out_specs=pl.BlockSpec((LANE, A_HEADS * A_HEAD_DIM), lambda b, i: (b * nq + i, 0)),
        scratch_shapes=[
            pltpu.VMEM((seq + LANE, LANE), I32),
            pltpu.VMEM((seq, LANE), F32),
            pltpu.VMEM((nq // 2, LANE, 2 * LANE), BF16),
            pltpu.VMEM((A_KV_HEADS, A_HEAD_DIM, (A_HEADS // A_KV_HEADS) * LANE), F32),
            pltpu.VMEM((A_HEADS * A_HEAD_DIM, LANE), F32),
        ],
        compiler_params=_params(("arbitrary", "arbitrary")),
        name="dsa_attention",
    )(h_a, h_a, h_a, h_f, h_f, h_f, bias_tiles)


def _gla_kernel(bq_ref, bk_ref, bv_ref, br_ref, slab_ref, w2_ref, gb_ref, ng_ref, o_ref,
                st_ref, *, n_chunks):
    @pl.when(pl.program_id(1) == 0)
    def _():
        st_ref[...] = jnp.zeros(st_ref.shape, F32)

    ch = GLA_CHUNK
    row = lax.broadcasted_iota(I32, (ch, ch), 0)
    col = lax.broadcasted_iota(I32, (ch, ch), 1)
    tril = row >= col
    tri = jnp.where(tril, 1.0, 0.0).astype(BF16)
    w2 = w2_ref[...]
    for c in range(n_chunks):
        rows = slice(c * ch, (c + 1) * ch)
        gate = _dot(slab_ref[rows, :].astype(BF16), w2) + gb_ref[...]
        log_a = (jnp.minimum(gate, 0.0) - jnp.log1p(jnp.exp(-jnp.abs(gate)))) / GATE_TAU
        hi, mid, lo = _split3(log_a)
        cum = _dot(tri, hi) + _dot(tri, mid) + _dot(tri, lo)
        last = cum[ch - 1:ch, :]
        q = bq_ref[rows, :] * (B_KEY_DIM ** -0.5)
        k = bk_ref[rows, :]
        q_dec = (q * jnp.exp(cum)).astype(BF16)
        k_inv = (k * jnp.exp(-cum)).astype(BF16)
        k_end = (k * jnp.exp(last - cum)).astype(BF16)
        decay = jnp.exp(last)
        for h in range(B_HEADS):
            cs = slice(h * LANE, (h + 1) * LANE)
            v = bv_ref[rows, cs].astype(BF16)
            sc = jnp.where(tril, _dot_nt(q_dec[:, cs], k_inv[:, cs]), 0.0)
            st = st_ref[h]
            o = _dot(sc.astype(BF16), v) + _dot_nt(q_dec[:, cs], st.astype(BF16))
            st_ref[h] = decay[:, cs] * st + _dot_tn(v, k_end[:, cs])
            mu = jnp.mean(o, axis=-1, keepdims=True)
            dlt = o - mu
            var = jnp.mean(dlt * dlt, axis=-1, keepdims=True)
            on = dlt * lax.rsqrt(var + LN_EPS) * ng_ref[:, cs]
            r = br_ref[rows, cs]
            o_ref[rows, cs] = (on * (r * _sigmoid(r))).astype(o_ref.dtype)


def _gla(h_f, w2p, gbp, norm_g, bsz, seq, rows_per_step=256):
    n = bsz * seq
    hw = B_HEADS * LANE
    steps = seq // rows_per_step
    slab_blk = (h_f.shape[1] - LANE) // LANE
    blk = lambda cb: pl.BlockSpec((rows_per_step, hw), lambda b, s, cb=cb: (b * steps + s, cb))
    return pl.pallas_call(
        functools.partial(_gla_kernel, n_chunks=rows_per_step // GLA_CHUNK),
        out_shape=jax.ShapeDtypeStruct((n, hw), BF16),
        grid=(bsz, steps),
        in_specs=[blk(1), blk(2), blk(3), blk(4),
                  pl.BlockSpec((rows_per_step, LANE), lambda b, s: (b * steps + s, slab_blk)),
                  pl.BlockSpec((LANE, hw), lambda b, s: (0, 0)),
                  pl.BlockSpec((1, hw), lambda b, s: (0, 0)),
                  pl.BlockSpec((1, hw), lambda b, s: (0, 0))],
        out_specs=pl.BlockSpec((rows_per_step, hw), lambda b, s: (b * steps + s, 0)),
        scratch_shapes=[pltpu.VMEM((B_HEADS, LANE, LANE), F32)],
        compiler_params=_params(("arbitrary", "arbitrary")),
        name="gla_attention",
    )(h_f, h_f, h_f, h_f, h_f, w2p, gbp, norm_g)


def _outproj_ln_kernel(oa_ref, ob_ref, x_ref, w_ref, g_ref, b_ref, out_ref):
    ka = oa_ref.shape[1]
    m = _dot(oa_ref[...], w_ref[0:ka, :]) + _dot(ob_ref[...], w_ref[ka:, :])
    out_ref[...] = _layer_norm(DN_ALPHA * x_ref[...] + m, g_ref[...], b_ref[...])


def _outproj_ln(o_a, o_b, x2, w, g, b, tm=512):
    n, d = x2.shape
    row = lambda width: pl.BlockSpec((tm, width), lambda i: (i, 0))
    full = lambda a: pl.BlockSpec(a.shape, lambda i: (0, 0))
    return pl.pallas_call(
        _outproj_ln_kernel,
        out_shape=jax.ShapeDtypeStruct((n, d), F32),
        grid=(n // tm,),
        in_specs=[row(o_a.shape[1]), row(o_b.shape[1]), row(d), full(w), full(g), full(b)],
        out_specs=row(d),
        compiler_params=_params(("arbitrary",)),
        name="out_proj_ln",
    )(o_a, o_b, x2, w, g, b)


def _route(lt):
    gl = [lt[g:g + 1, :] for g in range(MOE_GROUPS)]
    best, gsel = gl[0], jnp.zeros_like(gl[0], dtype=I32)
    for g in range(1, MOE_GROUPS):
        better = gl[g] > best
        gsel = jnp.where(better, g, gsel)
        best = jnp.where(better, gl[g], best)
    denom = sum(jnp.exp(x - best) for x in gl)
    g_w = 1.0 / denom
    fl = []
    for e in range(MOE_EXPERTS):
        acc = jnp.zeros_like(best)
        for g in range(MOE_GROUPS):
            r = MOE_GROUPS + g * MOE_EXPERTS + e
            acc = jnp.where(gsel == g, lt[r:r + 1, :], acc)
        fl.append(acc)
    v1, i1 = fl[0], jnp.zeros_like(gsel)
    for e in range(1, MOE_EXPERTS):
        better = fl[e] > v1
        i1 = jnp.where(better, e, i1)
        v1 = jnp.where(better, fl[e], v1)
    v2, i2 = jnp.full_like(v1, NEG_INF), jnp.zeros_like(gsel)
    for e in range(MOE_EXPERTS):
        better = (fl[e] > v2) & (i1 != e)
        i2 = jnp.where(better, e, i2)
        v2 = jnp.where(better, fl[e], v2)
    e2 = jnp.exp(v2 - v1)
    w1 = (1.0 / (1.0 + e2)) * g_w
    w2 = (e2 / (1.0 + e2)) * g_w
    e_gate = [jnp.where(i1 == e, w1, 0.0) + jnp.where(i2 == e, w2, 0.0)
              for e in range(MOE_EXPERTS)]
    return gsel, e_gate


def _moe_route_kernel(x_ref, wr_ref, rb_ref, xa_ref, rt_ref, cnt_ref, tri_ref, carry_ref):
    i = pl.program_id(0)
    tm, d = x_ref.shape

    @pl.when(i == 0)
    def _():
        carry_ref[...] = jnp.zeros(carry_ref.shape, F32)
        before = (lax.broadcasted_iota(I32, (tm, tm), 0) < lax.broadcasted_iota(I32, (tm, tm), 1))
        tri_ref[...] = jnp.where(before, 1.0, 0.0).astype(BF16)

    x = x_ref[...]
    xs = _split3(x)
    ws = _split3(wr_ref[...])
    lt = jnp.zeros((wr_ref.shape[0], tm), F32)
    for a, b in ((2, 0), (0, 2), (1, 1), (1, 0), (0, 1), (0, 0)):
        lt = lt + _dot_nt(ws[a], xs[b])
    lt = lt + rb_ref[...]
    gsel, e_gate = _route(lt)

    row8 = lax.broadcasted_iota(I32, (8, tm), 0)
    onehot = jnp.where(row8 == gsel, 1.0, 0.0)
    earlier = _dot(onehot.astype(BF16), tri_ref[...])
    rank = jnp.sum(onehot * (earlier + carry_ref[:, 0:1]), axis=0, keepdims=True)
    rt_ref[...] = jnp.where(row8 == 0, gsel, jnp.where(row8 == 1, rank.astype(I32), 0))
    carry_ref[...] = carry_ref[...] + jnp.sum(onehot, axis=1, keepdims=True)
    cnt_ref[...] = carry_ref[...].astype(I32)

    erow = lax.broadcasted_iota(I32, (LANE, tm), 0)
    gt = jnp.zeros((LANE, tm), F32)
    for e in range(MOE_EXPERTS):
        gt = jnp.where(erow == e, e_gate[e], gt)
    xa_ref[:, :d] = x
    xa_ref[:, d:] = gt.T


def _moe_route(h, wr, rb, tm=1024):
    n, d = h.shape
    tm = min(tm, n)
    full = lambda a: pl.BlockSpec(a.shape, lambda i: (0,) * a.ndim)
    return pl.pallas_call(
        _moe_route_kernel,
        out_shape=[jax.ShapeDtypeStruct((n, d + LANE), F32), jax.ShapeDtypeStruct((8, n), I32),
                   jax.ShapeDtypeStruct((8, LANE), I32)],
        grid=(n // tm,),
        in_specs=[pl.BlockSpec((tm, d), lambda i: (i, 0)), full(wr), full(rb)],
        out_specs=[pl.BlockSpec((tm, d + LANE), lambda i: (i, 0)),
                   pl.BlockSpec((8, tm), lambda i: (0, i)),
                   pl.BlockSpec((8, LANE), lambda i: (0, 0))],
        scratch_shapes=[pltpu.VMEM((tm, tm), BF16), pltpu.VMEM((8, LANE), F32)],
        compiler_params=_params(("arbitrary",)),
        name="moe_route",
    )(h, wr, rb)


def _row_copy_waves(n_rows, wave, start_row, wait_wave):
    n_waves = n_rows // wave
    for w in range(n_waves + 1):
        if w < n_waves:
            lax.fori_loop(w * wave, (w + 1) * wave, lambda r, c, w=w: (start_row(r, w % 2), c)[1], 0,
                          unroll=8)
        if w >= 1:
            wait_wave((w - 1) % 2)


def _moe_dispatch_kernel(pos_ref, fill_ref, xa_hbm, xs_hbm, zero_ref, sem, *, wave):
    i = pl.program_id(0)
    tm = pos_ref.shape[1]
    tme = zero_ref.shape[0]

    def copy(r, s):
        return pltpu.make_async_copy(xa_hbm.at[i * tm + r], xs_hbm.at[pos_ref[0, r]], sem.at[s])

    def wait_wave(s):
        rows = pl.ds(0, wave)
        pltpu.make_async_copy(xa_hbm.at[rows], xs_hbm.at[rows], sem.at[s]).wait()

    _row_copy_waves(tm, wave, lambda r, s: copy(r, s).start(), wait_wave)

    @pl.when(i == pl.num_programs(0) - 1)
    def _():
        zero_ref[...] = jnp.zeros(zero_ref.shape, F32)
        for g in range(MOE_GROUPS):
            first, count = fill_ref[0, g], fill_ref[1, g]

            def zcopy(r, first=first):
                return pltpu.make_async_copy(zero_ref.at[0], xs_hbm.at[first + r], sem.at[0])

            lax.fori_loop(0, count, lambda r, c: (zcopy(r).start(), c)[1], 0)
            lax.fori_loop(0, count, lambda r, c: (zcopy(r).wait(), c)[1], 0)
        n_tiles = fill_ref[2, 0]
        for k in range(MOE_GROUPS):
            @pl.when(k < fill_ref[2, 1])
            def _(k=k):
                tail = xs_hbm.at[pl.ds(pl.multiple_of((n_tiles + k) * tme, tme), tme)]
                cp = pltpu.make_async_copy(zero_ref, tail, sem.at[0])
                cp.start()
                cp.wait()


def _moe_dispatch(xa, pos, fill, n_sorted, tme, tm=1024, wave=256):
    n, da = xa.shape
    tm = min(tm, n)
    wave = min(wave, tm)
    return pl.pallas_call(
        functools.partial(_moe_dispatch_kernel, wave=wave),
        out_shape=jax.ShapeDtypeStruct((n_sorted, da), F32),
        grid=(n // tm,),
        in_specs=[pl.BlockSpec((1, tm), lambda i: (0, i), memory_space=pltpu.SMEM),
                  pl.BlockSpec(memory_space=pltpu.SMEM),
                  pl.BlockSpec(memory_space=pl.ANY)],
        out_specs=pl.BlockSpec(memory_space=pl.ANY),
        scratch_shapes=[pltpu.VMEM((tme, da), F32), pltpu.SemaphoreType.DMA((2,))],
        compiler_params=_params(("arbitrary",)),
        name="moe_dispatch",
    )(pos, fill, xa)


def _moe_expert_kernel(tg_ref, tj_ref, nt_ref, xs_ref, wg_ref, wu_ref, wd_ref, lg_ref, lb_ref,
                       ys_ref):
    d = ys_ref.shape[1]

    @pl.when(pl.program_id(0) < nt_ref[0])
    def _():
        x = xs_ref[:, :d]
        gates = xs_ref[:, d:]
        xb = x.astype(BF16)
        y = jnp.zeros(x.shape, F32)
        for e in range(MOE_EXPERTS):
            hg = _dot(xb, wg_ref[e])
            hu = _dot(xb, wu_ref[e])
            hid = hg * _sigmoid(hg) * hu * gates[:, e:e + 1]
            y = y + _dot(hid.astype(BF16), wd_ref[e])
        ys_ref[...] = _layer_norm(DN_ALPHA * x + y, lg_ref[...], lb_ref[...])

    @pl.when(pl.program_id(0) >= nt_ref[0])
    def _():
        ys_ref[...] = jnp.zeros(ys_ref.shape, F32)


def _moe_experts(xs, tile_g, tile_j, n_tiles, wg, wu, wd, lg, lb, tme):
    p, da = xs.shape
    d = da - LANE
    grp = lambda w: pl.BlockSpec((None,) + w.shape[1:], lambda i, tg, tj, nt: (tg[i], 0, 0, 0))
    full = lambda a: pl.BlockSpec(a.shape, lambda i, tg, tj, nt: (0,) * a.ndim)
    grid_spec = pltpu.PrefetchScalarGridSpec(
        num_scalar_prefetch=3,
        grid=(p // tme,),
        in_specs=[pl.BlockSpec((tme, da), lambda i, tg, tj, nt: (tj[i], 0)),
                  grp(wg), grp(wu), grp(wd), full(lg), full(lb)],
        out_specs=pl.BlockSpec((tme, d), lambda i, tg, tj, nt: (i, 0)),
    )
    return pl.pallas_call(
        _moe_expert_kernel,
        out_shape=jax.ShapeDtypeStruct((p, d), F32),
        grid_spec=grid_spec,
        compiler_params=_params(("arbitrary",)),
        name="moe_experts",
    )(tile_g, tile_j, n_tiles, xs, wg, wu, wd, lg, lb)


def _moe_combine_kernel(pos_ref, ys_hbm, o_ref, sem, *, wave):
    tm = o_ref.shape[0]

    def copy(r, s):
        return pltpu.make_async_copy(ys_hbm.at[pos_ref[0, r]], o_ref.at[r], sem.at[s])

    def wait_wave(s):
        rows = pl.ds(0, wave)
        pltpu.make_async_copy(ys_hbm.at[rows], o_ref.at[rows], sem.at[s]).wait()

    _row_copy_waves(tm, wave, lambda r, s: copy(r, s).start(), wait_wave)


def _moe_combine(ys, pos, n, tm=1024, wave=256):
    d = ys.shape[1]
    tm = min(tm, n)
    wave = min(wave, tm)
    return pl.pallas_call(
        functools.partial(_moe_combine_kernel, wave=wave),
        out_shape=jax.ShapeDtypeStruct((n, d), F32),
        grid=(n // tm,),
        in_specs=[pl.BlockSpec((1, tm), lambda i: (0, i), memory_space=pltpu.SMEM),
                  pl.BlockSpec(memory_space=pl.ANY)],
        out_specs=pl.BlockSpec((tm, d), lambda i: (i, 0)),
        scratch_shapes=[pltpu.SemaphoreType.DMA((2,))],
        compiler_params=_params(("arbitrary",)),
        name="moe_combine",
    )(pos, ys)


def _moe(h, wr, rb, wg, wu, wd, lg, lb, tme=512):
    n, d = h.shape
    tme = min(tme, n)
    xa, rt, cnt = _moe_route(h, wr, rb)
    counts = cnt[:MOE_GROUPS, 0]
    tiles = (counts + tme - 1) // tme
    first_tile = jnp.cumsum(tiles) - tiles
    gid, rank = rt[0], rt[1]
    pos = (first_tile[gid] * tme + rank)[None, :]
    n_slots = n // tme + MOE_GROUPS
    n_tiles = jnp.sum(tiles)
    fill = jnp.stack([first_tile * tme + counts, tiles * tme - counts,
                      jnp.zeros((MOE_GROUPS,), I32).at[0].set(n_tiles).at[1].set(n_slots - n_tiles)])
    slot = jnp.minimum(jnp.arange(n_slots, dtype=I32), n_tiles - 1)
    tile_g = jnp.sum(slot[:, None] >= (first_tile + tiles)[None, :], axis=1).astype(I32)
    xs = _moe_dispatch(xa, pos, fill, n_slots * tme, tme)
    ys = _moe_experts(xs, tile_g, slot, n_tiles[None].astype(I32), wg, wu, wd, lg, lb, tme)
    return _moe_combine(ys, pos, n)


def _s5_prep_kernel(lre_ref, lim_ref, ldt_ref, bre_ref, bim_ref, are_ref, aim_ref, bbre_ref,
                    bbim_ref):
    lr = jnp.minimum(lre_ref[...], -1e-4)
    li = lim_ref[...]
    dt = jnp.exp(ldt_ref[...])
    mag = jnp.exp(lr * dt)
    ab_re = mag * jnp.cos(li * dt)
    ab_im = mag * jnp.sin(li * dt)
    den = lr * lr + li * li
    nr = ab_re - 1.0
    coef_re = (nr * lr + ab_im * li) / den
    coef_im = (ab_im * lr - nr * li) / den
    are_ref[...] = ab_re
    aim_ref[...] = ab_im
    bbre_ref[...] = coef_re * bre_ref[...] - coef_im * bim_ref[...]
    bbim_ref[...] = coef_re * bim_ref[...] + coef_im * bre_ref[...]


def _s5_prep(lam_re, lam_im, log_dt, b_re, b_im):
    gp = C_GROUPS * C_STATE
    col = lambda a: a.reshape(gp, 1)
    ldt = jnp.broadcast_to(log_dt[:, None], (C_GROUPS, C_STATE))
    outs = pl.pallas_call(
        _s5_prep_kernel,
        out_shape=[jax.ShapeDtypeStruct((gp, 1), F32), jax.ShapeDtypeStruct((gp, 1), F32),
                   jax.ShapeDtypeStruct((gp, C_GROUP), F32), jax.ShapeDtypeStruct((gp, C_GROUP), F32)],
        name="s5_discretise",
    )(col(lam_re), col(lam_im), col(ldt), b_re.reshape(gp, C_GROUP), b_im.reshape(gp, C_GROUP))
    ab_re, ab_im, bb_re, bb_im = outs
    return (ab_re.reshape(C_GROUPS, C_STATE), ab_im.reshape(C_GROUPS, C_STATE),
            bb_re.reshape(C_GROUPS, C_STATE, C_GROUP), bb_im.reshape(C_GROUPS, C_STATE, C_GROUP))


def _s5_uproj_kernel(x_ref, w_ref, o_ref):
    o_ref[...] = _dot(x_ref[...].astype(BF16), w_ref[...])


def _s5_uproj(h2, w, bsz, seq, tm=512):
    n, d = h2.shape
    steps = seq // tm
    out = pl.pallas_call(
        _s5_uproj_kernel,
        out_shape=jax.ShapeDtypeStruct((seq, bsz * d), F32),
        grid=(bsz, steps),
        in_specs=[pl.BlockSpec((tm, d), lambda b, s: (b * steps + s, 0)),
                  pl.BlockSpec(w.shape, lambda b, s: (0, 0))],
        out_specs=pl.BlockSpec((tm, d), lambda b, s: (s, b)),
        compiler_params=_params(("arbitrary", "arbitrary")),
        name="s5_in_proj",
    )(h2, w)
    return out.reshape(seq * bsz, d)


def _gelu_tanh(x):
    return 0.5 * x * (1.0 + jnp.tanh(math.sqrt(2.0 / math.pi) * (x + 0.044715 * (x * x * x))))


def _s5_scan_kernel(u_ref, bbre_ref, bbim_ref, are_ref, aim_ref, cm_ref, d_ref, g_ref,
                    st_ref, bure_ref, buim_ref, s_ref, *, bsz, tl, n_cb, sub):
    @pl.when(pl.program_id(0) == 0)
    def _():
        st_ref[...] = jnp.zeros(st_ref.shape, F32)

    cw = D_MODEL // n_cb
    sw = bure_ref.shape[1]
    for cb in range(n_cb):
        ch = slice(cb * cw, (cb + 1) * cw)
        u = u_ref[:, ch]
        ub = u.astype(BF16)
        bure_ref[...] = _dot(ub, bbre_ref[cb])
        buim_ref[...] = _dot(ub, bbim_ref[cb])
        for s0 in range(0, sw, sub):
            cs = slice(s0, s0 + sub)
            a_re = jnp.broadcast_to(are_ref[cb, :, cs], (bsz, sub))
            a_im = jnp.broadcast_to(aim_ref[cb, :, cs], (bsz, sub))

            def step(t, carry, cs=cs, a_re=a_re, a_im=a_im):
                s_re, s_im = carry
                r0 = pl.multiple_of(t * bsz, bsz)
                n_re = a_re * s_re - a_im * s_im + bure_ref[pl.ds(r0, bsz), cs]
                n_im = a_re * s_im + a_im * s_re + buim_ref[pl.ds(r0, bsz), cs]
                s_ref[pl.ds(r0, bsz), cs] = n_re.astype(BF16)
                s_ref[pl.ds(r0, bsz), sw + s0:sw + s0 + sub] = n_im.astype(BF16)
                return n_re, n_im

            s_re, s_im = lax.fori_loop(0, tl, step, (st_ref[cb, 0, :, cs], st_ref[cb, 1, :, cs]),
                                       unroll=2)
            st_ref[cb, 0, :, cs] = s_re
            st_ref[cb, 1, :, cs] = s_im
        y = _dot(s_ref[...], cm_ref[cb]) + d_ref[:, ch] * u
        g_ref[:, ch] = _gelu_tanh(y).astype(g_ref.dtype)


def _s5_scan(u_lb, bbre, bbim, a_re, a_im, cmat, d_skip, bsz, seq, tl=32, sub=512):
    n, d = u_lb.shape
    n_cb, cw, sw = bbre.shape
    rows = tl * bsz
    full = lambda a: pl.BlockSpec(a.shape, lambda t: (0,) * a.ndim)
    return pl.pallas_call(
        functools.partial(_s5_scan_kernel, bsz=bsz, tl=tl, n_cb=n_cb, sub=sub),
        out_shape=jax.ShapeDtypeStruct((n, d), BF16),
        grid=(seq // tl,),
        in_specs=[pl.BlockSpec((rows, d), lambda t: (t, 0)), full(bbre), full(bbim), full(a_re),
                  full(a_im), full(cmat), full(d_skip)],
        out_specs=pl.BlockSpec((rows, d), lambda t: (t, 0)),
        scratch_shapes=[pltpu.VMEM((n_cb, 2, bsz, sw), F32), pltpu.VMEM((rows, sw), F32),
                        pltpu.VMEM((rows, sw), F32), pltpu.VMEM((rows, 2 * sw), BF16)],
        compiler_params=_params(("arbitrary",)),
        name="s5_scan",
    )(u_lb, bbre, bbim, a_re, a_im, cmat, d_skip)


def _s5_glu_kernel(g_ref, x_ref, w1_ref, w2_ref, wo_ref, lg_ref, lb_ref, out_ref):
    g = g_ref[...]
    z = _dot(g, w1_ref[...]) * _sigmoid(_dot(g, w2_ref[...]))
    m = _dot(z.astype(BF16), wo_ref[...])
    out_ref[...] = _layer_norm(DN_ALPHA * x_ref[...] + m, lg_ref[...], lb_ref[...])


def _s5_glu(g_lb, h2, w1, w2, wo, lg, lb, bsz, seq, tm=512):
    n, d = h2.shape
    steps = seq // tm
    full = lambda a: pl.BlockSpec(a.shape, lambda b, s: (0, 0))
    return pl.pallas_call(
        _s5_glu_kernel,
        out_shape=jax.ShapeDtypeStruct((n, d), F32),
        grid=(bsz, steps),
        in_specs=[pl.BlockSpec((tm, d), lambda b, s: (s, b)),
                  pl.BlockSpec((tm, d), lambda b, s: (b * steps + s, 0)),
                  full(w1), full(w2), full(wo), full(lg), full(lb)],
        out_specs=pl.BlockSpec((tm, d), lambda b, s: (b * steps + s, 0)),
        compiler_params=_params(("arbitrary", "arbitrary")),
        name="s5_glu_out_ln",
    )(g_lb.reshape(seq, bsz * d), h2, w1, w2, wo, lg, lb)


def _pad_heads(w, heads, dim):
    d = w.shape[0]
    w = w.reshape(d, heads, dim)
    return jnp.pad(w, ((0, 0), (0, 0), (0, LANE - dim))).reshape(d, heads * LANE)


def _even_layer_weights(w_in, gate_w2, gate_b):
    splits = (A_HEADS * A_HEAD_DIM, A_KV_HEADS * A_HEAD_DIM, A_KV_HEADS * A_HEAD_DIM,
              IDX_HEADS * IDX_DIM, IDX_DIM, IDX_HEADS,
              B_HEADS * B_KEY_DIM, B_HEADS * B_KEY_DIM, B_HEADS * B_VAL_DIM, GATE_RANK,
              B_HEADS * B_VAL_DIM)
    offs = np.cumsum((0,) + splits)
    aq, ak, av, iq, ik, iw, bq, bk, bv, bg, br = [w_in[:, offs[k]:offs[k + 1]] for k in range(11)]
    d = w_in.shape[0]
    slab = jnp.concatenate(
        [ik, iw, bg, jnp.zeros((d, LANE - IDX_DIM - IDX_HEADS - GATE_RANK), w_in.dtype)], axis=1)
    w_a = jnp.concatenate([_pad_heads(aq, A_HEADS, A_HEAD_DIM),
                           _pad_heads(ak, A_KV_HEADS, A_HEAD_DIM), av], axis=1)
    w_f = jnp.concatenate([_pad_heads(iq, IDX_HEADS, IDX_DIM), _pad_heads(bq, B_HEADS, B_KEY_DIM),
                           _pad_heads(bk, B_HEADS, B_KEY_DIM), bv, br, slab], axis=1)
    w_all = jnp.concatenate([w_a, w_f], axis=1).astype(BF16)
    g0 = IDX_DIM + IDX_HEADS
    w2p = jnp.zeros((LANE, B_HEADS * LANE), F32).at[g0:g0 + GATE_RANK].set(
        _pad_heads(gate_w2, B_HEADS, B_KEY_DIM)).astype(BF16)
    gbp = _pad_heads(gate_b[None, :], B_HEADS, B_KEY_DIM)
    return w_all, (w_a.shape[1], w_f.shape[1]), w2p, gbp


def _even_mixer_ln(x2, bsz, seq, w_in, bias_tiles, gate_w2, gate_b, norm_g, w_out, ln_g, ln_b):
    w_all, widths, w2p, gbp = _even_layer_weights(w_in, gate_w2, gate_b)
    h_a, h_f = _in_proj(x2, w_all, widths, (BF16, F32))
    o_a = _dsa(h_a, h_f, bias_tiles, bsz, seq)
    o_b = _gla(h_f, w2p, gbp, norm_g[None, :], bsz, seq)
    return _outproj_ln(o_a, o_b, x2, w_out.astype(BF16), ln_g[None, :], ln_b[None, :])


def _block_diag(m, per):
    g, r, c = m.shape
    eye = jnp.eye(per, dtype=m.dtype)
    m = m.reshape(g // per, per, r, c)
    bd = m[:, :, :, None, :] * eye[None, :, None, :, None]
    return bd.reshape(g // per, per * r, per * c)


def _s5_mixer_ln(h2, bsz, seq, w_in, lam_re, lam_im, log_dt, b_re, b_im, c_re, c_im, d_skip,
                 glu_w1, glu_w2, w_out, ln_g, ln_b, groups_per_block=16):
    ab_re, ab_im, bb_re, bb_im = _s5_prep(lam_re, lam_im, log_dt, b_re, b_im)
    per = groups_per_block
    n_cb = C_GROUPS // per
    bbre = _block_diag(jnp.swapaxes(bb_re, 1, 2), per).astype(BF16)
    bbim = _block_diag(jnp.swapaxes(bb_im, 1, 2), per).astype(BF16)
    a_re = ab_re.reshape(n_cb, 1, per * C_STATE)
    a_im = ab_im.reshape(n_cb, 1, per * C_STATE)
    cre = _block_diag(jnp.swapaxes(c_re, 1, 2), per)
    cim = _block_diag(jnp.swapaxes(c_im, 1, 2), per)
    cmat = jnp.concatenate([cre, -cim], axis=1).astype(BF16)
    u_lb = _s5_uproj(h2, w_in.astype(BF16), bsz, seq)
    g_lb = _s5_scan(u_lb, bbre, bbim, a_re, a_im, cmat, d_skip[None, :], bsz, seq)
    return _s5_glu(g_lb, h2, glu_w1.astype(BF16), glu_w2.astype(BF16), w_out.astype(BF16),
                   ln_g[None, :], ln_b[None, :], bsz, seq)


def _moe_ln(h, r_coarse, rb_coarse, r_fine, rb_fine, w_gate, w_up, w_down, ln_g, ln_b):
    d = h.shape[1]
    rows = MOE_GROUPS + N_EXPERTS
    wr = jnp.concatenate([r_coarse.T, jnp.transpose(r_fine, (0, 2, 1)).reshape(N_EXPERTS, d),
                          jnp.zeros((32 - rows, d), F32)], axis=0)
    rb = jnp.concatenate([rb_coarse, rb_fine.reshape(N_EXPERTS), jnp.zeros((32 - rows,), F32)])
    return _moe(h, wr, rb[:, None], w_gate.astype(BF16), w_up.astype(BF16), w_down.astype(BF16),
                ln_g[None, :], ln_b[None, :])


def kernel(x, rel_bias, ab_w_in, gla_gate_w2, gla_gate_b, gla_norm_g, ab_w_out, s5_w_in, s5_lam_re, s5_lam_im, s5_log_dt, s5_b_re, s5_b_im, s5_c_re, s5_c_im, s5_d, s5_glu_w1, s5_glu_w2, s5_w_out, ln_mix_g, ln_mix_b, ln_ffn_g, ln_ffn_b, moe_r_coarse, moe_rb_coarse, moe_r_fine, moe_rb_fine, moe_w_gate, moe_w_up, moe_w_down):
    bsz, seq, d = x.shape
    h = x.reshape(bsz * seq, d)
    bias_tiles = _bias_tiles(rel_bias)
    for layer in range(DEPTH):
        i = layer // 2
        if layer % 2 == 0:
            h = _even_mixer_ln(h, bsz, seq, ab_w_in[i], bias_tiles, gla_gate_w2[i], gla_gate_b[i],
                               gla_norm_g[i], ab_w_out[i], ln_mix_g[layer], ln_mix_b[layer])
        else:
            h = _s5_mixer_ln(h, bsz, seq, s5_w_in[i], s5_lam_re[i], s5_lam_im[i], s5_log_dt[i],
                             s5_b_re[i], s5_b_im[i], s5_c_re[i], s5_c_im[i], s5_d[i],
                             s5_glu_w1[i], s5_glu_w2[i], s5_w_out[i],
                             ln_mix_g[layer], ln_mix_b[layer])
        h = _moe_ln(h, moe_r_coarse[layer], moe_rb_coarse[layer], moe_r_fine[layer],
                    moe_rb_fine[layer], moe_w_gate[layer], moe_w_up[layer], moe_w_down[layer],
                    ln_ffn_g[layer], ln_ffn_b[layer])
    return h.reshape(bsz, seq, d).astype(x.dtype)
```

```python
import functools
import math

import numpy as np
import jax
import jax.numpy as jnp
from jax import lax
from jax.experimental import pallas as pl
from jax.experimental.pallas import tpu as pltpu

F32 = jnp.float32
BF16 = jnp.bfloat16
I32 = jnp.int32

D_MODEL = 1024
DEPTH = 2
DN_ALPHA = (2.0 * DEPTH) ** 0.25
LN_EPS = 1e-5
A_HEAD_DIM = 64
A_HEADS = 8
A_KV_HEADS = 2
IDX_HEADS = 4
IDX_DIM = 64
TOPK_MAX = 256
REL_BUCKETS = 32
REL_MAX_DIST = 128
B_HEADS = 4
B_VAL_DIM = 128
B_KEY_DIM = 64
GATE_RANK = 16
GATE_TAU = 16.0
GLA_CHUNK = 64
C_GROUP = 16
C_GROUPS = 64
C_STATE = 64
MOE_GROUPS = 4
MOE_EXPERTS = 4
MOE_HIDDEN = 512
N_EXPERTS = MOE_GROUPS * MOE_EXPERTS

LANE = 128
VMEM_LIMIT = 52 * 1024 * 1024
NEG_INF = float("-inf")
INT_MIN = -(2 ** 31)

_NT = (((1,), (1,)), ((), ()))
_TN = (((0,), (0,)), ((), ()))


def _dot(a, b):
    return jnp.dot(a, b, preferred_element_type=F32)


def _dot_nt(a, b):
    return lax.dot_general(a, b, _NT, preferred_element_type=F32)


def _dot_tn(a, b):
    return lax.dot_general(a, b, _TN, preferred_element_type=F32)


def _split3(x):
    hi = x.astype(BF16)
    r1 = x - hi.astype(F32)
    mid = r1.astype(BF16)
    lo = (r1 - mid.astype(F32)).astype(BF16)
    return hi, mid, lo


def _params(sem):
    return pltpu.CompilerParams(dimension_semantics=sem, vmem_limit_bytes=VMEM_LIMIT)


def _layer_norm(y, g, b):
    mu = jnp.mean(y, axis=-1, keepdims=True)
    d = y - mu
    var = jnp.mean(d * d, axis=-1, keepdims=True)
    return d * lax.rsqrt(var + LN_EPS) * g + b


def _sigmoid(x):
    return 1.0 / (1.0 + jnp.exp(-x))


def _bucket_thresholds():
    max_exact = REL_BUCKETS // 2
    nf = np.arange(max_exact, 4 * REL_MAX_DIST).astype(np.float32)
    large = max_exact + (np.log(nf / np.float32(max_exact))
                         / np.float32(math.log(REL_MAX_DIST / max_exact))
                         * np.float32(REL_BUCKETS - max_exact)).astype(np.int32)
    large = np.minimum(large, REL_BUCKETS - 1)
    return [int(nf[np.argmax(large >= max_exact + j)]) for j in range(1, REL_BUCKETS - max_exact)]


def _bias_kernel(rb_ref, out_ref):
    max_exact = REL_BUCKETS // 2
    srow = lax.broadcasted_iota(I32, (LANE, LANE), 0)
    tcol = lax.broadcasted_iota(I32, (LANE, LANE), 1)
    thr = _bucket_thresholds()
    for band in range(2):
        n = jnp.maximum(band * LANE + tcol - srow, 0)
        large = jnp.full((LANE, LANE), max_exact, I32)
        for t in thr:
            large = large + jnp.where(n >= t, 1, 0)
        bucket = jnp.where(n < max_exact, n, large)
        for h in range(A_HEADS):
            acc = jnp.zeros((LANE, LANE), F32)
            for bk in range(REL_BUCKETS):
                acc = acc + jnp.where(bucket == bk, rb_ref[bk, h], 0.0)
            out_ref[h, band] = acc
    for h in range(A_HEADS):
        out_ref[h, 2] = jnp.full((LANE, LANE), rb_ref[REL_BUCKETS - 1, h], F32)


def _bias_tiles(rel_bias):
    return pl.pallas_call(
        _bias_kernel,
        out_shape=jax.ShapeDtypeStruct((A_HEADS, 3, LANE, LANE), F32),
        in_specs=[pl.BlockSpec(memory_space=pltpu.SMEM)],
        name="rel_bias_tiles",
    )(rel_bias)


def _proj_kernel(x_ref, w_ref, *out_refs, widths, chunk):
    xb = x_ref[...].astype(BF16)
    c0 = 0
    for o_ref, width in zip(out_refs, widths):
        for c in range(0, width, chunk):
            ce = min(c + chunk, width)
            o_ref[:, c:ce] = _dot(xb, w_ref[:, c0 + c:c0 + ce]).astype(o_ref.dtype)
        c0 += width


def _in_proj(x2, w, widths, dtypes, tm=512, chunk=256):
    n, d = x2.shape
    return pl.pallas_call(
        functools.partial(_proj_kernel, widths=widths, chunk=chunk),
        out_shape=[jax.ShapeDtypeStruct((n, wd), dt) for wd, dt in zip(widths, dtypes)],
        grid=(n // tm,),
        in_specs=[pl.BlockSpec((tm, d), lambda i: (i, 0)),
                  pl.BlockSpec(w.shape, lambda i: (0, 0))],
        out_specs=[pl.BlockSpec((tm, wd), lambda i: (i, 0)) for wd in widths],
        compiler_params=_params(("arbitrary",)),
        name="in_proj",
    )(x2, w)


def _sortable(x):
    bits = lax.bitcast_convert_type(x, I32)
    return jnp.where(bits < 0, bits ^ jnp.int32(0x7FFFFFFF), bits)


def _dsa_kernel(q_ref, k_ref, v_ref, iq_ref, slab_ref, slabq_ref, bias_ref, o_ref,
                key_ref, am_ref, vt_ref, acc_ref, ot_ref, *, k_sel, seq):
    i = pl.program_id(1)
    t0 = i * LANE
    nkt = i + 1
    kt2 = 2 * LANE
    n2 = lax.shift_right_logical(nkt + 1, 1)
    rep = A_HEADS // A_KV_HEADS
    srow_2 = lax.broadcasted_iota(I32, (kt2, LANE), 0)
    tcol = t0 + lax.broadcasted_iota(I32, (kt2, LANE), 1)

    @pl.when(i == 0)
    def _():
        for jt in range(seq // kt2):
            vt_ref[jt] = v_ref[jt * kt2:(jt + 1) * kt2, :].astype(F32).T.astype(BF16)

    slab_t = slabq_ref[...].T
    w_t = slab_t[IDX_DIM:IDX_DIM + IDX_HEADS, :] * (IDX_HEADS ** -0.5)
    iqb = iq_ref[...].astype(BF16)
    iq_stack = jnp.concatenate([iqb[:, h * LANE:(h + 1) * LANE] for h in range(IDX_HEADS)], axis=0)

    def score_tile(j, carry):
        s0 = pl.multiple_of(j * kt2, kt2)
        ikt = slab_ref[pl.ds(s0, kt2), :].astype(BF16)
        d = _dot_nt(ikt, iq_stack) * (IDX_DIM ** -0.5)
        acc = jnp.zeros((kt2, LANE), F32)
        for h in range(IDX_HEADS):
            acc = acc + jnp.maximum(d[:, h * LANE:(h + 1) * LANE], 0.0) * w_t[h:h + 1, :]
        acc = jnp.where(s0 + srow_2 <= tcol, acc, NEG_INF)
        key_ref[pl.ds(s0, kt2), :] = _sortable(acc)
        return carry

    lax.fori_loop(0, n2, score_tile, 0)
    key_ref[pl.ds(pl.multiple_of(nkt * LANE, LANE), LANE), :] = jnp.full((LANE, LANE), INT_MIN, I32)

    def count(pred_fn):
        def body(j, c):
            s0 = pl.multiple_of(j * kt2, kt2)
            kt = key_ref[pl.ds(s0, kt2), :]
            m = jnp.where(pred_fn(kt, s0 + srow_2), 1, 0)
            return c + jnp.sum(m.reshape(kt2 // 8, 8, LANE), axis=0)
        c = lax.fori_loop(0, n2, body, jnp.zeros((8, LANE), I32))
        return jnp.sum(c, axis=0, keepdims=True)

    def bcast(v):
        return jnp.broadcast_to(v, (kt2, LANE))

    def search():
        c0 = count(lambda kt, s: kt >= 0)
        ans0 = jnp.where(c0 >= k_sel, 0, INT_MIN).astype(I32)

        def bit_body(bi, ans):
            cand = ans | lax.shift_left(jnp.int32(1), 30 - bi)
            cb = bcast(cand)
            cnt = count(lambda kt, s: kt >= cb)
            return jnp.where(cnt >= k_sel, cand, ans)

        ans = lax.fori_loop(0, 31, bit_body, ans0)
        ab = bcast(ans)
        cnt_ge = count(lambda kt, s: kt >= ab)
        cnt_gt = count(lambda kt, s: kt > ab)
        need = k_sel - cnt_gt

        def tie_search():
            def idx_body(bi, x):
                cand = x | lax.shift_left(jnp.int32(1), (seq.bit_length() - 2) - bi)
                cb = bcast(cand)
                f = count(lambda kt, s: (kt == ab) & (s < cb))
                return jnp.where(f < need, cand, x)
            return lax.fori_loop(0, seq.bit_length() - 1, idx_body, jnp.zeros((1, LANE), I32))

        cut = lax.cond(jnp.max(cnt_ge) > k_sel, tie_search,
                       lambda: jnp.full((1, LANE), seq - 1, I32))
        return ans, cut

    ans, cut = lax.cond(t0 >= k_sel, search,
                        lambda: (jnp.full((1, LANE), INT_MIN, I32),
                                 jnp.full((1, LANE), seq - 1, I32)))
    ans_b = bcast(ans)
    cut_b = bcast(cut)

    def mask_tile(j, carry):
        s0 = pl.multiple_of(j * kt2, kt2)
        kt = key_ref[pl.ds(s0, kt2), :]
        srow = s0 + srow_2
        sel = (kt > ans_b) | ((kt == ans_b) & (srow <= cut_b))
        am_ref[pl.ds(s0, kt2), :] = jnp.where(sel & (srow <= tcol), 0.0, NEG_INF)
        return carry

    lax.fori_loop(0, n2, mask_tile, 0)

    scale = jnp.asarray(A_HEAD_DIM ** -0.5, BF16)
    q_stacks = [jnp.concatenate([q_ref[:, (g * rep + r) * LANE:(g * rep + r + 1) * LANE]
                                 for r in range(rep)], axis=0) * scale for g in range(A_KV_HEADS)]
    acc_ref[...] = jnp.zeros(acc_ref.shape, F32)

    def tile_body(j, carry):
        ms, ls = carry
        s0 = pl.multiple_of(j * kt2, kt2)
        am = am_ref[pl.ds(s0, kt2), :]
        band0 = jnp.clip(i - 2 * j, 0, 2)
        band1 = jnp.clip(i - 2 * j - 1, 0, 2)
        new_ms, new_ls = [], []
        for g in range(A_KV_HEADS):
            kt = k_ref[pl.ds(s0, kt2), g * LANE:(g + 1) * LANE]
            lg_all = _dot_nt(kt, q_stacks[g])
            ps, alphas = [], []
            for r in range(rep):
                h = g * rep + r
                bias = jnp.concatenate([bias_ref[h, band0], bias_ref[h, band1]], axis=0)
                lg = lg_all[:, r * LANE:(r + 1) * LANE] + bias + am
                m_new = jnp.maximum(ms[h], jnp.max(lg, axis=0, keepdims=True))
                m_safe = jnp.where(m_new == NEG_INF, 0.0, m_new)
                alpha = jnp.exp(ms[h] - m_safe)
                p = jnp.exp(lg - m_safe)
                new_ms.append(m_new)
                new_ls.append(alpha * ls[h] + jnp.sum(p, axis=0, keepdims=True))
                ps.append(p.astype(BF16))
                alphas.append(alpha)
            vt = vt_ref[j, g * A_HEAD_DIM:(g + 1) * A_HEAD_DIM, :]
            pv = _dot(vt, jnp.concatenate(ps, axis=1))
            acc_ref[g] = acc_ref[g] * jnp.concatenate(alphas, axis=1) + pv
        return tuple(new_ms), tuple(new_ls)

    init = (tuple(jnp.full((1, LANE), NEG_INF, F32) for _ in range(A_HEADS)),
            tuple(jnp.zeros((1, LANE), F32) for _ in range(A_HEADS)))
    _, ls = lax.fori_loop(0, n2, tile_body, init)
    for g in range(A_KV_HEADS):
        for r in range(rep):
            h = g * rep + r
            ot_ref[h * A_HEAD_DIM:(h + 1) * A_HEAD_DIM, :] = (
                acc_ref[g, :, r * LANE:(r + 1) * LANE] / ls[h])
    o_ref[...] = ot_ref[...].T.astype(o_ref.dtype)


def _dsa(h_a, h_f, bias_tiles, bsz, seq):
    n = bsz * seq
    nq = seq // LANE
    k_sel = min(TOPK_MAX, seq // 4)
    assert k_sel % LANE == 0 and seq % (2 * LANE) == 0
    qw = A_HEADS * LANE
    slab_blk = (h_f.shape[1] - LANE) // LANE
    return pl.pallas_call(
        functools.partial(_dsa_kernel, k_sel=k_sel, seq=seq),
        out_shape=jax.ShapeDtypeStruct((n, A_HEADS * A_HEAD_DIM), BF16),
        grid=(bsz, nq),
        in_specs=[
            pl.BlockSpec((LANE, qw), lambda b, i: (b * nq + i, 0)),
            pl.BlockSpec((seq, A_KV_HEADS * LANE), lambda b, i: (b, qw // (A_KV_HEADS * LANE))),
            pl.BlockSpec((seq, LANE), lambda b, i: (b, (qw + A_KV_HEADS * LANE) // LANE)),
            pl.BlockSpec((LANE, IDX_HEADS * LANE), lambda b, i: (b * nq + i, 0)),
            pl.BlockSpec((seq, LANE), lambda b, i: (b, slab_blk)),
            pl.BlockSpec((LANE, LANE), lambda b, i: (b * nq + i, slab_blk)),
            pl.BlockSpec((A_HEADS, 3, LANE, LANE), lambda b, i: (0, 0, 0, 0)),
        ],
        out_specs=pl.BlockSpec((LANE, A_HEADS * A_HEAD_DIM), lambda b, i: (b * nq + i, 0)),
        scratch_shapes=[
            pltpu.VMEM((seq + LANE, LANE), I32),
            pltpu.VMEM((seq, LANE), F32),
            pltpu.VMEM((nq // 2, LANE, 2 * LANE), BF16),
            pltpu.VMEM((A_KV_HEADS, A_HEAD_DIM, (A_HEADS // A_KV_HEADS) * LANE), F32),
            pltpu.VMEM((A_HEADS * A_HEAD_DIM, LANE), F32),
        ],
        compiler_params=_params(("arbitrary", "arbitrary")),
        name="dsa_attention",
    )(h_a, h_a, h_a, h_f, h_f, h_f, bias_tiles)


def _gla_kernel(bq_ref, bk_ref, bv_ref, br_ref, slab_ref, w2_ref, gb_ref, ng_ref, o_ref,
                st_ref, *, n_chunks):
    @pl.when(pl.program_id(1) == 0)
    def _():
        st_ref[...] = jnp.zeros(st_ref.shape, F32)

    ch = GLA_CHUNK
    row = lax.broadcasted_iota(I32, (ch, ch), 0)
    col = lax.broadcasted_iota(I32, (ch, ch), 1)
    tril = row >= col
    tri = jnp.where(tril, 1.0, 0.0).astype(BF16)
    w2 = w2_ref[...]
    for c in range(n_chunks):
        rows = slice(c * ch, (c + 1) * ch)
        gate = _dot(slab_ref[rows, :].astype(BF16), w2) + gb_ref[...]
        log_a = (jnp.minimum(gate, 0.0) - jnp.log1p(jnp.exp(-jnp.abs(gate)))) / GATE_TAU
        hi, mid, lo = _split3(log_a)
        cum = _dot(tri, hi) + _dot(tri, mid) + _dot(tri, lo)
        last = cum[ch - 1:ch, :]
        q = bq_ref[rows, :] * (B_KEY_DIM ** -0.5)
        k = bk_ref[rows, :]
        q_dec = (q * jnp.exp(cum)).astype(BF16)
        k_inv = (k * jnp.exp(-cum)).astype(BF16)
        k_end = (k * jnp.exp(last - cum)).astype(BF16)
        decay = jnp.exp(last)
        for h in range(B_HEADS):
            cs = slice(h * LANE, (h + 1) * LANE)
            v = bv_ref[rows, cs].astype(BF16)
            sc = jnp.where(tril, _dot_nt(q_dec[:, cs], k_inv[:, cs]), 0.0)
            st = st_ref[h]
            o = _dot(sc.astype(BF16), v) + _dot_nt(q_dec[:, cs], st.astype(BF16))
            st_ref[h] = decay[:, cs] * st + _dot_tn(v, k_end[:, cs])
            mu = jnp.mean(o, axis=-1, keepdims=True)
            dlt = o - mu
            var = jnp.mean(dlt * dlt, axis=-1, keepdims=True)
            on = dlt * lax.rsqrt(var + LN_EPS) * ng_ref[:, cs]
            r = br_ref[rows, cs]
            o_ref[rows, cs] = (on * (r * _sigmoid(r))).astype(o_ref.dtype)


def _gla(h_f, w2p, gbp, norm_g, bsz, seq, rows_per_step=256):
    n = bsz * seq
    hw = B_HEADS * LANE
    steps = seq // rows_per_step
    slab_blk = (h_f.shape[1] - LANE) // LANE
    blk = lambda cb: pl.BlockSpec((rows_per_step, hw), lambda b, s, cb=cb: (b * steps + s, cb))
    return pl.pallas_call(
        functools.partial(_gla_kernel, n_chunks=rows_per_step // GLA_CHUNK),
        out_shape=jax.ShapeDtypeStruct((n, hw), BF16),
        grid=(bsz, steps),
        in_specs=[blk(1), blk(2), blk(3), blk(4),
                  pl.BlockSpec((rows_per_step, LANE), lambda b, s: (b * steps + s, slab_blk)),
                  pl.BlockSpec((LANE, hw), lambda b, s: (0, 0)),
                  pl.BlockSpec((1, hw), lambda b, s: (0, 0)),
                  pl.BlockSpec((1, hw), lambda b, s: (0, 0))],
        out_specs=pl.BlockSpec((rows_per_step, hw), lambda b, s: (b * steps + s, 0)),
        scratch_shapes=[pltpu.VMEM((B_HEADS, LANE, LANE), F32)],
        compiler_params=_params(("arbitrary", "arbitrary")),
        name="gla_attention",
    )(h_f, h_f, h_f, h_f, h_f, w2p, gbp, norm_g)


def _outproj_ln_kernel(oa_ref, ob_ref, x_ref, w_ref, g_ref, b_ref, out_ref):
    ka = oa_ref.shape[1]
    m = _dot(oa_ref[...], w_ref[0:ka, :]) + _dot(ob_ref[...], w_ref[ka:, :])
    out_ref[...] = _layer_norm(DN_ALPHA * x_ref[...] + m, g_ref[...], b_ref[...])


def _outproj_ln(o_a, o_b, x2, w, g, b, tm=512):
    n, d = x2.shape
    row = lambda width: pl.BlockSpec((tm, width), lambda i: (i, 0))
    full = lambda a: pl.BlockSpec(a.shape, lambda i: (0, 0))
    return pl.pallas_call(
        _outproj_ln_kernel,
        out_shape=jax.ShapeDtypeStruct((n, d), F32),
        grid=(n // tm,),
        in_specs=[row(o_a.shape[1]), row(o_b.shape[1]), row(d), full(w), full(g), full(b)],
        out_specs=row(d),
        compiler_params=_params(("arbitrary",)),
        name="out_proj_ln",
    )(o_a, o_b, x2, w, g, b)


def _route(lt):
    gl = [lt[g:g + 1, :] for g in range(MOE_GROUPS)]
    best, gsel = gl[0], jnp.zeros_like(gl[0], dtype=I32)
    for g in range(1, MOE_GROUPS):
        better = gl[g] > best
        gsel = jnp.where(better, g, gsel)
        best = jnp.where(better, gl[g], best)
    denom = sum(jnp.exp(x - best) for x in gl)
    g_w = 1.0 / denom
    fl = []
    for e in range(MOE_EXPERTS):
        acc = jnp.zeros_like(best)
        for g in range(MOE_GROUPS):
            r = MOE_GROUPS + g * MOE_EXPERTS + e
            acc = jnp.where(gsel == g, lt[r:r + 1, :], acc)
        fl.append(acc)
    v1, i1 = fl[0], jnp.zeros_like(gsel)
    for e in range(1, MOE_EXPERTS):
        better = fl[e] > v1
        i1 = jnp.where(better, e, i1)
        v1 = jnp.where(better, fl[e], v1)
    v2, i2 = jnp.full_like(v1, NEG_INF), jnp.zeros_like(gsel)
    for e in range(MOE_EXPERTS):
        better = (fl[e] > v2) & (i1 != e)
        i2 = jnp.where(better, e, i2)
        v2 = jnp.where(better, fl[e], v2)
    e2 = jnp.exp(v2 - v1)
    w1 = (1.0 / (1.0 + e2)) * g_w
    w2 = (e2 / (1.0 + e2)) * g_w
    e_gate = [jnp.where(i1 == e, w1, 0.0) + jnp.where(i2 == e, w2, 0.0)
              for e in range(MOE_EXPERTS)]
    return gsel, e_gate


def _moe_route_kernel(x_ref, wr_ref, rb_ref, xa_ref, rt_ref, cnt_ref, tri_ref, carry_ref):
    i = pl.program_id(0)
    tm, d = x_ref.shape

    @pl.when(i == 0)
    def _():
        carry_ref[...] = jnp.zeros(carry_ref.shape, F32)
        before = (lax.broadcasted_iota(I32, (tm, tm), 0) < lax.broadcasted_iota(I32, (tm, tm), 1))
        tri_ref[...] = jnp.where(before, 1.0, 0.0).astype(BF16)

    x = x_ref[...]
    xs = _split3(x)
    ws = _split3(wr_ref[...])
    lt = jnp.zeros((wr_ref.shape[0], tm), F32)
    for a, b in ((2, 0), (0, 2), (1, 1), (1, 0), (0, 1), (0, 0)):
        lt = lt + _dot_nt(ws[a], xs[b])
    lt = lt + rb_ref[...]
    gsel, e_gate = _route(lt)

    row8 = lax.broadcasted_iota(I32, (8, tm), 0)
    onehot = jnp.where(row8 == gsel, 1.0, 0.0)
    earlier = _dot(onehot.astype(BF16), tri_ref[...])
    rank = jnp.sum(onehot * (earlier + carry_ref[:, 0:1]), axis=0, keepdims=True)
    rt_ref[...] = jnp.where(row8 == 0, gsel, jnp.where(row8 == 1, rank.astype(I32), 0))
    carry_ref[...] = carry_ref[...] + jnp.sum(onehot, axis=1, keepdims=True)
    cnt_ref[...] = carry_ref[...].astype(I32)

    erow = lax.broadcasted_iota(I32, (LANE, tm), 0)
    gt = jnp.zeros((LANE, tm), F32)
    for e in range(MOE_EXPERTS):
        gt = jnp.where(erow == e, e_gate[e], gt)
    xa_ref[:, :d] = x
    xa_ref[:, d:] = gt.T


def _moe_route(h, wr, rb, tm=1024):
    n, d = h.shape
    tm = min(tm, n)
    full = lambda a: pl.BlockSpec(a.shape, lambda i: (0,) * a.ndim)
    return pl.pallas_call(
        _moe_route_kernel,
        out_shape=[jax.ShapeDtypeStruct((n, d + LANE), F32), jax.ShapeDtypeStruct((8, n), I32),
                   jax.ShapeDtypeStruct((8, LANE), I32)],
        grid=(n // tm,),
        in_specs=[pl.BlockSpec((tm, d), lambda i: (i, 0)), full(wr), full(rb)],
        out_specs=[pl.BlockSpec((tm, d + LANE), lambda i: (i, 0)),
                   pl.BlockSpec((8, tm), lambda i: (0, i)),
                   pl.BlockSpec((8, LANE), lambda i: (0, 0))],
        scratch_shapes=[pltpu.VMEM((tm, tm), BF16), pltpu.VMEM((8, LANE), F32)],
        compiler_params=_params(("arbitrary",)),
        name="moe_route",
    )(h, wr, rb)


def _row_copy_waves(n_rows, wave, start_row, wait_wave):
    n_waves = n_rows // wave
    for w in range(n_waves + 1):
        if w < n_waves:
            lax.fori_loop(w * wave, (w + 1) * wave, lambda r, c, w=w: (start_row(r, w % 2), c)[1], 0,
                          unroll=8)
        if w >= 1:
            wait_wave((w - 1) % 2)


def _moe_dispatch_kernel(pos_ref, fill_ref, xa_ref, xs_hbm, zero_ref, sem, *, wave):
    i = pl.program_id(0)
    tm = pos_ref.shape[1]
    tme = zero_ref.shape[0]

    def copy(r, s):
        return pltpu.make_async_copy(xa_ref.at[r], xs_hbm.at[pos_ref[0, r]], sem.at[s])

    def wait_wave(s):
        rows = pl.ds(0, wave)
        pltpu.make_async_copy(xa_ref.at[rows], xs_hbm.at[rows], sem.at[s]).wait()

    _row_copy_waves(tm, wave, lambda r, s: copy(r, s).start(), wait_wave)

    @pl.when(i == pl.num_programs(0) - 1)
    def _():
        zero_ref[...] = jnp.zeros(zero_ref.shape, F32)
        for g in range(MOE_GROUPS):
            first, count = fill_ref[0, g], fill_ref[1, g]

            def zcopy(r, first=first):
                return pltpu.make_async_copy(zero_ref.at[0], xs_hbm.at[first + r], sem.at[0])

            lax.fori_loop(0, count, lambda r, c: (zcopy(r).start(), c)[1], 0)
            lax.fori_loop(0, count, lambda r, c: (zcopy(r).wait(), c)[1], 0)
        n_tiles = fill_ref[2, 0]
        for k in range(MOE_GROUPS):
            @pl.when(k < fill_ref[2, 1])
            def _(k=k):
                tail = xs_hbm.at[pl.ds(pl.multiple_of((n_tiles + k) * tme, tme), tme)]
                cp = pltpu.make_async_copy(zero_ref, tail, sem.at[0])
                cp.start()
                cp.wait()


def _moe_dispatch(xa, pos, fill, n_sorted, tme, tm=1024, wave=256):
    n, da = xa.shape
    tm = min(tm, n)
    wave = min(wave, tm)
    return pl.pallas_call(
        functools.partial(_moe_dispatch_kernel, wave=wave),
        out_shape=jax.ShapeDtypeStruct((n_sorted, da), F32),
        grid=(n // tm,),
        in_specs=[pl.BlockSpec((1, tm), lambda i: (0, i), memory_space=pltpu.SMEM),
                  pl.BlockSpec(memory_space=pltpu.SMEM),
                  pl.BlockSpec((tm, da), lambda i: (i, 0))],
        out_specs=pl.BlockSpec(memory_space=pl.ANY),
        scratch_shapes=[pltpu.VMEM((tme, da), F32), pltpu.SemaphoreType.DMA((2,))],
        compiler_params=_params(("arbitrary",)),
        name="moe_dispatch",
    )(pos, fill, xa)


def _moe_expert_kernel(tg_ref, tj_ref, nt_ref, xs_ref, wg_ref, wu_ref, wd_ref, lg_ref, lb_ref,
                       ys_ref):
    d = ys_ref.shape[1]

    @pl.when(pl.program_id(0) < nt_ref[0])
    def _():
        x = xs_ref[:, :d]
        gates = xs_ref[:, d:]
        xb = x.astype(BF16)
        y = jnp.zeros(x.shape, F32)
        for e in range(MOE_EXPERTS):
            hg = _dot(xb, wg_ref[e])
            hu = _dot(xb, wu_ref[e])
            hid = hg * _sigmoid(hg) * hu * gates[:, e:e + 1]
            y = y + _dot(hid.astype(BF16), wd_ref[e])
        ys_ref[...] = _layer_norm(DN_ALPHA * x + y, lg_ref[...], lb_ref[...])

    @pl.when(pl.program_id(0) >= nt_ref[0])
    def _():
        ys_ref[...] = jnp.zeros(ys_ref.shape, F32)


def _moe_experts(xs, tile_g, tile_j, n_tiles, wg, wu, wd, lg, lb, tme):
    p, da = xs.shape
    d = da - LANE
    grp = lambda w: pl.BlockSpec((None,) + w.shape[1:], lambda i, tg, tj, nt: (tg[i], 0, 0, 0))
    full = lambda a: pl.BlockSpec(a.shape, lambda i, tg, tj, nt: (0,) * a.ndim)
    grid_spec = pltpu.PrefetchScalarGridSpec(
        num_scalar_prefetch=3,
        grid=(p // tme,),
        in_specs=[pl.BlockSpec((tme, da), lambda i, tg, tj, nt: (tj[i], 0)),
                  grp(wg), grp(wu), grp(wd), full(lg), full(lb)],
        out_specs=pl.BlockSpec((tme, d), lambda i, tg, tj, nt: (i, 0)),
    )
    return pl.pallas_call(
        _moe_expert_kernel,
        out_shape=jax.ShapeDtypeStruct((p, d), F32),
        grid_spec=grid_spec,
        compiler_params=_params(("arbitrary",)),
        name="moe_experts",
    )(tile_g, tile_j, n_tiles, xs, wg, wu, wd, lg, lb)


def _moe_combine_kernel(pos_ref, ys_hbm, o_ref, sem, *, wave):
    tm = o_ref.shape[0]

    def copy(r, s):
        return pltpu.make_async_copy(ys_hbm.at[pos_ref[0, r]], o_ref.at[r], sem.at[s])

    def wait_wave(s):
        rows = pl.ds(0, wave)
        pltpu.make_async_copy(ys_hbm.at[rows], o_ref.at[rows], sem.at[s]).wait()

    _row_copy_waves(tm, wave, lambda r, s: copy(r, s).start(), wait_wave)


def _moe_combine(ys, pos, n, tm=1024, wave=256):
    d = ys.shape[1]
    tm = min(tm, n)
    wave = min(wave, tm)
    return pl.pallas_call(
        functools.partial(_moe_combine_kernel, wave=wave),
        out_shape=jax.ShapeDtypeStruct((n, d), F32),
        grid=(n // tm,),
        in_specs=[pl.BlockSpec((1, tm), lambda i: (0, i), memory_space=pltpu.SMEM),
                  pl.BlockSpec(memory_space=pl.ANY)],
        out_specs=pl.BlockSpec((tm, d), lambda i: (i, 0)),
        scratch_shapes=[pltpu.SemaphoreType.DMA((2,))],
        compiler_params=_params(("arbitrary",)),
        name="moe_combine",
    )(pos, ys)


def _moe(h, wr, rb, wg, wu, wd, lg, lb, tme=512):
    n, d = h.shape
    tme = min(tme, n)
    xa, rt, cnt = _moe_route(h, wr, rb)
    counts = cnt[:MOE_GROUPS, 0]
    tiles = (counts + tme - 1) // tme
    first_tile = jnp.cumsum(tiles) - tiles
    gid, rank = rt[0], rt[1]
    pos = (first_tile[gid] * tme + rank)[None, :]
    n_slots = n // tme + MOE_GROUPS
    n_tiles = jnp.sum(tiles)
    fill = jnp.stack([first_tile * tme + counts, tiles * tme - counts,
                      jnp.zeros((MOE_GROUPS,), I32).at[0].set(n_tiles).at[1].set(n_slots - n_tiles)])
    slot = jnp.minimum(jnp.arange(n_slots, dtype=I32), n_tiles - 1)
    tile_g = jnp.sum(slot[:, None] >= (first_tile + tiles)[None, :], axis=1).astype(I32)
    xs = _moe_dispatch(xa, pos, fill, n_slots * tme, tme)
    ys = _moe_experts(xs, tile_g, slot, n_tiles[None].astype(I32), wg, wu, wd, lg, lb, tme)
    return _moe_combine(ys, pos, n)


def _s5_prep_kernel(lre_ref, lim_ref, ldt_ref, bre_ref, bim_ref, are_ref, aim_ref, bbre_ref,
                    bbim_ref):
    lr = jnp.minimum(lre_ref[...], -1e-4)
    li = lim_ref[...]
    dt = jnp.exp(ldt_ref[...])
    mag = jnp.exp(lr * dt)
    ab_re = mag * jnp.cos(li * dt)
    ab_im = mag * jnp.sin(li * dt)
    den = lr * lr + li * li
    nr = ab_re - 1.0
    coef_re = (nr * lr + ab_im * li) / den
    coef_im = (ab_im * lr - nr * li) / den
    are_ref[...] = ab_re
    aim_ref[...] = ab_im
    bbre_ref[...] = coef_re * bre_ref[...] - coef_im * bim_ref[...]
    bbim_ref[...] = coef_re * bim_ref[...] + coef_im * bre_ref[...]


def _s5_prep(lam_re, lam_im, log_dt, b_re, b_im):
    gp = C_GROUPS * C_STATE
    col = lambda a: a.reshape(gp, 1)
    ldt = jnp.broadcast_to(log_dt[:, None], (C_GROUPS, C_STATE))
    outs = pl.pallas_call(
        _s5_prep_kernel,
        out_shape=[jax.ShapeDtypeStruct((gp, 1), F32), jax.ShapeDtypeStruct((gp, 1), F32),
                   jax.ShapeDtypeStruct((gp, C_GROUP), F32), jax.ShapeDtypeStruct((gp, C_GROUP), F32)],
        name="s5_discretise",
    )(col(lam_re), col(lam_im), col(ldt), b_re.reshape(gp, C_GROUP), b_im.reshape(gp, C_GROUP))
    ab_re, ab_im, bb_re, bb_im = outs
    return (ab_re.reshape(C_GROUPS, C_STATE), ab_im.reshape(C_GROUPS, C_STATE),
            bb_re.reshape(C_GROUPS, C_STATE, C_GROUP), bb_im.reshape(C_GROUPS, C_STATE, C_GROUP))


def _s5_uproj_kernel(x_ref, w_ref, o_ref):
    o_ref[...] = _dot(x_ref[...].astype(BF16), w_ref[...])


def _s5_uproj(h2, w, bsz, seq, tm=512):
    n, d = h2.shape
    steps = seq // tm
    out = pl.pallas_call(
        _s5_uproj_kernel,
        out_shape=jax.ShapeDtypeStruct((seq, bsz * d), F32),
        grid=(bsz, steps),
        in_specs=[pl.BlockSpec((tm, d), lambda b, s: (b * steps + s, 0)),
                  pl.BlockSpec(w.shape, lambda b, s: (0, 0))],
        out_specs=pl.BlockSpec((tm, d), lambda b, s: (s, b)),
        compiler_params=_params(("arbitrary", "arbitrary")),
        name="s5_in_proj",
    )(h2, w)
    return out.reshape(seq * bsz, d)


def _gelu_tanh(x):
    return 0.5 * x * (1.0 + jnp.tanh(math.sqrt(2.0 / math.pi) * (x + 0.044715 * (x * x * x))))


def _s5_scan_kernel(u_ref, bbre_ref, bbim_ref, are_ref, aim_ref, cm_ref, d_ref, g_ref,
                    st_ref, bure_ref, buim_ref, s_ref, *, bsz, tl, n_cb, sub):
    @pl.when(pl.program_id(0) == 0)
    def _():
        st_ref[...] = jnp.zeros(st_ref.shape, F32)

    cw = D_MODEL // n_cb
    sw = bure_ref.shape[1]
    for cb in range(n_cb):
        ch = slice(cb * cw, (cb + 1) * cw)
        u = u_ref[:, ch]
        ub = u.astype(BF16)
        bure_ref[...] = _dot(ub, bbre_ref[cb])
        buim_ref[...] = _dot(ub, bbim_ref[cb])
        for s0 in range(0, sw, sub):
            cs = slice(s0, s0 + sub)
            a_re = jnp.broadcast_to(are_ref[cb, :, cs], (bsz, sub))
            a_im = jnp.broadcast_to(aim_ref[cb, :, cs], (bsz, sub))

            def step(t, carry, cs=cs, a_re=a_re, a_im=a_im):
                s_re, s_im = carry
                r0 = pl.multiple_of(t * bsz, bsz)
                n_re = a_re * s_re - a_im * s_im + bure_ref[pl.ds(r0, bsz), cs]
                n_im = a_re * s_im + a_im * s_re + buim_ref[pl.ds(r0, bsz), cs]
                s_ref[pl.ds(r0, bsz), cs] = n_re.astype(BF16)
                s_ref[pl.ds(r0, bsz), sw + s0:sw + s0 + sub] = n_im.astype(BF16)
                return n_re, n_im

            s_re, s_im = lax.fori_loop(0, tl, step, (st_ref[cb, 0, :, cs], st_ref[cb, 1, :, cs]),
                                       unroll=2)
            st_ref[cb, 0, :, cs] = s_re
            st_ref[cb, 1, :, cs] = s_im
        y = _dot(s_ref[...], cm_ref[cb]) + d_ref[:, ch] * u
        g_ref[:, ch] = _gelu_tanh(y).astype(g_ref.dtype)


def _s5_scan(u_lb, bbre, bbim, a_re, a_im, cmat, d_skip, bsz, seq, tl=32, sub=512):
    n, d = u_lb.shape
    n_cb, cw, sw = bbre.shape
    rows = tl * bsz
    full = lambda a: pl.BlockSpec(a.shape, lambda t: (0,) * a.ndim)
    return pl.pallas_call(
        functools.partial(_s5_scan_kernel, bsz=bsz, tl=tl, n_cb=n_cb, sub=sub),
        out_shape=jax.ShapeDtypeStruct((n, d), BF16),
        grid=(seq // tl,),
        in_specs=[pl.BlockSpec((rows, d), lambda t: (t, 0)), full(bbre), full(bbim), full(a_re),
                  full(a_im), full(cmat), full(d_skip)],
        out_specs=pl.BlockSpec((rows, d), lambda t: (t, 0)),
        scratch_shapes=[pltpu.VMEM((n_cb, 2, bsz, sw), F32), pltpu.VMEM((rows, sw), F32),
                        pltpu.VMEM((rows, sw), F32), pltpu.VMEM((rows, 2 * sw), BF16)],
        compiler_params=_params(("arbitrary",)),
        name="s5_scan",
    )(u_lb, bbre, bbim, a_re, a_im, cmat, d_skip)


def _s5_glu_kernel(g_ref, x_ref, w1_ref, w2_ref, wo_ref, lg_ref, lb_ref, out_ref):
    g = g_ref[...]
    z = _dot(g, w1_ref[...]) * _sigmoid(_dot(g, w2_ref[...]))
    m = _dot(z.astype(BF16), wo_ref[...])
    out_ref[...] = _layer_norm(DN_ALPHA * x_ref[...] + m, lg_ref[...], lb_ref[...])


def _s5_glu(g_lb, h2, w1, w2, wo, lg, lb, bsz, seq, tm=512):
    n, d = h2.shape
    steps = seq // tm
    full = lambda a: pl.BlockSpec(a.shape, lambda b, s: (0, 0))
    return pl.pallas_call(
        _s5_glu_kernel,
        out_shape=jax.ShapeDtypeStruct((n, d), F32),
        grid=(bsz, steps),
        in_specs=[pl.BlockSpec((tm, d), lambda b, s: (s, b)),
                  pl.BlockSpec((tm, d), lambda b, s: (b * steps + s, 0)),
                  full(w1), full(w2), full(wo), full(lg), full(lb)],
        out_specs=pl.BlockSpec((tm, d), lambda b, s: (b * steps + s, 0)),
        compiler_params=_params(("arbitrary", "arbitrary")),
        name="s5_glu_out_ln",
    )(g_lb.reshape(seq, bsz * d), h2, w1, w2, wo, lg, lb)


def _pad_heads(w, heads, dim):
    d = w.shape[0]
    w = w.reshape(d, heads, dim)
    return jnp.pad(w, ((0, 0), (0, 0), (0, LANE - dim))).reshape(d, heads * LANE)


def _even_layer_weights(w_in, gate_w2, gate_b):
    splits = (A_HEADS * A_HEAD_DIM, A_KV_HEADS * A_HEAD_DIM, A_KV_HEADS * A_HEAD_DIM,
              IDX_HEADS * IDX_DIM, IDX_DIM, IDX_HEADS,
              B_HEADS * B_KEY_DIM, B_HEADS * B_KEY_DIM, B_HEADS * B_VAL_DIM, GATE_RANK,
              B_HEADS * B_VAL_DIM)
    offs = np.cumsum((0,) + splits)
    aq, ak, av, iq, ik, iw, bq, bk, bv, bg, br = [w_in[:, offs[k]:offs[k + 1]] for k in range(11)]
    d = w_in.shape[0]
    slab = jnp.concatenate(
        [ik, iw, bg, jnp.zeros((d, LANE - IDX_DIM - IDX_HEADS - GATE_RANK), w_in.dtype)], axis=1)
    w_a = jnp.concatenate([_pad_heads(aq, A_HEADS, A_HEAD_DIM),
                           _pad_heads(ak, A_KV_HEADS, A_HEAD_DIM), av], axis=1)
    w_f = jnp.concatenate([_pad_heads(iq, IDX_HEADS, IDX_DIM), _pad_heads(bq, B_HEADS, B_KEY_DIM),
                           _pad_heads(bk, B_HEADS, B_KEY_DIM), bv, br, slab], axis=1)
    w_all = jnp.concatenate([w_a, w_f], axis=1).astype(BF16)
    g0 = IDX_DIM + IDX_HEADS
    w2p = jnp.zeros((LANE, B_HEADS * LANE), F32).at[g0:g0 + GATE_RANK].set(
        _pad_heads(gate_w2, B_HEADS, B_KEY_DIM)).astype(BF16)
    gbp = _pad_heads(gate_b[None, :], B_HEADS, B_KEY_DIM)
    return w_all, (w_a.shape[1], w_f.shape[1]), w2p, gbp


def _even_mixer_ln(x2, bsz, seq, w_in, bias_tiles, gate_w2, gate_b, norm_g, w_out, ln_g, ln_b):
    w_all, widths, w2p, gbp = _even_layer_weights(w_in, gate_w2, gate_b)
    h_a, h_f = _in_proj(x2, w_all, widths, (BF16, F32))
    o_a = _dsa(h_a, h_f, bias_tiles, bsz, seq)
    o_b = _gla(h_f, w2p, gbp, norm_g[None, :], bsz, seq)
    return _outproj_ln(o_a, o_b, x2, w_out.astype(BF16), ln_g[None, :], ln_b[None, :])


def _block_diag(m, per):
    g, r, c = m.shape
    eye = jnp.eye(per, dtype=m.dtype)
    m = m.reshape(g // per, per, r, c)
    bd = m[:, :, :, None, :] * eye[None, :, None, :, None]
    return bd.reshape(g // per, per * r, per * c)


def _s5_mixer_ln(h2, bsz, seq, w_in, lam_re, lam_im, log_dt, b_re, b_im, c_re, c_im, d_skip,
                 glu_w1, glu_w2, w_out, ln_g, ln_b, groups_per_block=16):
    ab_re, ab_im, bb_re, bb_im = _s5_prep(lam_re, lam_im, log_dt, b_re, b_im)
    per = groups_per_block
    n_cb = C_GROUPS // per
    bbre = _block_diag(jnp.swapaxes(bb_re, 1, 2), per).astype(BF16)
    bbim = _block_diag(jnp.swapaxes(bb_im, 1, 2), per).astype(BF16)
    a_re = ab_re.reshape(n_cb, 1, per * C_STATE)
    a_im = ab_im.reshape(n_cb, 1, per * C_STATE)
    cre = _block_diag(jnp.swapaxes(c_re, 1, 2), per)
    cim = _block_diag(jnp.swapaxes(c_im, 1, 2), per)
    cmat = jnp.concatenate([cre, -cim], axis=1).astype(BF16)
    u_lb = _s5_uproj(h2, w_in.astype(BF16), bsz, seq)
    g_lb = _s5_scan(u_lb, bbre, bbim, a_re, a_im, cmat, d_skip[None, :], bsz, seq)
    return _s5_glu(g_lb, h2, glu_w1.astype(BF16), glu_w2.astype(BF16), w_out.astype(BF16),
                   ln_g[None, :], ln_b[None, :], bsz, seq)


def _moe_ln(h, r_coarse, rb_coarse, r_fine, rb_fine, w_gate, w_up, w_down, ln_g, ln_b):
    d = h.shape[1]
    rows = MOE_GROUPS + N_EXPERTS
    wr = jnp.concatenate([r_coarse.T, jnp.transpose(r_fine, (0, 2, 1)).reshape(N_EXPERTS, d),
                          jnp.zeros((32 - rows, d), F32)], axis=0)
    rb = jnp.concatenate([rb_coarse, rb_fine.reshape(N_EXPERTS), jnp.zeros((32 - rows,), F32)])
    return _moe(h, wr, rb[:, None], w_gate.astype(BF16), w_up.astype(BF16), w_down.astype(BF16),
                ln_g[None, :], ln_b[None, :])


def kernel(x, rel_bias, ab_w_in, gla_gate_w2, gla_gate_b, gla_norm_g, ab_w_out, s5_w_in, s5_lam_re, s5_lam_im, s5_log_dt, s5_b_re, s5_b_im, s5_c_re, s5_c_im, s5_d, s5_glu_w1, s5_glu_w2, s5_w_out, ln_mix_g, ln_mix_b, ln_ffn_g, ln_ffn_b, moe_r_coarse, moe_rb_coarse, moe_r_fine, moe_rb_fine, moe_w_gate, moe_w_up, moe_w_down):
    bsz, seq, d = x.shape
    h = x.reshape(bsz * seq, d)
    bias_tiles = _bias_tiles(rel_bias)
    for layer in range(DEPTH):
        i = layer // 2
        if layer % 2 == 0:
            h = _even_mixer_ln(h, bsz, seq, ab_w_in[i], bias_tiles, gla_gate_w2[i], gla_gate_b[i],
                               gla_norm_g[i], ab_w_out[i], ln_mix_g[layer], ln_mix_b[layer])
        else:
            h = _s5_mixer_ln(h, bsz, seq, s5_w_in[i], s5_lam_re[i], s5_lam_im[i], s5_log_dt[i],
                             s5_b_re[i], s5_b_im[i], s5_c_re[i], s5_c_im[i], s5_d[i],
                             s5_glu_w1[i], s5_glu_w2[i], s5_w_out[i],
                             ln_mix_g[layer], ln_mix_b[layer])
        h = _moe_ln(h, moe_r_coarse[layer], moe_rb_coarse[layer], moe_r_fine[layer],
                    moe_rb_fine[layer], moe_w_gate[layer], moe_w_up[layer], moe_w_down[layer],
                    ln_ffn_g[layer], ln_ffn_b[layer])
    return h.reshape(bsz, seq, d).astype(x.dtype)
```

```python
import functools
import math

import numpy as np
import jax
import jax.numpy as jnp
from jax import lax
from jax.experimental import pallas as pl
from jax.experimental.pallas import tpu as pltpu

F32 = jnp.float32
BF16 = jnp.bfloat16
I32 = jnp.int32

D_MODEL = 1024
DEPTH = 2
DN_ALPHA = (2.0 * DEPTH) ** 0.25
LN_EPS = 1e-5
A_HEAD_DIM = 64
A_HEADS = 8
A_KV_HEADS = 2
IDX_HEADS = 4
IDX_DIM = 64
TOPK_MAX = 256
REL_BUCKETS = 32
REL_MAX_DIST = 128
B_HEADS = 4
B_VAL_DIM = 128
B_KEY_DIM = 64
GATE_RANK = 16
GATE_TAU = 16.0
GLA_CHUNK = 64
C_GROUP = 16
C_GROUPS = 64
C_STATE = 64
MOE_GROUPS = 4
MOE_EXPERTS = 4
MOE_HIDDEN = 512
N_EXPERTS = MOE_GROUPS * MOE_EXPERTS

LANE = 128
VMEM_LIMIT = 52 * 1024 * 1024
NEG_INF = float("-inf")
INT_MIN = -(2 ** 31)

_NT = (((1,), (1,)), ((), ()))
_TN = (((0,), (0,)), ((), ()))


def _dot(a, b):
    return jnp.dot(a, b, preferred_element_type=F32)


def _dot_nt(a, b):
    return lax.dot_general(a, b, _NT, preferred_element_type=F32)


def _dot_tn(a, b):
    return lax.dot_general(a, b, _TN, preferred_element_type=F32)


def _split3(x):
    hi = x.astype(BF16)
    r1 = x - hi.astype(F32)
    mid = r1.astype(BF16)
    lo = (r1 - mid.astype(F32)).astype(BF16)
    return hi, mid, lo


def _params(sem):
    return pltpu.CompilerParams(dimension_semantics=sem, vmem_limit_bytes=VMEM_LIMIT)


def _layer_norm(y, g, b):
    mu = jnp.mean(y, axis=-1, keepdims=True)
    d = y - mu
    var = jnp.mean(d * d, axis=-1, keepdims=True)
    return d * lax.rsqrt(var + LN_EPS) * g + b


def _sigmoid(x):
    return 1.0 / (1.0 + jnp.exp(-x))


def _bucket_thresholds():
    max_exact = REL_BUCKETS // 2
    nf = np.arange(max_exact, 4 * REL_MAX_DIST).astype(np.float32)
    large = max_exact + (np.log(nf / np.float32(max_exact))
                         / np.float32(math.log(REL_MAX_DIST / max_exact))
                         * np.float32(REL_BUCKETS - max_exact)).astype(np.int32)
    large = np.minimum(large, REL_BUCKETS - 1)
    return [int(nf[np.argmax(large >= max_exact + j)]) for j in range(1, REL_BUCKETS - max_exact)]


def _bias_kernel(rb_ref, out_ref):
    max_exact = REL_BUCKETS // 2
    srow = lax.broadcasted_iota(I32, (LANE, LANE), 0)
    tcol = lax.broadcasted_iota(I32, (LANE, LANE), 1)
    thr = _bucket_thresholds()
    for band in range(2):
        n = jnp.maximum(band * LANE + tcol - srow, 0)
        large = jnp.full((LANE, LANE), max_exact, I32)
        for t in thr:
            large = large + jnp.where(n >= t, 1, 0)
        bucket = jnp.where(n < max_exact, n, large)
        for h in range(A_HEADS):
            acc = jnp.zeros((LANE, LANE), F32)
            for bk in range(REL_BUCKETS):
                acc = acc + jnp.where(bucket == bk, rb_ref[bk, h], 0.0)
            out_ref[h, band] = acc
    for h in range(A_HEADS):
        out_ref[h, 2] = jnp.full((LANE, LANE), rb_ref[REL_BUCKETS - 1, h], F32)


def _bias_tiles(rel_bias):
    return pl.pallas_call(
        _bias_kernel,
        out_shape=jax.ShapeDtypeStruct((A_HEADS, 3, LANE, LANE), F32),
        in_specs=[pl.BlockSpec(memory_space=pltpu.SMEM)],
        name="rel_bias_tiles",
    )(rel_bias)


def _proj_kernel(x_ref, w_ref, *out_refs, widths, chunk):
    xb = x_ref[...].astype(BF16)
    c0 = 0
    for o_ref, width in zip(out_refs, widths):
        for c in range(0, width, chunk):
            ce = min(c + chunk, width)
            o_ref[:, c:ce] = _dot(xb, w_ref[:, c0 + c:c0 + ce]).astype(o_ref.dtype)
        c0 += width


def _in_proj(x2, w, widths, dtypes, tm=512, chunk=256):
    n, d = x2.shape
    return pl.pallas_call(
        functools.partial(_proj_kernel, widths=widths, chunk=chunk),
        out_shape=[jax.ShapeDtypeStruct((n, wd), dt) for wd, dt in zip(widths, dtypes)],
        grid=(n // tm,),
        in_specs=[pl.BlockSpec((tm, d), lambda i: (i, 0)),
                  pl.BlockSpec(w.shape, lambda i: (0, 0))],
        out_specs=[pl.BlockSpec((tm, wd), lambda i: (i, 0)) for wd in widths],
        compiler_params=_params(("arbitrary",)),
        name="in_proj",
    )(x2, w)


def _sortable(x):
    bits = lax.bitcast_convert_type(x, I32)
    return jnp.where(bits < 0, bits ^ jnp.int32(0x7FFFFFFF), bits)


def _dsa_kernel(q_ref, k_ref, v_ref, iq_ref, slab_ref, slabq_ref, bias_ref, o_ref,
                key_ref, am_ref, vt_ref, acc_ref, lg_ref, tmax_ref, ot_ref, *, k_sel, seq):
    i = pl.program_id(1)
    t0 = i * LANE
    nkt = i + 1
    kt2 = 2 * LANE
    n2 = lax.shift_right_logical(nkt + 1, 1)
    rep = A_HEADS // A_KV_HEADS
    srow_2 = lax.broadcasted_iota(I32, (kt2, LANE), 0)
    tcol = t0 + lax.broadcasted_iota(I32, (kt2, LANE), 1)

    @pl.when(i == 0)
    def _():
        for jt in range(seq // kt2):
            vt_ref[jt] = v_ref[jt * kt2:(jt + 1) * kt2, :].astype(F32).T.astype(BF16)

    slab_t = slabq_ref[...].T
    w_t = slab_t[IDX_DIM:IDX_DIM + IDX_HEADS, :] * (IDX_HEADS ** -0.5)
    iqb = iq_ref[...].astype(BF16)
    iq_stack = jnp.concatenate([iqb[:, h * LANE:(h + 1) * LANE] for h in range(IDX_HEADS)], axis=0)

    def score_tile(j, carry):
        s0 = pl.multiple_of(j * kt2, kt2)
        ikt = slab_ref[pl.ds(s0, kt2), :].astype(BF16)
        d = _dot_nt(ikt, iq_stack) * (IDX_DIM ** -0.5)
        acc = jnp.zeros((kt2, LANE), F32)
        for h in range(IDX_HEADS):
            acc = acc + jnp.maximum(d[:, h * LANE:(h + 1) * LANE], 0.0) * w_t[h:h + 1, :]
        acc = jnp.where(s0 + srow_2 <= tcol, acc, NEG_INF)
        key_ref[pl.ds(s0, kt2), :] = _sortable(acc)
        return carry

    lax.fori_loop(0, n2, score_tile, 0)
    key_ref[pl.ds(pl.multiple_of(nkt * LANE, LANE), LANE), :] = jnp.full((LANE, LANE), INT_MIN, I32)

    def count(pred_fn):
        def body(j, c):
            s0 = pl.multiple_of(j * kt2, kt2)
            kt = key_ref[pl.ds(s0, kt2), :]
            m = jnp.where(pred_fn(kt, s0 + srow_2), 1, 0)
            return c + jnp.sum(m.reshape(kt2 // 8, 8, LANE), axis=0)
        c = lax.fori_loop(0, n2, body, jnp.zeros((8, LANE), I32))
        return jnp.sum(c, axis=0, keepdims=True)

    def bcast(v):
        return jnp.broadcast_to(v, (kt2, LANE))

    def search():
        c0 = count(lambda kt, s: kt >= 0)
        ans0 = jnp.where(c0 >= k_sel, 0, INT_MIN).astype(I32)

        def bit_body(bi, ans):
            cand = ans | lax.shift_left(jnp.int32(1), 30 - bi)
            cb = bcast(cand)
            cnt = count(lambda kt, s: kt >= cb)
            return jnp.where(cnt >= k_sel, cand, ans)

        ans = lax.fori_loop(0, 31, bit_body, ans0)
        ab = bcast(ans)
        cnt_ge = count(lambda kt, s: kt >= ab)
        cnt_gt = count(lambda kt, s: kt > ab)
        need = k_sel - cnt_gt

        def tie_search():
            def idx_body(bi, x):
                cand = x | lax.shift_left(jnp.int32(1), (seq.bit_length() - 2) - bi)
                cb = bcast(cand)
                f = count(lambda kt, s: (kt == ab) & (s < cb))
                return jnp.where(f < need, cand, x)
            return lax.fori_loop(0, seq.bit_length() - 1, idx_body, jnp.zeros((1, LANE), I32))

        cut = lax.cond(jnp.max(cnt_ge) > k_sel, tie_search,
                       lambda: jnp.full((1, LANE), seq - 1, I32))
        return ans, cut

    ans, cut = lax.cond(t0 >= k_sel, search,
                        lambda: (jnp.full((1, LANE), INT_MIN, I32),
                                 jnp.full((1, LANE), seq - 1, I32)))
    ans_b = bcast(ans)
    cut_b = bcast(cut)

    def mask_tile(j, carry):
        s0 = pl.multiple_of(j * kt2, kt2)
        kt = key_ref[pl.ds(s0, kt2), :]
        srow = s0 + srow_2
        sel = (kt > ans_b) | ((kt == ans_b) & (srow <= cut_b))
        am_ref[pl.ds(s0, kt2), :] = jnp.where(sel & (srow <= tcol), 0.0, NEG_INF)
        return carry

    lax.fori_loop(0, n2, mask_tile, 0)

    scale = jnp.asarray(A_HEAD_DIM ** -0.5, BF16)
    q_stacks = [jnp.concatenate([q_ref[:, (g * rep + r) * LANE:(g * rep + r + 1) * LANE]
                                 for r in range(rep)], axis=0) * scale for g in range(A_KV_HEADS)]
    acc_ref[...] = jnp.zeros(acc_ref.shape, F32)

    def logits_stage(j):
        slot = j & 1
        s0 = pl.multiple_of(j * kt2, kt2)
        am = am_ref[pl.ds(s0, kt2), :]
        band0 = jnp.clip(i - 2 * j, 0, 2)
        band1 = jnp.clip(i - 2 * j - 1, 0, 2)
        for g in range(A_KV_HEADS):
            kt = k_ref[pl.ds(s0, kt2), g * LANE:(g + 1) * LANE]
            lg_all = _dot_nt(kt, q_stacks[g])
            for r in range(rep):
                h = g * rep + r
                cs = slice(r * LANE, (r + 1) * LANE)
                bias = jnp.concatenate([bias_ref[h, band0], bias_ref[h, band1]], axis=0)
                lg = lg_all[:, cs] + bias + am
                lg_ref[slot, g, :, cs] = lg
                tmax_ref[slot, h:h + 1, :] = jnp.max(lg, axis=0, keepdims=True)

    def softmax_stage(j, carry):
        ms, ls = carry
        slot = j & 1
        new_ms, new_ls = [], []
        for g in range(A_KV_HEADS):
            ps, alphas = [], []
            for r in range(rep):
                h = g * rep + r
                m_new = jnp.maximum(ms[h], tmax_ref[slot, h:h + 1, :])
                m_safe = jnp.where(m_new == NEG_INF, 0.0, m_new)
                alpha = jnp.exp(ms[h] - m_safe)
                p = jnp.exp(lg_ref[slot, g, :, r * LANE:(r + 1) * LANE] - m_safe)
                new_ms.append(m_new)
                new_ls.append(alpha * ls[h] + jnp.sum(p, axis=0, keepdims=True))
                ps.append(p.astype(BF16))
                alphas.append(alpha)
            vt = vt_ref[j, g * A_HEAD_DIM:(g + 1) * A_HEAD_DIM, :]
            pv = _dot(vt, jnp.concatenate(ps, axis=1))
            acc_ref[g] = acc_ref[g] * jnp.concatenate(alphas, axis=1) + pv
        return tuple(new_ms), tuple(new_ls)

    def tile_body(j, carry):
        carry = softmax_stage(j, carry)
        logits_stage(j + 1)
        return carry

    init = (tuple(jnp.full((1, LANE), NEG_INF, F32) for _ in range(A_HEADS)),
            tuple(jnp.zeros((1, LANE), F32) for _ in range(A_HEADS)))
    logits_stage(0)
    carry = lax.fori_loop(0, n2 - 1, tile_body, init)
    _, ls = softmax_stage(n2 - 1, carry)
    for g in range(A_KV_HEADS):
        for r in range(rep):
            h = g * rep + r
            ot_ref[h * A_HEAD_DIM:(h + 1) * A_HEAD_DIM, :] = (
                acc_ref[g, :, r * LANE:(r + 1) * LANE] / ls[h])
    o_ref[...] = ot_ref[...].T.astype(o_ref.dtype)


def _dsa(h_a, h_f, bias_tiles, bsz, seq):
    n = bsz * seq
    nq = seq // LANE
    k_sel = min(TOPK_MAX, seq // 4)
    assert k_sel % LANE == 0 and seq % (2 * LANE) == 0
    qw = A_HEADS * LANE
    slab_blk = (h_f.shape[1] - LANE) // LANE
    return pl.pallas_call(
        functools.partial(_dsa_kernel, k_sel=k_sel, seq=seq),
        out_shape=jax.ShapeDtypeStruct((n, A_HEADS * A_HEAD_DIM), BF16),
        grid=(bsz, nq),
        in_specs=[
            pl.BlockSpec((LANE, qw), lambda b, i: (b * nq + i, 0)),
            pl.BlockSpec((seq, A_KV_HEADS * LANE), lambda b, i: (b, qw // (A_KV_HEADS * LANE))),
            pl.BlockSpec((seq, LANE), lambda b, i: (b, (qw + A_KV_HEADS * LANE) // LANE)),
            pl.BlockSpec((LANE, IDX_HEADS * LANE), lambda b, i: (b * nq + i, 0)),
            pl.BlockSpec((seq, LANE), lambda b, i: (b, slab_blk)),
            pl.BlockSpec((LANE, LANE), lambda b, i: (b * nq + i, slab_blk)),
            pl.BlockSpec((A_HEADS, 3, LANE, LANE), lambda b, i: (0, 0, 0, 0)),
        ],
        out_specs=pl.BlockSpec((LANE, A_HEADS * A_HEAD_DIM), lambda b, i: (b * nq + i, 0)),
        scratch_shapes=[
            pltpu.VMEM((seq + LANE, LANE), I32),
            pltpu.VMEM((seq, LANE), F32),
            pltpu.VMEM((nq // 2, LANE, 2 * LANE), BF16),
            pltpu.VMEM((A_KV_HEADS, A_HEAD_DIM, (A_HEADS // A_KV_HEADS) * LANE), F32),
            pltpu.VMEM((2, A_KV_HEADS, 2 * LANE, (A_HEADS // A_KV_HEADS) * LANE), F32),
            pltpu.VMEM((2, A_HEADS, LANE), F32),
            pltpu.VMEM((A_HEADS * A_HEAD_DIM, LANE), F32),
        ],
        compiler_params=_params(("arbitrary", "arbitrary")),
        name="dsa_attention",
    )(h_a, h_a, h_a, h_f, h_f, h_f, bias_tiles)


def _gla_kernel(bq_ref, bk_ref, bv_ref, br_ref, slab_ref, w2_ref, gb_ref, ng_ref, o_ref,
                st_ref, *, n_chunks):
    @pl.when(pl.program_id(1) == 0)
    def _():
        st_ref[...] = jnp.zeros(st_ref.shape, F32)

    ch = GLA_CHUNK
    row = lax.broadcasted_iota(I32, (ch, ch), 0)
    col = lax.broadcasted_iota(I32, (ch, ch), 1)
    tril = row >= col
    tri = jnp.where(tril, 1.0, 0.0).astype(BF16)
    w2 = w2_ref[...]
    for c in range(n_chunks):
        rows = slice(c * ch, (c + 1) * ch)
        gate = _dot(slab_ref[rows, :].astype(BF16), w2) + gb_ref[...]
        log_a = (jnp.minimum(gate, 0.0) - jnp.log1p(jnp.exp(-jnp.abs(gate)))) / GATE_TAU
        hi, mid, lo = _split3(log_a)
        cum = _dot(tri, hi) + _dot(tri, mid) + _dot(tri, lo)
        last = cum[ch - 1:ch, :]
        q = bq_ref[rows, :] * (B_KEY_DIM ** -0.5)
        k = bk_ref[rows, :]
        q_dec = (q * jnp.exp(cum)).astype(BF16)
        k_inv = (k * jnp.exp(-cum)).astype(BF16)
        k_end = (k * jnp.exp(last - cum)).astype(BF16)
        decay = jnp.exp(last)
        for h in range(B_HEADS):
            cs = slice(h * LANE, (h + 1) * LANE)
            v = bv_ref[rows, cs].astype(BF16)
            sc = jnp.where(tril, _dot_nt(q_dec[:, cs], k_inv[:, cs]), 0.0)
            st = st_ref[h]
            o = _dot(sc.astype(BF16), v) + _dot_nt(q_dec[:, cs], st.astype(BF16))
            st_ref[h] = decay[:, cs] * st + _dot_tn(v, k_end[:, cs])
            mu = jnp.mean(o, axis=-1, keepdims=True)
            dlt = o - mu
            var = jnp.mean(dlt * dlt, axis=-1, keepdims=True)
            on = dlt * lax.rsqrt(var + LN_EPS) * ng_ref[:, cs]
            r = br_ref[rows, cs]
            o_ref[rows, cs] = (on * (r * _sigmoid(r))).astype(o_ref.dtype)


def _gla(h_f, w2p, gbp, norm_g, bsz, seq, rows_per_step=256):
    n = bsz * seq
    hw = B_HEADS * LANE
    steps = seq // rows_per_step
    slab_blk = (h_f.shape[1] - LANE) // LANE
    blk = lambda cb: pl.BlockSpec((rows_per_step, hw), lambda b, s, cb=cb: (b * steps + s, cb))
    return pl.pallas_call(
        functools.partial(_gla_kernel, n_chunks=rows_per_step // GLA_CHUNK),
        out_shape=jax.ShapeDtypeStruct((n, hw), BF16),
        grid=(bsz, steps),
        in_specs=[blk(1), blk(2), blk(3), blk(4),
                  pl.BlockSpec((rows_per_step, LANE), lambda b, s: (b * steps + s, slab_blk)),
                  pl.BlockSpec((LANE, hw), lambda b, s: (0, 0)),
                  pl.BlockSpec((1, hw), lambda b, s: (0, 0)),
                  pl.BlockSpec((1, hw), lambda b, s: (0, 0))],
        out_specs=pl.BlockSpec((rows_per_step, hw), lambda b, s: (b * steps + s, 0)),
        scratch_shapes=[pltpu.VMEM((B_HEADS, LANE, LANE), F32)],
        compiler_params=_params(("arbitrary", "arbitrary")),
        name="gla_attention",
    )(h_f, h_f, h_f, h_f, h_f, w2p, gbp, norm_g)


def _outproj_ln_kernel(oa_ref, ob_ref, x_ref, w_ref, g_ref, b_ref, out_ref):
    ka = oa_ref.shape[1]
    m = _dot(oa_ref[...], w_ref[0:ka, :]) + _dot(ob_ref[...], w_ref[ka:, :])
    out_ref[...] = _layer_norm(DN_ALPHA * x_ref[...] + m, g_ref[...], b_ref[...])


def _outproj_ln(o_a, o_b, x2, w, g, b, tm=512):
    n, d = x2.shape
    row = lambda width: pl.BlockSpec((tm, width), lambda i: (i, 0))
    full = lambda a: pl.BlockSpec(a.shape, lambda i: (0, 0))
    return pl.pallas_call(
        _outproj_ln_kernel,
        out_shape=jax.ShapeDtypeStruct((n, d), F32),
        grid=(n // tm,),
        in_specs=[row(o_a.shape[1]), row(o_b.shape[1]), row(d), full(w), full(g), full(b)],
        out_specs=row(d),
        compiler_params=_params(("arbitrary",)),
        name="out_proj_ln",
    )(o_a, o_b, x2, w, g, b)


def _route(lt):
    gl = [lt[g:g + 1, :] for g in range(MOE_GROUPS)]
    best, gsel = gl[0], jnp.zeros_like(gl[0], dtype=I32)
    for g in range(1, MOE_GROUPS):
        better = gl[g] > best
        gsel = jnp.where(better, g, gsel)
        best = jnp.where(better, gl[g], best)
    denom = sum(jnp.exp(x - best) for x in gl)
    g_w = 1.0 / denom
    fl = []
    for e in range(MOE_EXPERTS):
        acc = jnp.zeros_like(best)
        for g in range(MOE_GROUPS):
            r = MOE_GROUPS + g * MOE_EXPERTS + e
            acc = jnp.where(gsel == g, lt[r:r + 1, :], acc)
        fl.append(acc)
    v1, i1 = fl[0], jnp.zeros_like(gsel)
    for e in range(1, MOE_EXPERTS):
        better = fl[e] > v1
        i1 = jnp.where(better, e, i1)
        v1 = jnp.where(better, fl[e], v1)
    v2, i2 = jnp.full_like(v1, NEG_INF), jnp.zeros_like(gsel)
    for e in range(MOE_EXPERTS):
        better = (fl[e] > v2) & (i1 != e)
        i2 = jnp.where(better, e, i2)
        v2 = jnp.where(better, fl[e], v2)
    e2 = jnp.exp(v2 - v1)
    w1 = (1.0 / (1.0 + e2)) * g_w
    w2 = (e2 / (1.0 + e2)) * g_w
    e_gate = [jnp.where(i1 == e, w1, 0.0) + jnp.where(i2 == e, w2, 0.0)
              for e in range(MOE_EXPERTS)]
    return gsel, e_gate


def _moe_route_kernel(x_ref, wr_ref, rb_ref, xa_ref, rt_ref, cnt_ref, tri_ref, carry_ref):
    i = pl.program_id(0)
    tm, d = x_ref.shape

    @pl.when(i == 0)
    def _():
        carry_ref[...] = jnp.zeros(carry_ref.shape, F32)
        before = (lax.broadcasted_iota(I32, (tm, tm), 0) < lax.broadcasted_iota(I32, (tm, tm), 1))
        tri_ref[...] = jnp.where(before, 1.0, 0.0).astype(BF16)

    x = x_ref[...]
    xs = _split3(x)
    ws = _split3(wr_ref[...])
    lt = jnp.zeros((wr_ref.shape[0], tm), F32)
    for a, b in ((2, 0), (0, 2), (1, 1), (1, 0), (0, 1), (0, 0)):
        lt = lt + _dot_nt(ws[a], xs[b])
    lt = lt + rb_ref[...]
    gsel, e_gate = _route(lt)

    row8 = lax.broadcasted_iota(I32, (8, tm), 0)
    onehot = jnp.where(row8 == gsel, 1.0, 0.0)
    earlier = _dot(onehot.astype(BF16), tri_ref[...])
    rank = jnp.sum(onehot * (earlier + carry_ref[:, 0:1]), axis=0, keepdims=True)
    rt_ref[...] = jnp.where(row8 == 0, gsel, jnp.where(row8 == 1, rank.astype(I32), 0))
    carry_ref[...] = carry_ref[...] + jnp.sum(onehot, axis=1, keepdims=True)
    cnt_ref[...] = carry_ref[...].astype(I32)

    erow = lax.broadcasted_iota(I32, (LANE, tm), 0)
    gt = jnp.zeros((LANE, tm), F32)
    for e in range(MOE_EXPERTS):
        gt = jnp.where(erow == e, e_gate[e], gt)
    xa_ref[:, :d] = x
    xa_ref[:, d:] = gt.T


def _moe_route(h, wr, rb, tm=1024):
    n, d = h.shape
    tm = min(tm, n)
    full = lambda a: pl.BlockSpec(a.shape, lambda i: (0,) * a.ndim)
    return pl.pallas_call(
        _moe_route_kernel,
        out_shape=[jax.ShapeDtypeStruct((n, d + LANE), F32), jax.ShapeDtypeStruct((8, n), I32),
                   jax.ShapeDtypeStruct((8, LANE), I32)],
        grid=(n // tm,),
        in_specs=[pl.BlockSpec((tm, d), lambda i: (i, 0)), full(wr), full(rb)],
        out_specs=[pl.BlockSpec((tm, d + LANE), lambda i: (i, 0)),
                   pl.BlockSpec((8, tm), lambda i: (0, i)),
                   pl.BlockSpec((8, LANE), lambda i: (0, 0))],
        scratch_shapes=[pltpu.VMEM((tm, tm), BF16), pltpu.VMEM((8, LANE), F32)],
        compiler_params=_params(("arbitrary",)),
        name="moe_route",
    )(h, wr, rb)


def _row_copy_waves(n_rows, wave, start_row, wait_wave):
    n_waves = n_rows // wave
    for w in range(n_waves + 1):
        if w < n_waves:
            lax.fori_loop(w * wave, (w + 1) * wave, lambda r, c, w=w: (start_row(r, w % 2), c)[1], 0,
                          unroll=8)
        if w >= 1:
            wait_wave((w - 1) % 2)


def _moe_dispatch_kernel(pos_ref, fill_ref, xa_ref, xs_hbm, zero_ref, sem, *, wave):
    i = pl.program_id(0)
    tm = pos_ref.shape[1]
    tme = zero_ref.shape[0]

    def copy(r, s):
        return pltpu.make_async_copy(xa_ref.at[r], xs_hbm.at[pos_ref[0, r]], sem.at[s])

    def wait_wave(s):
        rows = pl.ds(0, wave)
        pltpu.make_async_copy(xa_ref.at[rows], xs_hbm.at[rows], sem.at[s]).wait()

    _row_copy_waves(tm, wave, lambda r, s: copy(r, s).start(), wait_wave)

    @pl.when(i == pl.num_programs(0) - 1)
    def _():
        zero_ref[...] = jnp.zeros(zero_ref.shape, F32)
        for g in range(MOE_GROUPS):
            first, count = fill_ref[0, g], fill_ref[1, g]

            def zcopy(r, first=first):
                return pltpu.make_async_copy(zero_ref.at[0], xs_hbm.at[first + r], sem.at[0])

            lax.fori_loop(0, count, lambda r, c: (zcopy(r).start(), c)[1], 0)
            lax.fori_loop(0, count, lambda r, c: (zcopy(r).wait(), c)[1], 0)
        n_tiles = fill_ref[2, 0]
        for k in range(MOE_GROUPS):
            @pl.when(k < fill_ref[2, 1])
            def _(k=k):
                tail = xs_hbm.at[pl.ds(pl.multiple_of((n_tiles + k) * tme, tme), tme)]
                cp = pltpu.make_async_copy(zero_ref, tail, sem.at[0])
                cp.start()
                cp.wait()


def _moe_dispatch(xa, pos, fill, n_sorted, tme, tm=1024, wave=256):
    n, da = xa.shape
    tm = min(tm, n)
    wave = min(wave, tm)
    return pl.pallas_call(
        functools.partial(_moe_dispatch_kernel, wave=wave),
        out_shape=jax.ShapeDtypeStruct((n_sorted, da), F32),
        grid=(n // tm,),
        in_specs=[pl.BlockSpec((1, tm), lambda i: (0, i), memory_space=pltpu.SMEM),
                  pl.BlockSpec(memory_space=pltpu.SMEM),
                  pl.BlockSpec((tm, da), lambda i: (i, 0))],
        out_specs=pl.BlockSpec(memory_space=pl.ANY),
        scratch_shapes=[pltpu.VMEM((tme, da), F32), pltpu.SemaphoreType.DMA((2,))],
        compiler_params=_params(("arbitrary",)),
        name="moe_dispatch",
    )(pos, fill, xa)


def _moe_expert_kernel(tg_ref, tj_ref, nt_ref, xs_ref, wg_ref, wu_ref, wd_ref, lg_ref, lb_ref,
                       ys_ref):
    d = ys_ref.shape[1]

    @pl.when(pl.program_id(0) < nt_ref[0])
    def _():
        x = xs_ref[:, :d]
        gates = xs_ref[:, d:]
        xb = x.astype(BF16)
        y = jnp.zeros(x.shape, F32)
        for e in range(MOE_EXPERTS):
            hg = _dot(xb, wg_ref[e])
            hu = _dot(xb, wu_ref[e])
            hid = hg * _sigmoid(hg) * hu * gates[:, e:e + 1]
            y = y + _dot(hid.astype(BF16), wd_ref[e])
        ys_ref[...] = _layer_norm(DN_ALPHA * x + y, lg_ref[...], lb_ref[...])

    @pl.when(pl.program_id(0) >= nt_ref[0])
    def _():
        ys_ref[...] = jnp.zeros(ys_ref.shape, F32)


def _moe_experts(xs, tile_g, tile_j, n_tiles, wg, wu, wd, lg, lb, tme):
    p, da = xs.shape
    d = da - LANE
    grp = lambda w: pl.BlockSpec((None,) + w.shape[1:], lambda i, tg, tj, nt: (tg[i], 0, 0, 0))
    full = lambda a: pl.BlockSpec(a.shape, lambda i, tg, tj, nt: (0,) * a.ndim)
    grid_spec = pltpu.PrefetchScalarGridSpec(
        num_scalar_prefetch=3,
        grid=(p // tme,),
        in_specs=[pl.BlockSpec((tme, da), lambda i, tg, tj, nt: (tj[i], 0)),
                  grp(wg), grp(wu), grp(wd), full(lg), full(lb)],
        out_specs=pl.BlockSpec((tme, d), lambda i, tg, tj, nt: (i, 0)),
    )
    return pl.pallas_call(
        _moe_expert_kernel,
        out_shape=jax.ShapeDtypeStruct((p, d), F32),
        grid_spec=grid_spec,
        compiler_params=_params(("arbitrary",)),
        name="moe_experts",
    )(tile_g, tile_j, n_tiles, xs, wg, wu, wd, lg, lb)


def _moe_combine_kernel(pos_ref, ys_hbm, o_ref, sem, *, wave):
    tm = o_ref.shape[0]

    def copy(r, s):
        return pltpu.make_async_copy(ys_hbm.at[pos_ref[0, r]], o_ref.at[r], sem.at[s])

    def wait_wave(s):
        rows = pl.ds(0, wave)
        pltpu.make_async_copy(ys_hbm.at[rows], o_ref.at[rows], sem.at[s]).wait()

    _row_copy_waves(tm, wave, lambda r, s: copy(r, s).start(), wait_wave)


def _moe_combine(ys, pos, n, tm=1024, wave=256):
    d = ys.shape[1]
    tm = min(tm, n)
    wave = min(wave, tm)
    return pl.pallas_call(
        functools.partial(_moe_combine_kernel, wave=wave),
        out_shape=jax.ShapeDtypeStruct((n, d), F32),
        grid=(n // tm,),
        in_specs=[pl.BlockSpec((1, tm), lambda i: (0, i), memory_space=pltpu.SMEM),
                  pl.BlockSpec(memory_space=pl.ANY)],
        out_specs=pl.BlockSpec((tm, d), lambda i: (i, 0)),
        scratch_shapes=[pltpu.SemaphoreType.DMA((2,))],
        compiler_params=_params(("arbitrary",)),
        name="moe_combine",
    )(pos, ys)


def _moe(h, wr, rb, wg, wu, wd, lg, lb, regroup=None, tme=512):
    n, d = h.shape
    tme = min(tme, n)
    xa, rt, cnt = _moe_route(h, wr, rb)
    counts = cnt[:MOE_GROUPS, 0]
    tiles = (counts + tme - 1) // tme
    first_tile = jnp.cumsum(tiles) - tiles
    gid, rank = rt[0], rt[1]
    pos = (first_tile[gid] * tme + rank)[None, :]
    n_slots = n // tme + MOE_GROUPS
    n_tiles = jnp.sum(tiles)
    fill = jnp.stack([first_tile * tme + counts, tiles * tme - counts,
                      jnp.zeros((MOE_GROUPS,), I32).at[0].set(n_tiles).at[1].set(n_slots - n_tiles)])
    slot = jnp.minimum(jnp.arange(n_slots, dtype=I32), n_tiles - 1)
    tile_g = jnp.sum(slot[:, None] >= (first_tile + tiles)[None, :], axis=1).astype(I32)
    xs = _moe_dispatch(xa, pos, fill, n_slots * tme, tme)
    ys = _moe_experts(xs, tile_g, slot, n_tiles[None].astype(I32), wg, wu, wd, lg, lb, tme)
    if regroup is not None:
        pos = pos.reshape(regroup).T.reshape(1, n)
    return _moe_combine(ys, pos, n)


def _s5_prep_kernel(lre_ref, lim_ref, ldt_ref, bre_ref, bim_ref, are_ref, aim_ref, bbre_ref,
                    bbim_ref):
    lr = jnp.minimum(lre_ref[...], -1e-4)
    li = lim_ref[...]
    dt = jnp.exp(ldt_ref[...])
    mag = jnp.exp(lr * dt)
    ab_re = mag * jnp.cos(li * dt)
    ab_im = mag * jnp.sin(li * dt)
    den = lr * lr + li * li
    nr = ab_re - 1.0
    coef_re = (nr * lr + ab_im * li) / den
    coef_im = (ab_im * lr - nr * li) / den
    are_ref[...] = ab_re
    aim_ref[...] = ab_im
    bbre_ref[...] = coef_re * bre_ref[...] - coef_im * bim_ref[...]
    bbim_ref[...] = coef_re * bim_ref[...] + coef_im * bre_ref[...]


def _s5_prep(lam_re, lam_im, log_dt, b_re, b_im):
    gp = C_GROUPS * C_STATE
    col = lambda a: a.reshape(gp, 1)
    ldt = jnp.broadcast_to(log_dt[:, None], (C_GROUPS, C_STATE))
    outs = pl.pallas_call(
        _s5_prep_kernel,
        out_shape=[jax.ShapeDtypeStruct((gp, 1), F32), jax.ShapeDtypeStruct((gp, 1), F32),
                   jax.ShapeDtypeStruct((gp, C_GROUP), F32), jax.ShapeDtypeStruct((gp, C_GROUP), F32)],
        name="s5_discretise",
    )(col(lam_re), col(lam_im), col(ldt), b_re.reshape(gp, C_GROUP), b_im.reshape(gp, C_GROUP))
    ab_re, ab_im, bb_re, bb_im = outs
    return (ab_re.reshape(C_GROUPS, C_STATE), ab_im.reshape(C_GROUPS, C_STATE),
            bb_re.reshape(C_GROUPS, C_STATE, C_GROUP), bb_im.reshape(C_GROUPS, C_STATE, C_GROUP))


def _s5_uproj_kernel(x_ref, w_ref, o_ref):
    o_ref[...] = _dot(x_ref[...].astype(BF16), w_ref[...])


def _s5_uproj(h2, w, tm=512):
    n, d = h2.shape
    return pl.pallas_call(
        _s5_uproj_kernel,
        out_shape=jax.ShapeDtypeStruct((n, d), F32),
        grid=(n // tm,),
        in_specs=[pl.BlockSpec((tm, d), lambda i: (i, 0)), pl.BlockSpec(w.shape, lambda i: (0, 0))],
        out_specs=pl.BlockSpec((tm, d), lambda i: (i, 0)),
        compiler_params=_params(("arbitrary",)),
        name="s5_in_proj",
    )(h2, w)


def _gelu_tanh(x):
    return 0.5 * x * (1.0 + jnp.tanh(math.sqrt(2.0 / math.pi) * (x + 0.044715 * (x * x * x))))


def _s5_scan_kernel(u_ref, bbre_ref, bbim_ref, are_ref, aim_ref, cm_ref, d_ref, g_ref,
                    st_ref, bure_ref, buim_ref, s_ref, *, bsz, tl, n_cb, sub):
    @pl.when(pl.program_id(0) == 0)
    def _():
        st_ref[...] = jnp.zeros(st_ref.shape, F32)

    cw = D_MODEL // n_cb
    sw = bure_ref.shape[1]
    for cb in range(n_cb):
        ch = slice(cb * cw, (cb + 1) * cw)
        u = u_ref[:, ch]
        ub = u.astype(BF16)
        bure_ref[...] = _dot(ub, bbre_ref[cb])
        buim_ref[...] = _dot(ub, bbim_ref[cb])
        for s0 in range(0, sw, sub):
            cs = slice(s0, s0 + sub)
            a_re = jnp.broadcast_to(are_ref[cb, :, cs], (bsz, sub))
            a_im = jnp.broadcast_to(aim_ref[cb, :, cs], (bsz, sub))

            def step(t, carry, cs=cs, a_re=a_re, a_im=a_im):
                s_re, s_im = carry
                r0 = pl.multiple_of(t * bsz, bsz)
                n_re = a_re * s_re - a_im * s_im + bure_ref[pl.ds(r0, bsz), cs]
                n_im = a_re * s_im + a_im * s_re + buim_ref[pl.ds(r0, bsz), cs]
                s_ref[pl.ds(r0, bsz), cs] = n_re.astype(BF16)
                s_ref[pl.ds(r0, bsz), sw + s0:sw + s0 + sub] = n_im.astype(BF16)
                return n_re, n_im

            s_re, s_im = lax.fori_loop(0, tl, step, (st_ref[cb, 0, :, cs], st_ref[cb, 1, :, cs]),
                                       unroll=2)
            st_ref[cb, 0, :, cs] = s_re
            st_ref[cb, 1, :, cs] = s_im
        y = _dot(s_ref[...], cm_ref[cb]) + d_ref[:, ch] * u
        g_ref[:, ch] = _gelu_tanh(y).astype(g_ref.dtype)


def _s5_scan(u_lb, bbre, bbim, a_re, a_im, cmat, d_skip, bsz, seq, tl=32, sub=512):
    n, d = u_lb.shape
    n_cb, cw, sw = bbre.shape
    rows = tl * bsz
    full = lambda a: pl.BlockSpec(a.shape, lambda t: (0,) * a.ndim)
    return pl.pallas_call(
        functools.partial(_s5_scan_kernel, bsz=bsz, tl=tl, n_cb=n_cb, sub=sub),
        out_shape=jax.ShapeDtypeStruct((n, d), BF16),
        grid=(seq // tl,),
        in_specs=[pl.BlockSpec((rows, d), lambda t: (t, 0)), full(bbre), full(bbim), full(a_re),
                  full(a_im), full(cmat), full(d_skip)],
        out_specs=pl.BlockSpec((rows, d), lambda t: (t, 0)),
        scratch_shapes=[pltpu.VMEM((n_cb, 2, bsz, sw), F32), pltpu.VMEM((rows, sw), F32),
                        pltpu.VMEM((rows, sw), F32), pltpu.VMEM((rows, 2 * sw), BF16)],
        compiler_params=_params(("arbitrary",)),
        name="s5_scan",
    )(u_lb, bbre, bbim, a_re, a_im, cmat, d_skip)


def _s5_glu_kernel(g_ref, x_ref, w1_ref, w2_ref, wo_ref, lg_ref, lb_ref, out_ref):
    g = g_ref[...]
    z = _dot(g, w1_ref[...]) * _sigmoid(_dot(g, w2_ref[...]))
    m = _dot(z.astype(BF16), wo_ref[...])
    out_ref[...] = _layer_norm(DN_ALPHA * x_ref[...] + m, lg_ref[...], lb_ref[...])


def _s5_glu(g, h2, w1, w2, wo, lg, lb, tm=512):
    n, d = h2.shape
    row = pl.BlockSpec((tm, d), lambda i: (i, 0))
    full = lambda a: pl.BlockSpec(a.shape, lambda i: (0, 0))
    return pl.pallas_call(
        _s5_glu_kernel,
        out_shape=jax.ShapeDtypeStruct((n, d), F32),
        grid=(n // tm,),
        in_specs=[row, row, full(w1), full(w2), full(wo), full(lg), full(lb)],
        out_specs=row,
        compiler_params=_params(("arbitrary",)),
        name="s5_glu_out_ln",
    )(g, h2, w1, w2, wo, lg, lb)


def _pad_heads(w, heads, dim):
    d = w.shape[0]
    w = w.reshape(d, heads, dim)
    return jnp.pad(w, ((0, 0), (0, 0), (0, LANE - dim))).reshape(d, heads * LANE)


def _even_layer_weights(w_in, gate_w2, gate_b):
    splits = (A_HEADS * A_HEAD_DIM, A_KV_HEADS * A_HEAD_DIM, A_KV_HEADS * A_HEAD_DIM,
              IDX_HEADS * IDX_DIM, IDX_DIM, IDX_HEADS,
              B_HEADS * B_KEY_DIM, B_HEADS * B_KEY_DIM, B_HEADS * B_VAL_DIM, GATE_RANK,
              B_HEADS * B_VAL_DIM)
    offs = np.cumsum((0,) + splits)
    aq, ak, av, iq, ik, iw, bq, bk, bv, bg, br = [w_in[:, offs[k]:offs[k + 1]] for k in range(11)]
    d = w_in.shape[0]
    slab = jnp.concatenate(
        [ik, iw, bg, jnp.zeros((d, LANE - IDX_DIM - IDX_HEADS - GATE_RANK), w_in.dtype)], axis=1)
    w_a = jnp.concatenate([_pad_heads(aq, A_HEADS, A_HEAD_DIM),
                           _pad_heads(ak, A_KV_HEADS, A_HEAD_DIM), av], axis=1)
    w_f = jnp.concatenate([_pad_heads(iq, IDX_HEADS, IDX_DIM), _pad_heads(bq, B_HEADS, B_KEY_DIM),
                           _pad_heads(bk, B_HEADS, B_KEY_DIM), bv, br, slab], axis=1)
    w_all = jnp.concatenate([w_a, w_f], axis=1).astype(BF16)
    g0 = IDX_DIM + IDX_HEADS
    w2p = jnp.zeros((LANE, B_HEADS * LANE), F32).at[g0:g0 + GATE_RANK].set(
        _pad_heads(gate_w2, B_HEADS, B_KEY_DIM)).astype(BF16)
    gbp = _pad_heads(gate_b[None, :], B_HEADS, B_KEY_DIM)
    return w_all, (w_a.shape[1], w_f.shape[1]), w2p, gbp


def _even_mixer_ln(x2, bsz, seq, w_in, bias_tiles, gate_w2, gate_b, norm_g, w_out, ln_g, ln_b):
    w_all, widths, w2p, gbp = _even_layer_weights(w_in, gate_w2, gate_b)
    h_a, h_f = _in_proj(x2, w_all, widths, (BF16, F32))
    o_a = _dsa(h_a, h_f, bias_tiles, bsz, seq)
    o_b = _gla(h_f, w2p, gbp, norm_g[None, :], bsz, seq)
    return _outproj_ln(o_a, o_b, x2, w_out.astype(BF16), ln_g[None, :], ln_b[None, :])


def _block_diag(m, per):
    g, r, c = m.shape
    eye = jnp.eye(per, dtype=m.dtype)
    m = m.reshape(g // per, per, r, c)
    bd = m[:, :, :, None, :] * eye[None, :, None, :, None]
    return bd.reshape(g // per, per * r, per * c)


def _s5_mixer_ln(h2, bsz, seq, w_in, lam_re, lam_im, log_dt, b_re, b_im, c_re, c_im, d_skip,
                 glu_w1, glu_w2, w_out, ln_g, ln_b, groups_per_block=16):
    ab_re, ab_im, bb_re, bb_im = _s5_prep(lam_re, lam_im, log_dt, b_re, b_im)
    per = groups_per_block
    n_cb = C_GROUPS // per
    bbre = _block_diag(jnp.swapaxes(bb_re, 1, 2), per).astype(BF16)
    bbim = _block_diag(jnp.swapaxes(bb_im, 1, 2), per).astype(BF16)
    a_re = ab_re.reshape(n_cb, 1, per * C_STATE)
    a_im = ab_im.reshape(n_cb, 1, per * C_STATE)
    cre = _block_diag(jnp.swapaxes(c_re, 1, 2), per)
    cim = _block_diag(jnp.swapaxes(c_im, 1, 2), per)
    cmat = jnp.concatenate([cre, -cim], axis=1).astype(BF16)
    u = _s5_uproj(h2, w_in.astype(BF16))
    g = _s5_scan(u, bbre, bbim, a_re, a_im, cmat, d_skip[None, :], bsz, seq)
    return _s5_glu(g, h2, glu_w1.astype(BF16), glu_w2.astype(BF16), w_out.astype(BF16),
                   ln_g[None, :], ln_b[None, :])


def _moe_ln(h, r_coarse, rb_coarse, r_fine, rb_fine, w_gate, w_up, w_down, ln_g, ln_b,
            regroup=None):
    d = h.shape[1]
    rows = MOE_GROUPS + N_EXPERTS
    wr = jnp.concatenate([r_coarse.T, jnp.transpose(r_fine, (0, 2, 1)).reshape(N_EXPERTS, d),
                          jnp.zeros((32 - rows, d), F32)], axis=0)
    rb = jnp.concatenate([rb_coarse, rb_fine.reshape(N_EXPERTS), jnp.zeros((32 - rows,), F32)])
    return _moe(h, wr, rb[:, None], w_gate.astype(BF16), w_up.astype(BF16), w_down.astype(BF16),
                ln_g[None, :], ln_b[None, :], regroup)


def kernel(x, rel_bias, ab_w_in, gla_gate_w2, gla_gate_b, gla_norm_g, ab_w_out, s5_w_in, s5_lam_re, s5_lam_im, s5_log_dt, s5_b_re, s5_b_im, s5_c_re, s5_c_im, s5_d, s5_glu_w1, s5_glu_w2, s5_w_out, ln_mix_g, ln_mix_b, ln_ffn_g, ln_ffn_b, moe_r_coarse, moe_rb_coarse, moe_r_fine, moe_rb_fine, moe_w_gate, moe_w_up, moe_w_down):
    bsz, seq, d = x.shape
    h = x.reshape(bsz * seq, d)
    bias_tiles = _bias_tiles(rel_bias)
    time_major = False
    for layer in range(DEPTH):
        i = layer // 2
        assert time_major == (layer % 2 == 1)
        if layer % 2 == 0:
            h = _even_mixer_ln(h, bsz, seq, ab_w_in[i], bias_tiles, gla_gate_w2[i], gla_gate_b[i],
                               gla_norm_g[i], ab_w_out[i], ln_mix_g[layer], ln_mix_b[layer])
        else:
            h = _s5_mixer_ln(h, bsz, seq, s5_w_in[i], s5_lam_re[i], s5_lam_im[i], s5_log_dt[i],
                             s5_b_re[i], s5_b_im[i], s5_c_re[i], s5_c_im[i], s5_d[i],
                             s5_glu_w1[i], s5_glu_w2[i], s5_w_out[i],
                             ln_mix_g[layer], ln_mix_b[layer])
        want_time_major = layer + 1 < DEPTH and (layer + 1) % 2 == 1
        regroup = None
        if want_time_major != time_major:
            regroup = (seq, bsz) if time_major else (bsz, seq)
        h = _moe_ln(h, moe_r_coarse[layer], moe_rb_coarse[layer], moe_r_fine[layer],
                    moe_rb_fine[layer], moe_w_gate[layer], moe_w_up[layer], moe_w_down[layer],
                    ln_ffn_g[layer], ln_ffn_b[layer], regroup)
        time_major = want_time_major
    return h.reshape(bsz, seq, d).astype(x.dtype)
```

```python
import functools
import math

import numpy as np
import jax
import jax.numpy as jnp
from jax import lax
from jax.experimental import pallas as pl
from jax.experimental.pallas import tpu as pltpu

F32 = jnp.float32
BF16 = jnp.bfloat16
I32 = jnp.int32

D_MODEL = 1024
DEPTH = 2
DN_ALPHA = (2.0 * DEPTH) ** 0.25
LN_EPS = 1e-5
A_HEAD_DIM = 64
A_HEADS = 8
A_KV_HEADS = 2
IDX_HEADS = 4
IDX_DIM = 64
TOPK_MAX = 256
REL_BUCKETS = 32
REL_MAX_DIST = 128
B_HEADS = 4
B_VAL_DIM = 128
B_KEY_DIM = 64
GATE_RANK = 16
GATE_TAU = 16.0
GLA_CHUNK = 64
C_GROUP = 16
C_GROUPS = 64
C_STATE = 64
MOE_GROUPS = 4
MOE_EXPERTS = 4
MOE_HIDDEN = 512
N_EXPERTS = MOE_GROUPS * MOE_EXPERTS

LANE = 128
VMEM_LIMIT = 52 * 1024 * 1024
NEG_INF = float("-inf")
INT_MIN = -(2 ** 31)

_NT = (((1,), (1,)), ((), ()))
_TN = (((0,), (0,)), ((), ()))


def _dot(a, b):
    return jnp.dot(a, b, preferred_element_type=F32)


def _dot_nt(a, b):
    return lax.dot_general(a, b, _NT, preferred_element_type=F32)


def _dot_tn(a, b):
    return lax.dot_general(a, b, _TN, preferred_element_type=F32)


def _split3(x):
    hi = x.astype(BF16)
    r1 = x - hi.astype(F32)
    mid = r1.astype(BF16)
    lo = (r1 - mid.astype(F32)).astype(BF16)
    return hi, mid, lo


def _params(sem):
    return pltpu.CompilerParams(dimension_semantics=sem, vmem_limit_bytes=VMEM_LIMIT)


def _layer_norm(y, g, b):
    mu = jnp.mean(y, axis=-1, keepdims=True)
    d = y - mu
    var = jnp.mean(d * d, axis=-1, keepdims=True)
    return d * lax.rsqrt(var + LN_EPS) * g + b


def _sigmoid(x):
    return 1.0 / (1.0 + jnp.exp(-x))


def _bucket_thresholds():
    max_exact = REL_BUCKETS // 2
    nf = np.arange(max_exact, 4 * REL_MAX_DIST).astype(np.float32)
    large = max_exact + (np.log(nf / np.float32(max_exact))
                         / np.float32(math.log(REL_MAX_DIST / max_exact))
                         * np.float32(REL_BUCKETS - max_exact)).astype(np.int32)
    large = np.minimum(large, REL_BUCKETS - 1)
    return [int(nf[np.argmax(large >= max_exact + j)]) for j in range(1, REL_BUCKETS - max_exact)]


def _bias_kernel(rb_ref, out_ref):
    max_exact = REL_BUCKETS // 2
    srow = lax.broadcasted_iota(I32, (LANE, LANE), 0)
    tcol = lax.broadcasted_iota(I32, (LANE, LANE), 1)
    thr = _bucket_thresholds()
    for band in range(2):
        n = jnp.maximum(band * LANE + tcol - srow, 0)
        large = jnp.full((LANE, LANE), max_exact, I32)
        for t in thr:
            large = large + jnp.where(n >= t, 1, 0)
        bucket = jnp.where(n < max_exact, n, large)
        for h in range(A_HEADS):
            acc = jnp.zeros((LANE, LANE), F32)
            for bk in range(REL_BUCKETS):
                acc = acc + jnp.where(bucket == bk, rb_ref[bk, h], 0.0)
            out_ref[h, band] = acc
    for h in range(A_HEADS):
        out_ref[h, 2] = jnp.full((LANE, LANE), rb_ref[REL_BUCKETS - 1, h], F32)


def _bias_tiles(rel_bias):
    return pl.pallas_call(
        _bias_kernel,
        out_shape=jax.ShapeDtypeStruct((A_HEADS, 3, LANE, LANE), F32),
        in_specs=[pl.BlockSpec(memory_space=pltpu.SMEM)],
        name="rel_bias_tiles",
    )(rel_bias)


def _proj_kernel(x_ref, w_ref, *out_refs, widths, chunk):
    xb = x_ref[...].astype(BF16)
    c0 = 0
    for o_ref, width in zip(out_refs, widths):
        for c in range(0, width, chunk):
            ce = min(c + chunk, width)
            o_ref[:, c:ce] = _dot(xb, w_ref[:, c0 + c:c0 + ce]).astype(o_ref.dtype)
        c0 += width


def _in_proj(x2, w, widths, dtypes, tm=512, chunk=256):
    n, d = x2.shape
    return pl.pallas_call(
        functools.partial(_proj_kernel, widths=widths, chunk=chunk),
        out_shape=[jax.ShapeDtypeStruct((n, wd), dt) for wd, dt in zip(widths, dtypes)],
        grid=(n // tm,),
        in_specs=[pl.BlockSpec((tm, d), lambda i: (i, 0)),
                  pl.BlockSpec(w.shape, lambda i: (0, 0))],
        out_specs=[pl.BlockSpec((tm, wd), lambda i: (i, 0)) for wd in widths],
        compiler_params=_params(("arbitrary",)),
        name="in_proj",
    )(x2, w)


def _sortable(x):
    bits = lax.bitcast_convert_type(x, I32)
    return jnp.where(bits < 0, bits ^ jnp.int32(0x7FFFFFFF), bits)


def _dsa_kernel(q_ref, k_ref, v_ref, iq_ref, slab_ref, slabq_ref, bias_ref, o_ref,
                key_ref, am_ref, vt_ref, acc_ref, lg_ref, tmax_ref, ot_ref, *, k_sel, seq):
    i = pl.program_id(1)
    t0 = i * LANE
    nkt = i + 1
    kt2 = 2 * LANE
    n2 = lax.shift_right_logical(nkt + 1, 1)
    rep = A_HEADS // A_KV_HEADS
    srow_2 = lax.broadcasted_iota(I32, (kt2, LANE), 0)
    tcol = t0 + lax.broadcasted_iota(I32, (kt2, LANE), 1)

    @pl.when(i == 0)
    def _():
        for jt in range(seq // kt2):
            vt_ref[jt] = v_ref[jt * kt2:(jt + 1) * kt2, :].astype(F32).T.astype(BF16)

    slab_t = slabq_ref[...].T
    w_t = slab_t[IDX_DIM:IDX_DIM + IDX_HEADS, :] * (IDX_HEADS ** -0.5)
    iqb = iq_ref[...].astype(BF16)
    iq_stack = jnp.concatenate([iqb[:, h * LANE:(h + 1) * LANE] for h in range(IDX_HEADS)], axis=0)

    def score_tile(j, carry):
        s0 = pl.multiple_of(j * kt2, kt2)
        ikt = slab_ref[pl.ds(s0, kt2), :].astype(BF16)
        d = _dot_nt(ikt, iq_stack) * (IDX_DIM ** -0.5)
        acc = jnp.zeros((kt2, LANE), F32)
        for h in range(IDX_HEADS):
            acc = acc + jnp.maximum(d[:, h * LANE:(h + 1) * LANE], 0.0) * w_t[h:h + 1, :]
        acc = jnp.where(s0 + srow_2 <= tcol, acc, NEG_INF)
        key_ref[pl.ds(s0, kt2), :] = _sortable(acc)
        return carry

    lax.fori_loop(0, n2, score_tile, 0)
    key_ref[pl.ds(pl.multiple_of(nkt * LANE, LANE), LANE), :] = jnp.full((LANE, LANE), INT_MIN, I32)

    def count(pred_fn):
        def body(j, c):
            s0 = pl.multiple_of(j * kt2, kt2)
            kt = key_ref[pl.ds(s0, kt2), :]
            m = jnp.where(pred_fn(kt, s0 + srow_2), 1, 0)
            return c + jnp.sum(m.reshape(kt2 // 8, 8, LANE), axis=0)
        c = lax.fori_loop(0, n2, body, jnp.zeros((8, LANE), I32))
        return jnp.sum(c, axis=0, keepdims=True)

    def bcast(v):
        return jnp.broadcast_to(v, (kt2, LANE))

    def search():
        c0 = count(lambda kt, s: kt >= 0)
        ans0 = jnp.where(c0 >= k_sel, 0, INT_MIN).astype(I32)

        def bit_body(bi, ans):
            cand = ans | lax.shift_left(jnp.int32(1), 30 - bi)
            cb = bcast(cand)
            cnt = count(lambda kt, s: kt >= cb)
            return jnp.where(cnt >= k_sel, cand, ans)

        ans = lax.fori_loop(0, 31, bit_body, ans0)
        ab = bcast(ans)
        cnt_ge = count(lambda kt, s: kt >= ab)
        cnt_gt = count(lambda kt, s: kt > ab)
        need = k_sel - cnt_gt

        def tie_search():
            def idx_body(bi, x):
                cand = x | lax.shift_left(jnp.int32(1), (seq.bit_length() - 2) - bi)
                cb = bcast(cand)
                f = count(lambda kt, s: (kt == ab) & (s < cb))
                return jnp.where(f < need, cand, x)
            return lax.fori_loop(0, seq.bit_length() - 1, idx_body, jnp.zeros((1, LANE), I32))

        cut = lax.cond(jnp.max(cnt_ge) > k_sel, tie_search,
                       lambda: jnp.full((1, LANE), seq - 1, I32))
        return ans, cut

    ans, cut = lax.cond(t0 >= k_sel, search,
                        lambda: (jnp.full((1, LANE), INT_MIN, I32),
                                 jnp.full((1, LANE), seq - 1, I32)))
    ans_b = bcast(ans)
    cut_b = bcast(cut)

    def mask_tile(j, carry):
        s0 = pl.multiple_of(j * kt2, kt2)
        kt = key_ref[pl.ds(s0, kt2), :]
        srow = s0 + srow_2
        sel = (kt > ans_b) | ((kt == ans_b) & (srow <= cut_b))
        am_ref[pl.ds(s0, kt2), :] = jnp.where(sel & (srow <= tcol), 0.0, NEG_INF)
        return carry

    lax.fori_loop(0, n2, mask_tile, 0)

    scale = jnp.asarray(A_HEAD_DIM ** -0.5, BF16)
    q_stacks = [jnp.concatenate([q_ref[:, (g * rep + r) * LANE:(g * rep + r + 1) * LANE]
                                 for r in range(rep)], axis=0) * scale for g in range(A_KV_HEADS)]
    acc_ref[...] = jnp.zeros(acc_ref.shape, F32)

    def logits_stage(j):
        slot = j & 1
        s0 = pl.multiple_of(j * kt2, kt2)
        am = am_ref[pl.ds(s0, kt2), :]
        band0 = jnp.clip(i - 2 * j, 0, 2)
        band1 = jnp.clip(i - 2 * j - 1, 0, 2)
        for g in range(A_KV_HEADS):
            kt = k_ref[pl.ds(s0, kt2), g * LANE:(g + 1) * LANE]
            lg_all = _dot_nt(kt, q_stacks[g])
            for r in range(rep):
                h = g * rep + r
                cs = slice(r * LANE, (r + 1) * LANE)
                bias = jnp.concatenate([bias_ref[h, band0], bias_ref[h, band1]], axis=0)
                lg = lg_all[:, cs] + bias + am
                lg_ref[slot, g, :, cs] = lg
                tmax_ref[slot, h:h + 1, :] = jnp.max(lg, axis=0, keepdims=True)

    def softmax_stage(j, carry):
        ms, ls = carry
        slot = j & 1
        new_ms, new_ls = [], []
        for g in range(A_KV_HEADS):
            ps, alphas = [], []
            for r in range(rep):
                h = g * rep + r
                m_new = jnp.maximum(ms[h], tmax_ref[slot, h:h + 1, :])
                m_safe = jnp.where(m_new == NEG_INF, 0.0, m_new)
                alpha = jnp.exp(ms[h] - m_safe)
                p = jnp.exp(lg_ref[slot, g, :, r * LANE:(r + 1) * LANE] - m_safe)
                new_ms.append(m_new)
                new_ls.append(alpha * ls[h] + jnp.sum(p, axis=0, keepdims=True))
                ps.append(p.astype(BF16))
                alphas.append(alpha)
            vt = vt_ref[j, g * A_HEAD_DIM:(g + 1) * A_HEAD_DIM, :]
            pv = _dot(vt, jnp.concatenate(ps, axis=1))
            acc_ref[g] = acc_ref[g] * jnp.concatenate(alphas, axis=1) + pv
        return tuple(new_ms), tuple(new_ls)

    def tile_body(j, carry):
        carry = softmax_stage(j, carry)
        logits_stage(j + 1)
        return carry

    init = (tuple(jnp.full((1, LANE), NEG_INF, F32) for _ in range(A_HEADS)),
            tuple(jnp.zeros((1, LANE), F32) for _ in range(A_HEADS)))
    logits_stage(0)
    carry = lax.fori_loop(0, n2 - 1, tile_body, init)
    _, ls = softmax_stage(n2 - 1, carry)
    for g in range(A_KV_HEADS):
        for r in range(rep):
            h = g * rep + r
            ot_ref[h * A_HEAD_DIM:(h + 1) * A_HEAD_DIM, :] = (
                acc_ref[g, :, r * LANE:(r + 1) * LANE] / ls[h])
    o_ref[...] = ot_ref[...].T.astype(o_ref.dtype)


def _dsa(h_a, h_f, bias_tiles, bsz, seq):
    n = bsz * seq
    nq = seq // LANE
    k_sel = min(TOPK_MAX, seq // 4)
    assert k_sel % LANE == 0 and seq % (2 * LANE) == 0
    qw = A_HEADS * LANE
    slab_blk = (h_f.shape[1] - LANE) // LANE
    return pl.pallas_call(
        functools.partial(_dsa_kernel, k_sel=k_sel, seq=seq),
        out_shape=jax.ShapeDtypeStruct((n, A_HEADS * A_HEAD_DIM), BF16),
        grid=(bsz, nq),
        in_specs=[
            pl.BlockSpec((LANE, qw), lambda b, i: (b * nq + i, 0)),
            pl.BlockSpec((seq, A_KV_HEADS * LANE), lambda b, i: (b, qw // (A_KV_HEADS * LANE))),
            pl.BlockSpec((seq, LANE), lambda b, i: (b, (qw + A_KV_HEADS * LANE) // LANE)),
            pl.BlockSpec((LANE, IDX_HEADS * LANE), lambda b, i: (b * nq + i, 0)),
            pl.BlockSpec((seq, LANE), lambda b, i: (b, slab_blk)),
            pl.BlockSpec((LANE, LANE), lambda b, i: (b * nq + i, slab_blk)),
            pl.BlockSpec((A_HEADS, 3, LANE, LANE), lambda b, i: (0, 0, 0, 0)),
        ],
        out_specs=pl.BlockSpec((LANE, A_HEADS * A_HEAD_DIM), lambda b, i: (b * nq + i, 0)),
        scratch_shapes=[
            pltpu.VMEM((seq + LANE, LANE), I32),
            pltpu.VMEM((seq, LANE), F32),
            pltpu.VMEM((nq // 2, LANE, 2 * LANE), BF16),
            pltpu.VMEM((A_KV_HEADS, A_HEAD_DIM, (A_HEADS // A_KV_HEADS) * LANE), F32),
            pltpu.VMEM((2, A_KV_HEADS, 2 * LANE, (A_HEADS // A_KV_HEADS) * LANE), F32),
            pltpu.VMEM((2, A_HEADS, LANE), F32),
            pltpu.VMEM((A_HEADS * A_HEAD_DIM, LANE), F32),
        ],
        compiler_params=_params(("arbitrary", "arbitrary")),
        name="dsa_attention",
    )(h_a, h_a, h_a, h_f, h_f, h_f, bias_tiles)


def _gla_kernel(bq_ref, bk_ref, bv_ref, br_ref, slab_ref, w2_ref, gb_ref, ng_ref, o_ref,
                st_ref, *, n_chunks):
    @pl.when(pl.program_id(1) == 0)
    def _():
        st_ref[...] = jnp.zeros(st_ref.shape, F32)

    ch = GLA_CHUNK
    rws = n_chunks * ch
    shift = ch.bit_length() - 1
    assert 1 << shift == ch
    row = lax.broadcasted_iota(I32, (rws, rws), 0)
    col = lax.broadcasted_iota(I32, (rws, rws), 1)
    tril = (lax.shift_right_logical(row, shift) == lax.shift_right_logical(col, shift)) & (row >= col)
    tri = jnp.where(tril, 1.0, 0.0).astype(BF16)
    gate = _dot(slab_ref[...].astype(BF16), w2_ref[...]) + gb_ref[...]
    log_a = (jnp.minimum(gate, 0.0) - jnp.log1p(jnp.exp(-jnp.abs(gate)))) / GATE_TAU
    hi, mid, lo = _split3(log_a)
    cum = _dot(tri, hi) + _dot(tri, mid) + _dot(tri, lo)
    lasts = [cum[(c + 1) * ch - 1:(c + 1) * ch, :] for c in range(n_chunks)]
    last = jnp.concatenate([jnp.broadcast_to(l, (ch, l.shape[1])) for l in lasts], axis=0)
    q = bq_ref[...] * (B_KEY_DIM ** -0.5)
    k = bk_ref[...]
    q_dec = (q * jnp.exp(cum)).astype(BF16)
    k_inv = (k * jnp.exp(-cum)).astype(BF16)
    k_end = (k * jnp.exp(last - cum)).astype(BF16)
    decays = [jnp.exp(l) for l in lasts]
    for h in range(B_HEADS):
        cs = slice(h * LANE, (h + 1) * LANE)
        v = bv_ref[:, cs].astype(BF16)
        sc = jnp.where(tril, _dot_nt(q_dec[:, cs], k_inv[:, cs]), 0.0)
        o_intra = _dot(sc.astype(BF16), v)
        st = st_ref[h]
        parts = []
        for c in range(n_chunks):
            rows = slice(c * ch, (c + 1) * ch)
            parts.append(o_intra[rows] + _dot_nt(q_dec[rows, cs], st.astype(BF16)))
            st = decays[c][:, cs] * st + _dot_tn(v[rows], k_end[rows, cs])
        st_ref[h] = st
        o = jnp.concatenate(parts, axis=0)
        mu = jnp.mean(o, axis=-1, keepdims=True)
        dlt = o - mu
        var = jnp.mean(dlt * dlt, axis=-1, keepdims=True)
        on = dlt * lax.rsqrt(var + LN_EPS) * ng_ref[:, cs]
        r = br_ref[:, cs]
        o_ref[:, cs] = (on * (r * _sigmoid(r))).astype(o_ref.dtype)


def _gla(h_f, w2p, gbp, norm_g, bsz, seq, rows_per_step=256):
    n = bsz * seq
    hw = B_HEADS * LANE
    steps = seq // rows_per_step
    slab_blk = (h_f.shape[1] - LANE) // LANE
    blk = lambda cb: pl.BlockSpec((rows_per_step, hw), lambda b, s, cb=cb: (b * steps + s, cb))
    return pl.pallas_call(
        functools.partial(_gla_kernel, n_chunks=rows_per_step // GLA_CHUNK),
        out_shape=jax.ShapeDtypeStruct((n, hw), BF16),
        grid=(bsz, steps),
        in_specs=[blk(1), blk(2), blk(3), blk(4),
                  pl.BlockSpec((rows_per_step, LANE), lambda b, s: (b * steps + s, slab_blk)),
                  pl.BlockSpec((LANE, hw), lambda b, s: (0, 0)),
                  pl.BlockSpec((1, hw), lambda b, s: (0, 0)),
                  pl.BlockSpec((1, hw), lambda b, s: (0, 0))],
        out_specs=pl.BlockSpec((rows_per_step, hw), lambda b, s: (b * steps + s, 0)),
        scratch_shapes=[pltpu.VMEM((B_HEADS, LANE, LANE), F32)],
        compiler_params=_params(("arbitrary", "arbitrary")),
        name="gla_attention",
    )(h_f, h_f, h_f, h_f, h_f, w2p, gbp, norm_g)


def _outproj_ln_kernel(oa_ref, ob_ref, x_ref, w_ref, g_ref, b_ref, out_ref):
    ka = oa_ref.shape[1]
    m = _dot(oa_ref[...], w_ref[0:ka, :]) + _dot(ob_ref[...], w_ref[ka:, :])
    out_ref[...] = _layer_norm(DN_ALPHA * x_ref[...] + m, g_ref[...], b_ref[...])


def _outproj_ln(o_a, o_b, x2, w, g, b, tm=512):
    n, d = x2.shape
    row = lambda width: pl.BlockSpec((tm, width), lambda i: (i, 0))
    full = lambda a: pl.BlockSpec(a.shape, lambda i: (0, 0))
    return pl.pallas_call(
        _outproj_ln_kernel,
        out_shape=jax.ShapeDtypeStruct((n, d), F32),
        grid=(n // tm,),
        in_specs=[row(o_a.shape[1]), row(o_b.shape[1]), row(d), full(w), full(g), full(b)],
        out_specs=row(d),
        compiler_params=_params(("arbitrary",)),
        name="out_proj_ln",
    )(o_a, o_b, x2, w, g, b)


def _route(lt):
    gl = [lt[g:g + 1, :] for g in range(MOE_GROUPS)]
    best, gsel = gl[0], jnp.zeros_like(gl[0], dtype=I32)
    for g in range(1, MOE_GROUPS):
        better = gl[g] > best
        gsel = jnp.where(better, g, gsel)
        best = jnp.where(better, gl[g], best)
    denom = sum(jnp.exp(x - best) for x in gl)
    g_w = 1.0 / denom
    fl = []
    for e in range(MOE_EXPERTS):
        acc = jnp.zeros_like(best)
        for g in range(MOE_GROUPS):
            r = MOE_GROUPS + g * MOE_EXPERTS + e
            acc = jnp.where(gsel == g, lt[r:r + 1, :], acc)
        fl.append(acc)
    v1, i1 = fl[0], jnp.zeros_like(gsel)
    for e in range(1, MOE_EXPERTS):
        better = fl[e] > v1
        i1 = jnp.where(better, e, i1)
        v1 = jnp.where(better, fl[e], v1)
    v2, i2 = jnp.full_like(v1, NEG_INF), jnp.zeros_like(gsel)
    for e in range(MOE_EXPERTS):
        better = (fl[e] > v2) & (i1 != e)
        i2 = jnp.where(better, e, i2)
        v2 = jnp.where(better, fl[e], v2)
    e2 = jnp.exp(v2 - v1)
    w1 = (1.0 / (1.0 + e2)) * g_w
    w2 = (e2 / (1.0 + e2)) * g_w
    e_gate = [jnp.where(i1 == e, w1, 0.0) + jnp.where(i2 == e, w2, 0.0)
              for e in range(MOE_EXPERTS)]
    return gsel, e_gate


def _moe_route_kernel(x_ref, wr_ref, rb_ref, xa_ref, rt_ref, cnt_ref, tri_ref, carry_ref):
    i = pl.program_id(0)
    tm, d = x_ref.shape

    @pl.when(i == 0)
    def _():
        carry_ref[...] = jnp.zeros(carry_ref.shape, F32)
        before = (lax.broadcasted_iota(I32, (tm, tm), 0) < lax.broadcasted_iota(I32, (tm, tm), 1))
        tri_ref[...] = jnp.where(before, 1.0, 0.0).astype(BF16)

    x = x_ref[...]
    xs = _split3(x)
    ws = _split3(wr_ref[...])
    lt = jnp.zeros((wr_ref.shape[0], tm), F32)
    for a, b in ((2, 0), (0, 2), (1, 1), (1, 0), (0, 1), (0, 0)):
        lt = lt + _dot_nt(ws[a], xs[b])
    lt = lt + rb_ref[...]
    gsel, e_gate = _route(lt)

    row8 = lax.broadcasted_iota(I32, (8, tm), 0)
    onehot = jnp.where(row8 == gsel, 1.0, 0.0)
    earlier = _dot(onehot.astype(BF16), tri_ref[...])
    rank = jnp.sum(onehot * (earlier + carry_ref[:, 0:1]), axis=0, keepdims=True)
    rt_ref[...] = jnp.where(row8 == 0, gsel, jnp.where(row8 == 1, rank.astype(I32), 0))
    carry_ref[...] = carry_ref[...] + jnp.sum(onehot, axis=1, keepdims=True)
    cnt_ref[...] = carry_ref[...].astype(I32)

    erow = lax.broadcasted_iota(I32, (LANE, tm), 0)
    gt = jnp.zeros((LANE, tm), F32)
    for e in range(MOE_EXPERTS):
        gt = jnp.where(erow == e, e_gate[e], gt)
    xa_ref[:, :d] = x
    xa_ref[:, d:] = gt.T


def _moe_route(h, wr, rb, tm=1024):
    n, d = h.shape
    tm = min(tm, n)
    full = lambda a: pl.BlockSpec(a.shape, lambda i: (0,) * a.ndim)
    return pl.pallas_call(
        _moe_route_kernel,
        out_shape=[jax.ShapeDtypeStruct((n, d + LANE), F32), jax.ShapeDtypeStruct((8, n), I32),
                   jax.ShapeDtypeStruct((8, LANE), I32)],
        grid=(n // tm,),
        in_specs=[pl.BlockSpec((tm, d), lambda i: (i, 0)), full(wr), full(rb)],
        out_specs=[pl.BlockSpec((tm, d + LANE), lambda i: (i, 0)),
                   pl.BlockSpec((8, tm), lambda i: (0, i)),
                   pl.BlockSpec((8, LANE), lambda i: (0, 0))],
        scratch_shapes=[pltpu.VMEM((tm, tm), BF16), pltpu.VMEM((8, LANE), F32)],
        compiler_params=_params(("arbitrary",)),
        name="moe_route",
    )(h, wr, rb)


def _row_copy_waves(n_rows, wave, start_row, wait_wave):
    n_waves = n_rows // wave
    for w in range(n_waves + 1):
        if w < n_waves:
            lax.fori_loop(w * wave, (w + 1) * wave, lambda r, c, w=w: (start_row(r, w % 2), c)[1], 0,
                          unroll=8)
        if w >= 1:
            wait_wave((w - 1) % 2)


def _moe_dispatch_kernel(pos_ref, fill_ref, xa_ref, xs_hbm, zero_ref, sem, *, wave):
    i = pl.program_id(0)
    tm = pos_ref.shape[1]
    tme = zero_ref.shape[0]

    def copy(r, s):
        return pltpu.make_async_copy(xa_ref.at[r], xs_hbm.at[pos_ref[0, r]], sem.at[s])

    def wait_wave(s):
        rows = pl.ds(0, wave)
        pltpu.make_async_copy(xa_ref.at[rows], xs_hbm.at[rows], sem.at[s]).wait()

    _row_copy_waves(tm, wave, lambda r, s: copy(r, s).start(), wait_wave)

    @pl.when(i == pl.num_programs(0) - 1)
    def _():
        zero_ref[...] = jnp.zeros(zero_ref.shape, F32)
        for g in range(MOE_GROUPS):
            first, count = fill_ref[0, g], fill_ref[1, g]

            def zcopy(r, first=first):
                return pltpu.make_async_copy(zero_ref.at[0], xs_hbm.at[first + r], sem.at[0])

            lax.fori_loop(0, count, lambda r, c: (zcopy(r).start(), c)[1], 0)
            lax.fori_loop(0, count, lambda r, c: (zcopy(r).wait(), c)[1], 0)
        n_tiles = fill_ref[2, 0]
        for k in range(MOE_GROUPS):
            @pl.when(k < fill_ref[2, 1])
            def _(k=k):
                tail = xs_hbm.at[pl.ds(pl.multiple_of((n_tiles + k) * tme, tme), tme)]
                cp = pltpu.make_async_copy(zero_ref, tail, sem.at[0])
                cp.start()
                cp.wait()


def _moe_dispatch(xa, pos, fill, n_sorted, tme, tm=1024, wave=256):
    n, da = xa.shape
    tm = min(tm, n)
    wave = min(wave, tm)
    return pl.pallas_call(
        functools.partial(_moe_dispatch_kernel, wave=wave),
        out_shape=jax.ShapeDtypeStruct((n_sorted, da), F32),
        grid=(n // tm,),
        in_specs=[pl.BlockSpec((1, tm), lambda i: (0, i), memory_space=pltpu.SMEM),
                  pl.BlockSpec(memory_space=pltpu.SMEM),
                  pl.BlockSpec((tm, da), lambda i: (i, 0))],
        out_specs=pl.BlockSpec(memory_space=pl.ANY),
        scratch_shapes=[pltpu.VMEM((tme, da), F32), pltpu.SemaphoreType.DMA((2,))],
        compiler_params=_params(("arbitrary",)),
        name="moe_dispatch",
    )(pos, fill, xa)


def _moe_expert_kernel(tg_ref, tj_ref, nt_ref, xs_ref, wg_ref, wu_ref, wd_ref, lg_ref, lb_ref,
                       ys_ref):
    d = ys_ref.shape[1]

    @pl.when(pl.program_id(0) < nt_ref[0])
    def _():
        x = xs_ref[:, :d]
        gates = xs_ref[:, d:]
        xb = x.astype(BF16)
        y = jnp.zeros(x.shape, F32)
        for e in range(MOE_EXPERTS):
            hg = _dot(xb, wg_ref[e])
            hu = _dot(xb, wu_ref[e])
            hid = hg * _sigmoid(hg) * hu * gates[:, e:e + 1]
            y = y + _dot(hid.astype(BF16), wd_ref[e])
        ys_ref[...] = _layer_norm(DN_ALPHA * x + y, lg_ref[...], lb_ref[...])

    @pl.when(pl.program_id(0) >= nt_ref[0])
    def _():
        ys_ref[...] = jnp.zeros(ys_ref.shape, F32)


def _moe_experts(xs, tile_g, tile_j, n_tiles, wg, wu, wd, lg, lb, tme):
    p, da = xs.shape
    d = da - LANE
    grp = lambda w: pl.BlockSpec((None,) + w.shape[1:], lambda i, tg, tj, nt: (tg[i], 0, 0, 0))
    full = lambda a: pl.BlockSpec(a.shape, lambda i, tg, tj, nt: (0,) * a.ndim)
    grid_spec = pltpu.PrefetchScalarGridSpec(
        num_scalar_prefetch=3,
        grid=(p // tme,),
        in_specs=[pl.BlockSpec((tme, da), lambda i, tg, tj, nt: (tj[i], 0)),
                  grp(wg), grp(wu), grp(wd), full(lg), full(lb)],
        out_specs=pl.BlockSpec((tme, d), lambda i, tg, tj, nt: (i, 0)),
    )
    return pl.pallas_call(
        _moe_expert_kernel,
        out_shape=jax.ShapeDtypeStruct((p, d), F32),
        grid_spec=grid_spec,
        compiler_params=_params(("arbitrary",)),
        name="moe_experts",
    )(tile_g, tile_j, n_tiles, xs, wg, wu, wd, lg, lb)


def _moe_combine_kernel(pos_ref, ys_hbm, o_ref, sem, *, wave):
    tm = o_ref.shape[0]

    def copy(r, s):
        return pltpu.make_async_copy(ys_hbm.at[pos_ref[0, r]], o_ref.at[r], sem.at[s])

    def wait_wave(s):
        rows = pl.ds(0, wave)
        pltpu.make_async_copy(ys_hbm.at[rows], o_ref.at[rows], sem.at[s]).wait()

    _row_copy_waves(tm, wave, lambda r, s: copy(r, s).start(), wait_wave)


def _moe_combine(ys, pos, n, tm=1024, wave=256):
    d = ys.shape[1]
    tm = min(tm, n)
    wave = min(wave, tm)
    return pl.pallas_call(
        functools.partial(_moe_combine_kernel, wave=wave),
        out_shape=jax.ShapeDtypeStruct((n, d), F32),
        grid=(n // tm,),
        in_specs=[pl.BlockSpec((1, tm), lambda i: (0, i), memory_space=pltpu.SMEM),
                  pl.BlockSpec(memory_space=pl.ANY)],
        out_specs=pl.BlockSpec((tm, d), lambda i: (i, 0)),
        scratch_shapes=[pltpu.SemaphoreType.DMA((2,))],
        compiler_params=_params(("arbitrary",)),
        name="moe_combine",
    )(pos, ys)


def _moe(h, wr, rb, wg, wu, wd, lg, lb, regroup=None, tme=512):
    n, d = h.shape
    tme = min(tme, n)
    xa, rt, cnt = _moe_route(h, wr, rb)
    counts = cnt[:MOE_GROUPS, 0]
    tiles = (counts + tme - 1) // tme
    first_tile = jnp.cumsum(tiles) - tiles
    gid, rank = rt[0], rt[1]
    pos = (first_tile[gid] * tme + rank)[None, :]
    n_slots = n // tme + MOE_GROUPS
    n_tiles = jnp.sum(tiles)
    fill = jnp.stack([first_tile * tme + counts, tiles * tme - counts,
                      jnp.zeros((MOE_GROUPS,), I32).at[0].set(n_tiles).at[1].set(n_slots - n_tiles)])
    slot = jnp.minimum(jnp.arange(n_slots, dtype=I32), n_tiles - 1)
    tile_g = jnp.sum(slot[:, None] >= (first_tile + tiles)[None, :], axis=1).astype(I32)
    xs = _moe_dispatch(xa, pos, fill, n_slots * tme, tme)
    ys = _moe_experts(xs, tile_g, slot, n_tiles[None].astype(I32), wg, wu, wd, lg, lb, tme)
    if regroup is not None:
        pos = pos.reshape(regroup).T.reshape(1, n)
    return _moe_combine(ys, pos, n)


def _s5_prep_kernel(lre_ref, lim_ref, ldt_ref, bre_ref, bim_ref, are_ref, aim_ref, bbre_ref,
                    bbim_ref):
    lr = jnp.minimum(lre_ref[...], -1e-4)
    li = lim_ref[...]
    dt = jnp.exp(ldt_ref[...])
    mag = jnp.exp(lr * dt)
    ab_re = mag * jnp.cos(li * dt)
    ab_im = mag * jnp.sin(li * dt)
    den = lr * lr + li * li
    nr = ab_re - 1.0
    coef_re = (nr * lr + ab_im * li) / den
    coef_im = (ab_im * lr - nr * li) / den
    are_ref[...] = ab_re
    aim_ref[...] = ab_im
    bbre_ref[...] = coef_re * bre_ref[...] - coef_im * bim_ref[...]
    bbim_ref[...] = coef_re * bim_ref[...] + coef_im * bre_ref[...]


def _s5_prep(lam_re, lam_im, log_dt, b_re, b_im):
    gp = C_GROUPS * C_STATE
    col = lambda a: a.reshape(gp, 1)
    ldt = jnp.broadcast_to(log_dt[:, None], (C_GROUPS, C_STATE))
    outs = pl.pallas_call(
        _s5_prep_kernel,
        out_shape=[jax.ShapeDtypeStruct((gp, 1), F32), jax.ShapeDtypeStruct((gp, 1), F32),
                   jax.ShapeDtypeStruct((gp, C_GROUP), F32), jax.ShapeDtypeStruct((gp, C_GROUP), F32)],
        name="s5_discretise",
    )(col(lam_re), col(lam_im), col(ldt), b_re.reshape(gp, C_GROUP), b_im.reshape(gp, C_GROUP))
    ab_re, ab_im, bb_re, bb_im = outs
    return (ab_re.reshape(C_GROUPS, C_STATE), ab_im.reshape(C_GROUPS, C_STATE),
            bb_re.reshape(C_GROUPS, C_STATE, C_GROUP), bb_im.reshape(C_GROUPS, C_STATE, C_GROUP))


def _s5_uproj_kernel(x_ref, w_ref, o_ref):
    o_ref[...] = _dot(x_ref[...].astype(BF16), w_ref[...])


def _s5_uproj(h2, w, tm=512):
    n, d = h2.shape
    return pl.pallas_call(
        _s5_uproj_kernel,
        out_shape=jax.ShapeDtypeStruct((n, d), F32),
        grid=(n // tm,),
        in_specs=[pl.BlockSpec((tm, d), lambda i: (i, 0)), pl.BlockSpec(w.shape, lambda i: (0, 0))],
        out_specs=pl.BlockSpec((tm, d), lambda i: (i, 0)),
        compiler_params=_params(("arbitrary",)),
        name="s5_in_proj",
    )(h2, w)


def _gelu_tanh(x):
    return 0.5 * x * (1.0 + jnp.tanh(math.sqrt(2.0 / math.pi) * (x + 0.044715 * (x * x * x))))


def _s5_scan_kernel(u_ref, bbre_ref, bbim_ref, are_ref, aim_ref, cm_ref, d_ref, g_ref,
                    st_ref, bure_ref, buim_ref, s_ref, *, bsz, tl, n_cb, sub):
    @pl.when(pl.program_id(0) == 0)
    def _():
        st_ref[...] = jnp.zeros(st_ref.shape, F32)

    cw = D_MODEL // n_cb
    sw = bure_ref.shape[2]

    def input_states(cb):
        ub = u_ref[:, cb * cw:(cb + 1) * cw].astype(BF16)
        bure_ref[cb % 2] = _dot(ub, bbre_ref[cb])
        buim_ref[cb % 2] = _dot(ub, bbim_ref[cb])

    input_states(0)
    for cb in range(n_cb):
        buf = cb % 2
        if cb + 1 < n_cb:
            input_states(cb + 1)
        for s0 in range(0, sw, sub):
            cs = slice(s0, s0 + sub)
            a_re = jnp.broadcast_to(are_ref[cb, :, cs], (bsz, sub))
            a_im = jnp.broadcast_to(aim_ref[cb, :, cs], (bsz, sub))
            s_re, s_im = st_ref[cb, 0, :, cs], st_ref[cb, 1, :, cs]
            for t in range(tl):
                rows = slice(t * bsz, (t + 1) * bsz)
                s_re, s_im = (a_re * s_re - a_im * s_im + bure_ref[buf, rows, cs],
                              a_re * s_im + a_im * s_re + buim_ref[buf, rows, cs])
                s_ref[buf, rows, cs] = s_re.astype(BF16)
                s_ref[buf, rows, sw + s0:sw + s0 + sub] = s_im.astype(BF16)
            st_ref[cb, 0, :, cs] = s_re
            st_ref[cb, 1, :, cs] = s_im
        ch = slice(cb * cw, (cb + 1) * cw)
        y = _dot(s_ref[buf], cm_ref[cb]) + d_ref[:, ch] * u_ref[:, ch]
        g_ref[:, ch] = _gelu_tanh(y).astype(g_ref.dtype)


def _s5_scan(u_lb, bbre, bbim, a_re, a_im, cmat, d_skip, bsz, seq, tl=32, sub=512):
    n, d = u_lb.shape
    n_cb, cw, sw = bbre.shape
    rows = tl * bsz
    full = lambda a: pl.BlockSpec(a.shape, lambda t: (0,) * a.ndim)
    return pl.pallas_call(
        functools.partial(_s5_scan_kernel, bsz=bsz, tl=tl, n_cb=n_cb, sub=sub),
        out_shape=jax.ShapeDtypeStruct((n, d), BF16),
        grid=(seq // tl,),
        in_specs=[pl.BlockSpec((rows, d), lambda t: (t, 0)), full(bbre), full(bbim), full(a_re),
                  full(a_im), full(cmat), full(d_skip)],
        out_specs=pl.BlockSpec((rows, d), lambda t: (t, 0)),
        scratch_shapes=[pltpu.VMEM((n_cb, 2, bsz, sw), F32), pltpu.VMEM((2, rows, sw), F32),
                        pltpu.VMEM((2, rows, sw), F32), pltpu.VMEM((2, rows, 2 * sw), BF16)],
        compiler_params=_params(("arbitrary",)),
        name="s5_scan",
    )(u_lb, bbre, bbim, a_re, a_im, cmat, d_skip)


def _s5_glu_kernel(g_ref, x_ref, w1_ref, w2_ref, wo_ref, lg_ref, lb_ref, out_ref):
    g = g_ref[...]
    z = _dot(g, w1_ref[...]) * _sigmoid(_dot(g, w2_ref[...]))
    m = _dot(z.astype(BF16), wo_ref[...])
    out_ref[...] = _layer_norm(DN_ALPHA * x_ref[...] + m, lg_ref[...], lb_ref[...])


def _s5_glu(g, h2, w1, w2, wo, lg, lb, tm=512):
    n, d = h2.shape
    row = pl.BlockSpec((tm, d), lambda i: (i, 0))
    full = lambda a: pl.BlockSpec(a.shape, lambda i: (0, 0))
    return pl.pallas_call(
        _s5_glu_kernel,
        out_shape=jax.ShapeDtypeStruct((n, d), F32),
        grid=(n // tm,),
        in_specs=[row, row, full(w1), full(w2), full(wo), full(lg), full(lb)],
        out_specs=row,
        compiler_params=_params(("arbitrary",)),
        name="s5_glu_out_ln",
    )(g, h2, w1, w2, wo, lg, lb)


def _pad_heads(w, heads, dim):
    d = w.shape[0]
    w = w.reshape(d, heads, dim)
    return jnp.pad(w, ((0, 0), (0, 0), (0, LANE - dim))).reshape(d, heads * LANE)


def _even_layer_weights(w_in, gate_w2, gate_b):
    splits = (A_HEADS * A_HEAD_DIM, A_KV_HEADS * A_HEAD_DIM, A_KV_HEADS * A_HEAD_DIM,
              IDX_HEADS * IDX_DIM, IDX_DIM, IDX_HEADS,
              B_HEADS * B_KEY_DIM, B_HEADS * B_KEY_DIM, B_HEADS * B_VAL_DIM, GATE_RANK,
              B_HEADS * B_VAL_DIM)
    offs = np.cumsum((0,) + splits)
    aq, ak, av, iq, ik, iw, bq, bk, bv, bg, br = [w_in[:, offs[k]:offs[k + 1]] for k in range(11)]
    d = w_in.shape[0]
    slab = jnp.concatenate(
        [ik, iw, bg, jnp.zeros((d, LANE - IDX_DIM - IDX_HEADS - GATE_RANK), w_in.dtype)], axis=1)
    w_a = jnp.concatenate([_pad_heads(aq, A_HEADS, A_HEAD_DIM),
                           _pad_heads(ak, A_KV_HEADS, A_HEAD_DIM), av], axis=1)
    w_f = jnp.concatenate([_pad_heads(iq, IDX_HEADS, IDX_DIM), _pad_heads(bq, B_HEADS, B_KEY_DIM),
                           _pad_heads(bk, B_HEADS, B_KEY_DIM), bv, br, slab], axis=1)
    w_all = jnp.concatenate([w_a, w_f], axis=1).astype(BF16)
    g0 = IDX_DIM + IDX_HEADS
    w2p = jnp.zeros((LANE, B_HEADS * LANE), F32).at[g0:g0 + GATE_RANK].set(
        _pad_heads(gate_w2, B_HEADS, B_KEY_DIM)).astype(BF16)
    gbp = _pad_heads(gate_b[None, :], B_HEADS, B_KEY_DIM)
    return w_all, (w_a.shape[1], w_f.shape[1]), w2p, gbp


def _even_mixer_ln(x2, bsz, seq, w_in, bias_tiles, gate_w2, gate_b, norm_g, w_out, ln_g, ln_b):
    w_all, widths, w2p, gbp = _even_layer_weights(w_in, gate_w2, gate_b)
    h_a, h_f = _in_proj(x2, w_all, widths, (BF16, F32))
    o_a = _dsa(h_a, h_f, bias_tiles, bsz, seq)
    o_b = _gla(h_f, w2p, gbp, norm_g[None, :], bsz, seq)
    return _outproj_ln(o_a, o_b, x2, w_out.astype(BF16), ln_g[None, :], ln_b[None, :])


def _block_diag(m, per):
    g, r, c = m.shape
    eye = jnp.eye(per, dtype=m.dtype)
    m = m.reshape(g // per, per, r, c)
    bd = m[:, :, :, None, :] * eye[None, :, None, :, None]
    return bd.reshape(g // per, per * r, per * c)


def _s5_mixer_ln(h2, bsz, seq, w_in, lam_re, lam_im, log_dt, b_re, b_im, c_re, c_im, d_skip,
                 glu_w1, glu_w2, w_out, ln_g, ln_b, groups_per_block=16):
    ab_re, ab_im, bb_re, bb_im = _s5_prep(lam_re, lam_im, log_dt, b_re, b_im)
    per = groups_per_block
    n_cb = C_GROUPS // per
    bbre = _block_diag(jnp.swapaxes(bb_re, 1, 2), per).astype(BF16)
    bbim = _block_diag(jnp.swapaxes(bb_im, 1, 2), per).astype(BF16)
    a_re = ab_re.reshape(n_cb, 1, per * C_STATE)
    a_im = ab_im.reshape(n_cb, 1, per * C_STATE)
    cre = _block_diag(jnp.swapaxes(c_re, 1, 2), per)
    cim = _block_diag(jnp.swapaxes(c_im, 1, 2), per)
    cmat = jnp.concatenate([cre, -cim], axis=1).astype(BF16)
    u = _s5_uproj(h2, w_in.astype(BF16))
    g = _s5_scan(u, bbre, bbim, a_re, a_im, cmat, d_skip[None, :], bsz, seq)
    return _s5_glu(g, h2, glu_w1.astype(BF16), glu_w2.astype(BF16), w_out.astype(BF16),
                   ln_g[None, :], ln_b[None, :])


def _moe_ln(h, r_coarse, rb_coarse, r_fine, rb_fine, w_gate, w_up, w_down, ln_g, ln_b,
            regroup=None):
    d = h.shape[1]
    rows = MOE_GROUPS + N_EXPERTS
    wr = jnp.concatenate([r_coarse.T, jnp.transpose(r_fine, (0, 2, 1)).reshape(N_EXPERTS, d),
                          jnp.zeros((32 - rows, d), F32)], axis=0)
    rb = jnp.concatenate([rb_coarse, rb_fine.reshape(N_EXPERTS), jnp.zeros((32 - rows,), F32)])
    return _moe(h, wr, rb[:, None], w_gate.astype(BF16), w_up.astype(BF16), w_down.astype(BF16),
                ln_g[None, :], ln_b[None, :], regroup)


def kernel(x, rel_bias, ab_w_in, gla_gate_w2, gla_gate_b, gla_norm_g, ab_w_out, s5_w_in, s5_lam_re, s5_lam_im, s5_log_dt, s5_b_re, s5_b_im, s5_c_re, s5_c_im, s5_d, s5_glu_w1, s5_glu_w2, s5_w_out, ln_mix_g, ln_mix_b, ln_ffn_g, ln_ffn_b, moe_r_coarse, moe_rb_coarse, moe_r_fine, moe_rb_fine, moe_w_gate, moe_w_up, moe_w_down):
    bsz, seq, d = x.shape
    h = x.reshape(bsz * seq, d)
    bias_tiles = _bias_tiles(rel_bias)
    time_major = False
    for layer in range(DEPTH):
        i = layer // 2
        assert time_major == (layer % 2 == 1)
        if layer % 2 == 0:
            h = _even_mixer_ln(h, bsz, seq, ab_w_in[i], bias_tiles, gla_gate_w2[i], gla_gate_b[i],
                               gla_norm_g[i], ab_w_out[i], ln_mix_g[layer], ln_mix_b[layer])
        else:
            h = _s5_mixer_ln(h, bsz, seq, s5_w_in[i], s5_lam_re[i], s5_lam_im[i], s5_log_dt[i],
                             s5_b_re[i], s5_b_im[i], s5_c_re[i], s5_c_im[i], s5_d[i],
                             s5_glu_w1[i], s5_glu_w2[i], s5_w_out[i],
                             ln_mix_g[layer], ln_mix_b[layer])
        want_time_major = layer + 1 < DEPTH and (layer + 1) % 2 == 1
        regroup = None
        if want_time_major != time_major:
            regroup = (seq, bsz) if time_major else (bsz, seq)
        h = _moe_ln(h, moe_r_coarse[layer], moe_rb_coarse[layer], moe_r_fine[layer],
                    moe_rb_fine[layer], moe_w_gate[layer], moe_w_up[layer], moe_w_down[layer],
                    ln_ffn_g[layer], ln_ffn_b[layer], regroup)
        time_major = want_time_major
    return h.reshape(bsz, seq, d).astype(x.dtype)
```

```python
import functools
import math

import numpy as np
import jax
import jax.numpy as jnp
from jax import lax
from jax.experimental import pallas as pl
from jax.experimental.pallas import tpu as pltpu

F32 = jnp.float32
BF16 = jnp.bfloat16
I32 = jnp.int32
I16 = jnp.int16
HALF16 = 1 << 15

D_MODEL = 1024
DEPTH = 2
DN_ALPHA = (2.0 * DEPTH) ** 0.25
LN_EPS = 1e-5
A_HEAD_DIM = 64
A_HEADS = 8
A_KV_HEADS = 2
IDX_HEADS = 4
IDX_DIM = 64
TOPK_MAX = 256
REL_BUCKETS = 32
REL_MAX_DIST = 128
B_HEADS = 4
B_VAL_DIM = 128
B_KEY_DIM = 64
GATE_RANK = 16
GATE_TAU = 16.0
GLA_CHUNK = 64
C_GROUP = 16
C_GROUPS = 64
C_STATE = 64
MOE_GROUPS = 4
MOE_EXPERTS = 4
MOE_HIDDEN = 512
N_EXPERTS = MOE_GROUPS * MOE_EXPERTS

LANE = 128
VMEM_LIMIT = 52 * 1024 * 1024
NEG_INF = float("-inf")
INT_MIN = -(2 ** 31)

_NT = (((1,), (1,)), ((), ()))
_TN = (((0,), (0,)), ((), ()))


def _dot(a, b):
    return jnp.dot(a, b, preferred_element_type=F32)


def _dot_nt(a, b):
    return lax.dot_general(a, b, _NT, preferred_element_type=F32)


def _dot_tn(a, b):
    return lax.dot_general(a, b, _TN, preferred_element_type=F32)


def _split3(x):
    hi = x.astype(BF16)
    r1 = x - hi.astype(F32)
    mid = r1.astype(BF16)
    lo = (r1 - mid.astype(F32)).astype(BF16)
    return hi, mid, lo


def _params(sem):
    return pltpu.CompilerParams(dimension_semantics=sem, vmem_limit_bytes=VMEM_LIMIT)


def _layer_norm(y, g, b):
    mu = jnp.mean(y, axis=-1, keepdims=True)
    d = y - mu
    var = jnp.mean(d * d, axis=-1, keepdims=True)
    return d * lax.rsqrt(var + LN_EPS) * g + b


def _sigmoid(x):
    return 1.0 / (1.0 + jnp.exp(-x))


def _bucket_thresholds():
    max_exact = REL_BUCKETS // 2
    nf = np.arange(max_exact, 4 * REL_MAX_DIST).astype(np.float32)
    large = max_exact + (np.log(nf / np.float32(max_exact))
                         / np.float32(math.log(REL_MAX_DIST / max_exact))
                         * np.float32(REL_BUCKETS - max_exact)).astype(np.int32)
    large = np.minimum(large, REL_BUCKETS - 1)
    return [int(nf[np.argmax(large >= max_exact + j)]) for j in range(1, REL_BUCKETS - max_exact)]


def _bias_kernel(rb_ref, out_ref):
    max_exact = REL_BUCKETS // 2
    srow = lax.broadcasted_iota(I32, (LANE, LANE), 0)
    tcol = lax.broadcasted_iota(I32, (LANE, LANE), 1)
    thr = _bucket_thresholds()
    for band in range(2):
        n = jnp.maximum(band * LANE + tcol - srow, 0)
        large = jnp.full((LANE, LANE), max_exact, I32)
        for t in thr:
            large = large + jnp.where(n >= t, 1, 0)
        bucket = jnp.where(n < max_exact, n, large)
        for h in range(A_HEADS):
            acc = jnp.zeros((LANE, LANE), F32)
            for bk in range(REL_BUCKETS):
                acc = acc + jnp.where(bucket == bk, rb_ref[bk, h], 0.0)
            out_ref[h, band] = acc
    for h in range(A_HEADS):
        out_ref[h, 2] = jnp.full((LANE, LANE), rb_ref[REL_BUCKETS - 1, h], F32)


def _bias_tiles(rel_bias):
    return pl.pallas_call(
        _bias_kernel,
        out_shape=jax.ShapeDtypeStruct((A_HEADS, 3, LANE, LANE), F32),
        in_specs=[pl.BlockSpec(memory_space=pltpu.SMEM)],
        name="rel_bias_tiles",
    )(rel_bias)


def _proj_kernel(x_ref, w_ref, *out_refs, widths, chunk):
    xb = x_ref[...].astype(BF16)
    c0 = 0
    for o_ref, width in zip(out_refs, widths):
        for c in range(0, width, chunk):
            ce = min(c + chunk, width)
            o_ref[:, c:ce] = _dot(xb, w_ref[:, c0 + c:c0 + ce]).astype(o_ref.dtype)
        c0 += width


def _in_proj(x2, w, widths, dtypes, tm=512, chunk=256):
    n, d = x2.shape
    return pl.pallas_call(
        functools.partial(_proj_kernel, widths=widths, chunk=chunk),
        out_shape=[jax.ShapeDtypeStruct((n, wd), dt) for wd, dt in zip(widths, dtypes)],
        grid=(n // tm,),
        in_specs=[pl.BlockSpec((tm, d), lambda i: (i, 0)),
                  pl.BlockSpec(w.shape, lambda i: (0, 0))],
        out_specs=[pl.BlockSpec((tm, wd), lambda i: (i, 0)) for wd in widths],
        compiler_params=_params(("arbitrary",)),
        name="in_proj",
    )(x2, w)


def _sortable(x):
    bits = lax.bitcast_convert_type(x, I32)
    return jnp.where(bits < 0, bits ^ jnp.int32(0x7FFFFFFF), bits)


def _dsa_kernel(q_ref, k_ref, v_ref, iq_ref, slab_ref, slabq_ref, bias_ref, o_ref,
                key_ref, hi_ref, lo_ref, am_ref, vt_ref, acc_ref, lg_ref, tmax_ref, ot_ref,
                *, k_sel, seq):
    i = pl.program_id(1)
    t0 = i * LANE
    nkt = i + 1
    kt2 = 2 * LANE
    n2 = lax.shift_right_logical(nkt + 1, 1)
    rep = A_HEADS // A_KV_HEADS
    srow_2 = lax.broadcasted_iota(I32, (kt2, LANE), 0)
    tcol = t0 + lax.broadcasted_iota(I32, (kt2, LANE), 1)

    @pl.when(i == 0)
    def _():
        for jt in range(seq // kt2):
            vt_ref[jt] = v_ref[jt * kt2:(jt + 1) * kt2, :].astype(F32).T.astype(BF16)

    slab_t = slabq_ref[...].T
    w_t = slab_t[IDX_DIM:IDX_DIM + IDX_HEADS, :] * (IDX_HEADS ** -0.5)
    iqb = iq_ref[...].astype(BF16)
    iq_stack = jnp.concatenate([iqb[:, h * LANE:(h + 1) * LANE] for h in range(IDX_HEADS)], axis=0)

    def score_tile(j, carry):
        s0 = pl.multiple_of(j * kt2, kt2)
        ikt = slab_ref[pl.ds(s0, kt2), :].astype(BF16)
        d = _dot_nt(ikt, iq_stack) * (IDX_DIM ** -0.5)
        acc = jnp.zeros((kt2, LANE), F32)
        for h in range(IDX_HEADS):
            acc = acc + jnp.maximum(d[:, h * LANE:(h + 1) * LANE], 0.0) * w_t[h:h + 1, :]
        acc = jnp.where(s0 + srow_2 <= tcol, acc, NEG_INF)
        key = _sortable(acc)
        key_ref[pl.ds(s0, kt2), :] = key
        hi_ref[pl.ds(s0, kt2), :] = lax.shift_right_arithmetic(key, 16).astype(I16)
        lo_ref[pl.ds(s0, kt2), :] = ((key & 0xFFFF) - HALF16).astype(I16)
        return carry

    lax.fori_loop(0, n2, score_tile, 0)
    tail = pl.multiple_of(nkt * LANE, LANE)
    key_ref[pl.ds(tail, LANE), :] = jnp.full((LANE, LANE), INT_MIN, I32)
    hi_ref[pl.ds(tail, 3 * LANE), :] = jnp.full((3 * LANE, LANE), -HALF16, I16)
    lo_ref[pl.ds(tail, 3 * LANE), :] = jnp.full((3 * LANE, LANE), -HALF16, I16)

    def count(pred_fn):
        def body(j, c):
            s0 = pl.multiple_of(j * kt2, kt2)
            kt = key_ref[pl.ds(s0, kt2), :]
            m = jnp.where(pred_fn(kt, s0 + srow_2), 1, 0)
            return c + jnp.sum(m.reshape(kt2 // 8, 8, LANE), axis=0)
        c = lax.fori_loop(0, n2, body, jnp.zeros((8, LANE), I32))
        return jnp.sum(c, axis=0, keepdims=True)

    def bcast(v):
        return jnp.broadcast_to(v, (kt2, LANE))

    kt4 = 4 * LANE
    n4 = lax.shift_right_logical(nkt + 3, 2)

    def bcast16(v):
        return jnp.broadcast_to(v.astype(I16), (kt4, LANE))

    def count16(ref, pred_fn):
        def body(j, c):
            kt = ref[pl.ds(pl.multiple_of(j * kt4, kt4), kt4), :]
            m = jnp.where(pred_fn(kt), jnp.int16(1), jnp.int16(0))
            parts = [m[r * 16:(r + 1) * 16] for r in range(kt4 // 16)]
            while len(parts) > 1:
                parts = [a + b for a, b in zip(parts[0::2], parts[1::2])]
            return c + parts[0]
        c = lax.fori_loop(0, n4, body, jnp.zeros((16, LANE), I16))
        return jnp.sum(c.astype(I32), axis=0, keepdims=True)

    def kth_largest16(ref, need):
        c0 = count16(ref, lambda kt: kt >= jnp.int16(0))
        v0 = jnp.where(c0 >= need, 0, -HALF16).astype(I32)

        def bit_body(bi, v):
            cand = v | lax.shift_left(jnp.int32(1), 14 - bi)
            cb = bcast16(cand)
            cnt = count16(ref, lambda kt: kt >= cb)
            return jnp.where(cnt >= need, cand, v)

        return lax.fori_loop(0, 15, bit_body, v0)

    def search():
        ans_hi = kth_largest16(hi_ref, k_sel)
        hb = bcast16(ans_hi)
        need_lo = k_sel - count16(hi_ref, lambda kt: kt > hb)

        def keep_low(j, carry):
            rows = pl.ds(pl.multiple_of(j * kt4, kt4), kt4)
            lo_ref[rows, :] = jnp.where(hi_ref[rows, :] == hb, lo_ref[rows, :], jnp.int16(-HALF16))
            return carry

        lax.fori_loop(0, n4, keep_low, 0)
        ans_lo = kth_largest16(lo_ref, need_lo)
        ans = lax.shift_left(ans_hi, 16) | (ans_lo + HALF16)
        ab = bcast(ans)
        cnt_ge = count(lambda kt, s: kt >= ab)
        cnt_gt = count(lambda kt, s: kt > ab)
        need = k_sel - cnt_gt

        def tie_search():
            def idx_body(bi, x):
                cand = x | lax.shift_left(jnp.int32(1), (seq.bit_length() - 2) - bi)
                cb = bcast(cand)
                f = count(lambda kt, s: (kt == ab) & (s < cb))
                return jnp.where(f < need, cand, x)
            return lax.fori_loop(0, seq.bit_length() - 1, idx_body, jnp.zeros((1, LANE), I32))

        cut = lax.cond(jnp.max(cnt_ge) > k_sel, tie_search,
                       lambda: jnp.full((1, LANE), seq - 1, I32))
        return ans, cut

    ans, cut = lax.cond(t0 >= k_sel, search,
                        lambda: (jnp.full((1, LANE), INT_MIN, I32),
                                 jnp.full((1, LANE), seq - 1, I32)))
    ans_b = bcast(ans)
    cut_b = bcast(cut)

    def mask_tile(j, carry):
        s0 = pl.multiple_of(j * kt2, kt2)
        kt = key_ref[pl.ds(s0, kt2), :]
        srow = s0 + srow_2
        sel = (kt > ans_b) | ((kt == ans_b) & (srow <= cut_b))
        am_ref[pl.ds(s0, kt2), :] = jnp.where(sel & (srow <= tcol), 0.0, NEG_INF)
        return carry

    lax.fori_loop(0, n2, mask_tile, 0)

    scale = jnp.asarray(A_HEAD_DIM ** -0.5, BF16)
    q_stacks = [jnp.concatenate([q_ref[:, (g * rep + r) * LANE:(g * rep + r + 1) * LANE]
                                 for r in range(rep)], axis=0) * scale for g in range(A_KV_HEADS)]
    acc_ref[...] = jnp.zeros(acc_ref.shape, F32)

    def logits_stage(j):
        slot = j & 1
        s0 = pl.multiple_of(j * kt2, kt2)
        am = am_ref[pl.ds(s0, kt2), :]
        band0 = jnp.clip(i - 2 * j, 0, 2)
        band1 = jnp.clip(i - 2 * j - 1, 0, 2)
        for g in range(A_KV_HEADS):
            kt = k_ref[pl.ds(s0, kt2), g * LANE:(g + 1) * LANE]
            lg_all = _dot_nt(kt, q_stacks[g])
            for r in range(rep):
                h = g * rep + r
                cs = slice(r * LANE, (r + 1) * LANE)
                bias = jnp.concatenate([bias_ref[h, band0], bias_ref[h, band1]], axis=0)
                lg = lg_all[:, cs] + bias + am
                lg_ref[slot, g, :, cs] = lg
                tmax_ref[slot, h:h + 1, :] = jnp.max(lg, axis=0, keepdims=True)

    def softmax_stage(j, carry):
        ms, ls = carry
        slot = j & 1
        new_ms, new_ls = [], []
        for g in range(A_KV_HEADS):
            ps, alphas = [], []
            for r in range(rep):
                h = g * rep + r
                m_new = jnp.maximum(ms[h], tmax_ref[slot, h:h + 1, :])
                m_safe = jnp.where(m_new == NEG_INF, 0.0, m_new)
                alpha = jnp.exp(ms[h] - m_safe)
                p = jnp.exp(lg_ref[slot, g, :, r * LANE:(r + 1) * LANE] - m_safe)
                new_ms.append(m_new)
                new_ls.append(alpha * ls[h] + jnp.sum(p, axis=0, keepdims=True))
                ps.append(p.astype(BF16))
                alphas.append(alpha)
            vt = vt_ref[j, g * A_HEAD_DIM:(g + 1) * A_HEAD_DIM, :]
            pv = _dot(vt, jnp.concatenate(ps, axis=1))
            acc_ref[g] = acc_ref[g] * jnp.concatenate(alphas, axis=1) + pv
        return tuple(new_ms), tuple(new_ls)

    def tile_body(j, carry):
        carry = softmax_stage(j, carry)
        logits_stage(j + 1)
        return carry

    init = (tuple(jnp.full((1, LANE), NEG_INF, F32) for _ in range(A_HEADS)),
            tuple(jnp.zeros((1, LANE), F32) for _ in range(A_HEADS)))
    logits_stage(0)
    carry = lax.fori_loop(0, n2 - 1, tile_body, init)
    _, ls = softmax_stage(n2 - 1, carry)
    for g in range(A_KV_HEADS):
        for r in range(rep):
            h = g * rep + r
            ot_ref[h * A_HEAD_DIM:(h + 1) * A_HEAD_DIM, :] = (
                acc_ref[g, :, r * LANE:(r + 1) * LANE] / ls[h])
    o_ref[...] = ot_ref[...].T.astype(o_ref.dtype)


def _dsa(h_a, h_f, bias_tiles, bsz, seq):
    n = bsz * seq
    nq = seq // LANE
    k_sel = min(TOPK_MAX, seq // 4)
    assert k_sel % LANE == 0 and seq % (2 * LANE) == 0
    qw = A_HEADS * LANE
    slab_blk = (h_f.shape[1] - LANE) // LANE
    return pl.pallas_call(
        functools.partial(_dsa_kernel, k_sel=k_sel, seq=seq),
        out_shape=jax.ShapeDtypeStruct((n, A_HEADS * A_HEAD_DIM), BF16),
        grid=(bsz, nq),
        in_specs=[
            pl.BlockSpec((LANE, qw), lambda b, i: (b * nq + i, 0)),
            pl.BlockSpec((seq, A_KV_HEADS * LANE), lambda b, i: (b, qw // (A_KV_HEADS * LANE))),
            pl.BlockSpec((seq, LANE), lambda b, i: (b, (qw + A_KV_HEADS * LANE) // LANE)),
            pl.BlockSpec((LANE, IDX_HEADS * LANE), lambda b, i: (b * nq + i, 0)),
            pl.BlockSpec((seq, LANE), lambda b, i: (b, slab_blk)),
            pl.BlockSpec((LANE, LANE), lambda b, i: (b * nq + i, slab_blk)),
            pl.BlockSpec((A_HEADS, 3, LANE, LANE), lambda b, i: (0, 0, 0, 0)),
        ],
        out_specs=pl.BlockSpec((LANE, A_HEADS * A_HEAD_DIM), lambda b, i: (b * nq + i, 0)),
        scratch_shapes=[
            pltpu.VMEM((seq + LANE, LANE), I32),
            pltpu.VMEM((seq + 3 * LANE, LANE), I16),
            pltpu.VMEM((seq + 3 * LANE, LANE), I16),
            pltpu.VMEM((seq, LANE), F32),
            pltpu.VMEM((nq // 2, LANE, 2 * LANE), BF16),
            pltpu.VMEM((A_KV_HEADS, A_HEAD_DIM, (A_HEADS // A_KV_HEADS) * LANE), F32),
            pltpu.VMEM((2, A_KV_HEADS, 2 * LANE, (A_HEADS // A_KV_HEADS) * LANE), F32),
            pltpu.VMEM((2, A_HEADS, LANE), F32),
            pltpu.VMEM((A_HEADS * A_HEAD_DIM, LANE), F32),
        ],
        compiler_params=_params(("arbitrary", "arbitrary")),
        name="dsa_attention",
    )(h_a, h_a, h_a, h_f, h_f, h_f, bias_tiles)


def _gla_kernel(bq_ref, bk_ref, bv_ref, br_ref, slab_ref, w2_ref, gb_ref, ng_ref, o_ref,
                st_ref, *, n_chunks):
    @pl.when(pl.program_id(1) == 0)
    def _():
        st_ref[...] = jnp.zeros(st_ref.shape, F32)

    ch = GLA_CHUNK
    rws = n_chunks * ch
    shift = ch.bit_length() - 1
    assert 1 << shift == ch
    row = lax.broadcasted_iota(I32, (rws, rws), 0)
    col = lax.broadcasted_iota(I32, (rws, rws), 1)
    tril = (lax.shift_right_logical(row, shift) == lax.shift_right_logical(col, shift)) & (row >= col)
    tri = jnp.where(tril, 1.0, 0.0).astype(BF16)
    gate = _dot(slab_ref[...].astype(BF16), w2_ref[...]) + gb_ref[...]
    log_a = (jnp.minimum(gate, 0.0) - jnp.log1p(jnp.exp(-jnp.abs(gate)))) / GATE_TAU
    hi, mid, lo = _split3(log_a)
    cum = _dot(tri, hi) + _dot(tri, mid) + _dot(tri, lo)
    lasts = [cum[(c + 1) * ch - 1:(c + 1) * ch, :] for c in range(n_chunks)]
    last = jnp.concatenate([jnp.broadcast_to(l, (ch, l.shape[1])) for l in lasts], axis=0)
    q = bq_ref[...] * (B_KEY_DIM ** -0.5)
    k = bk_ref[...]
    q_dec = (q * jnp.exp(cum)).astype(BF16)
    k_inv = (k * jnp.exp(-cum)).astype(BF16)
    k_end = (k * jnp.exp(last - cum)).astype(BF16)
    decays = [jnp.exp(l) for l in lasts]
    for h in range(B_HEADS):
        cs = slice(h * LANE, (h + 1) * LANE)
        v = bv_ref[:, cs].astype(BF16)
        sc = jnp.where(tril, _dot_nt(q_dec[:, cs], k_inv[:, cs]), 0.0)
        o_intra = _dot(sc.astype(BF16), v)
        st = st_ref[h]
        parts = []
        for c in range(n_chunks):
            rows = slice(c * ch, (c + 1) * ch)
            parts.append(o_intra[rows] + _dot_nt(q_dec[rows, cs], st.astype(BF16)))
            st = decays[c][:, cs] * st + _dot_tn(v[rows], k_end[rows, cs])
        st_ref[h] = st
        o = jnp.concatenate(parts, axis=0)
        mu = jnp.mean(o, axis=-1, keepdims=True)
        dlt = o - mu
        var = jnp.mean(dlt * dlt, axis=-1, keepdims=True)
        on = dlt * lax.rsqrt(var + LN_EPS) * ng_ref[:, cs]
        r = br_ref[:, cs]
        o_ref[:, cs] = (on * (r * _sigmoid(r))).astype(o_ref.dtype)


def _gla(h_f, w2p, gbp, norm_g, bsz, seq, rows_per_step=256):
    n = bsz * seq
    hw = B_HEADS * LANE
    steps = seq // rows_per_step
    slab_blk = (h_f.shape[1] - LANE) // LANE
    blk = lambda cb: pl.BlockSpec((rows_per_step, hw), lambda b, s, cb=cb: (b * steps + s, cb))
    return pl.pallas_call(
        functools.partial(_gla_kernel, n_chunks=rows_per_step // GLA_CHUNK),
        out_shape=jax.ShapeDtypeStruct((n, hw), BF16),
        grid=(bsz, steps),
        in_specs=[blk(1), blk(2), blk(3), blk(4),
                  pl.BlockSpec((rows_per_step, LANE), lambda b, s: (b * steps + s, slab_blk)),
                  pl.BlockSpec((LANE, hw), lambda b, s: (0, 0)),
                  pl.BlockSpec((1, hw), lambda b, s: (0, 0)),
                  pl.BlockSpec((1, hw), lambda b, s: (0, 0))],
        out_specs=pl.BlockSpec((rows_per_step, hw), lambda b, s: (b * steps + s, 0)),
        scratch_shapes=[pltpu.VMEM((B_HEADS, LANE, LANE), F32)],
        compiler_params=_params(("arbitrary", "arbitrary")),
        name="gla_attention",
    )(h_f, h_f, h_f, h_f, h_f, w2p, gbp, norm_g)


def _outproj_ln_kernel(oa_ref, ob_ref, x_ref, w_ref, g_ref, b_ref, out_ref):
    ka = oa_ref.shape[1]
    m = _dot(oa_ref[...], w_ref[0:ka, :]) + _dot(ob_ref[...], w_ref[ka:, :])
    out_ref[...] = _layer_norm(DN_ALPHA * x_ref[...] + m, g_ref[...], b_ref[...])


def _outproj_ln(o_a, o_b, x2, w, g, b, tm=512):
    n, d = x2.shape
    row = lambda width: pl.BlockSpec((tm, width), lambda i: (i, 0))
    full = lambda a: pl.BlockSpec(a.shape, lambda i: (0, 0))
    return pl.pallas_call(
        _outproj_ln_kernel,
        out_shape=jax.ShapeDtypeStruct((n, d), F32),
        grid=(n // tm,),
        in_specs=[row(o_a.shape[1]), row(o_b.shape[1]), row(d), full(w), full(g), full(b)],
        out_specs=row(d),
        compiler_params=_params(("arbitrary",)),
        name="out_proj_ln",
    )(o_a, o_b, x2, w, g, b)


def _route(lt):
    gl = [lt[g:g + 1, :] for g in range(MOE_GROUPS)]
    best, gsel = gl[0], jnp.zeros_like(gl[0], dtype=I32)
    for g in range(1, MOE_GROUPS):
        better = gl[g] > best
        gsel = jnp.where(better, g, gsel)
        best = jnp.where(better, gl[g], best)
    denom = sum(jnp.exp(x - best) for x in gl)
    g_w = 1.0 / denom
    fl = []
    for e in range(MOE_EXPERTS):
        acc = jnp.zeros_like(best)
        for g in range(MOE_GROUPS):
            r = MOE_GROUPS + g * MOE_EXPERTS + e
            acc = jnp.where(gsel == g, lt[r:r + 1, :], acc)
        fl.append(acc)
    v1, i1 = fl[0], jnp.zeros_like(gsel)
    for e in range(1, MOE_EXPERTS):
        better = fl[e] > v1
        i1 = jnp.where(better, e, i1)
        v1 = jnp.where(better, fl[e], v1)
    v2, i2 = jnp.full_like(v1, NEG_INF), jnp.zeros_like(gsel)
    for e in range(MOE_EXPERTS):
        better = (fl[e] > v2) & (i1 != e)
        i2 = jnp.where(better, e, i2)
        v2 = jnp.where(better, fl[e], v2)
    e2 = jnp.exp(v2 - v1)
    w1 = (1.0 / (1.0 + e2)) * g_w
    w2 = (e2 / (1.0 + e2)) * g_w
    e_gate = [jnp.where(i1 == e, w1, 0.0) + jnp.where(i2 == e, w2, 0.0)
              for e in range(MOE_EXPERTS)]
    return gsel, e_gate


def _moe_route_kernel(x_ref, wr_ref, rb_ref, xa_ref, rt_ref, cnt_ref, tri_ref, carry_ref):
    i = pl.program_id(0)
    tm, d = x_ref.shape

    @pl.when(i == 0)
    def _():
        carry_ref[...] = jnp.zeros(carry_ref.shape, F32)
        before = (lax.broadcasted_iota(I32, (tm, tm), 0) < lax.broadcasted_iota(I32, (tm, tm), 1))
        tri_ref[...] = jnp.where(before, 1.0, 0.0).astype(BF16)

    x = x_ref[...]
    xs = _split3(x)
    ws = _split3(wr_ref[...])
    lt = jnp.zeros((wr_ref.shape[0], tm), F32)
    for a, b in ((2, 0), (0, 2), (1, 1), (1, 0), (0, 1), (0, 0)):
        lt = lt + _dot_nt(ws[a], xs[b])
    lt = lt + rb_ref[...]
    gsel, e_gate = _route(lt)

    row8 = lax.broadcasted_iota(I32, (8, tm), 0)
    onehot = jnp.where(row8 == gsel, 1.0, 0.0)
    earlier = _dot(onehot.astype(BF16), tri_ref[...])
    rank = jnp.sum(onehot * (earlier + carry_ref[:, 0:1]), axis=0, keepdims=True)
    rt_ref[...] = jnp.where(row8 == 0, gsel, jnp.where(row8 == 1, rank.astype(I32), 0))
    carry_ref[...] = carry_ref[...] + jnp.sum(onehot, axis=1, keepdims=True)
    cnt_ref[...] = carry_ref[...].astype(I32)

    erow = lax.broadcasted_iota(I32, (LANE, tm), 0)
    gt = jnp.zeros((LANE, tm), F32)
    for e in range(MOE_EXPERTS):
        gt = jnp.where(erow == e, e_gate[e], gt)
    xa_ref[:, :d] = x
    xa_ref[:, d:] = gt.T


def _moe_route(h, wr, rb, tm=1024):
    n, d = h.shape
    tm = min(tm, n)
    full = lambda a: pl.BlockSpec(a.shape, lambda i: (0,) * a.ndim)
    return pl.pallas_call(
        _moe_route_kernel,
        out_shape=[jax.ShapeDtypeStruct((n, d + LANE), F32), jax.ShapeDtypeStruct((8, n), I32),
                   jax.ShapeDtypeStruct((8, LANE), I32)],
        grid=(n // tm,),
        in_specs=[pl.BlockSpec((tm, d), lambda i: (i, 0)), full(wr), full(rb)],
        out_specs=[pl.BlockSpec((tm, d + LANE), lambda i: (i, 0)),
                   pl.BlockSpec((8, tm), lambda i: (0, i)),
                   pl.BlockSpec((8, LANE), lambda i: (0, 0))],
        scratch_shapes=[pltpu.VMEM((tm, tm), BF16), pltpu.VMEM((8, LANE), F32)],
        compiler_params=_params(("arbitrary",)),
        name="moe_route",
    )(h, wr, rb)


def _row_copy_waves(n_rows, wave, start_row, wait_wave):
    n_waves = n_rows // wave
    for w in range(n_waves + 1):
        if w < n_waves:
            lax.fori_loop(w * wave, (w + 1) * wave, lambda r, c, w=w: (start_row(r, w % 2), c)[1], 0,
                          unroll=8)
        if w >= 1:
            wait_wave((w - 1) % 2)


def _moe_dispatch_kernel(pos_ref, fill_ref, xa_ref, xs_hbm, zero_ref, sem, *, wave):
    i = pl.program_id(0)
    tm = pos_ref.shape[1]
    tme = zero_ref.shape[0]

    def copy(r, s):
        return pltpu.make_async_copy(xa_ref.at[r], xs_hbm.at[pos_ref[0, r]], sem.at[s])

    def wait_wave(s):
        rows = pl.ds(0, wave)
        pltpu.make_async_copy(xa_ref.at[rows], xs_hbm.at[rows], sem.at[s]).wait()

    _row_copy_waves(tm, wave, lambda r, s: copy(r, s).start(), wait_wave)

    @pl.when(i == pl.num_programs(0) - 1)
    def _():
        zero_ref[...] = jnp.zeros(zero_ref.shape, F32)
        for g in range(MOE_GROUPS):
            first, count = fill_ref[0, g], fill_ref[1, g]

            def zcopy(r, first=first):
                return pltpu.make_async_copy(zero_ref.at[0], xs_hbm.at[first + r], sem.at[0])

            lax.fori_loop(0, count, lambda r, c: (zcopy(r).start(), c)[1], 0)
            lax.fori_loop(0, count, lambda r, c: (zcopy(r).wait(), c)[1], 0)
        n_tiles = fill_ref[2, 0]
        for k in range(MOE_GROUPS):
            @pl.when(k < fill_ref[2, 1])
            def _(k=k):
                tail = xs_hbm.at[pl.ds(pl.multiple_of((n_tiles + k) * tme, tme), tme)]
                cp = pltpu.make_async_copy(zero_ref, tail, sem.at[0])
                cp.start()
                cp.wait()


def _moe_dispatch(xa, pos, fill, n_sorted, tme, tm=1024, wave=256):
    n, da = xa.shape
    tm = min(tm, n)
    wave = min(wave, tm)
    return pl.pallas_call(
        functools.partial(_moe_dispatch_kernel, wave=wave),
        out_shape=jax.ShapeDtypeStruct((n_sorted, da), F32),
        grid=(n // tm,),
        in_specs=[pl.BlockSpec((1, tm), lambda i: (0, i), memory_space=pltpu.SMEM),
                  pl.BlockSpec(memory_space=pltpu.SMEM),
                  pl.BlockSpec((tm, da), lambda i: (i, 0))],
        out_specs=pl.BlockSpec(memory_space=pl.ANY),
        scratch_shapes=[pltpu.VMEM((tme, da), F32), pltpu.SemaphoreType.DMA((2,))],
        compiler_params=_params(("arbitrary",)),
        name="moe_dispatch",
    )(pos, fill, xa)


def _moe_expert_kernel(tg_ref, tj_ref, nt_ref, xs_ref, wg_ref, wu_ref, wd_ref, lg_ref, lb_ref,
                       ys_ref):
    d = ys_ref.shape[1]

    @pl.when(pl.program_id(0) < nt_ref[0])
    def _():
        x = xs_ref[:, :d]
        gates = xs_ref[:, d:]
        xb = x.astype(BF16)
        y = jnp.zeros(x.shape, F32)
        for e in range(MOE_EXPERTS):
            hg = _dot(xb, wg_ref[e])
            hu = _dot(xb, wu_ref[e])
            hid = hg * _sigmoid(hg) * hu * gates[:, e:e + 1]
            y = y + _dot(hid.astype(BF16), wd_ref[e])
        ys_ref[...] = _layer_norm(DN_ALPHA * x + y, lg_ref[...], lb_ref[...])

    @pl.when(pl.program_id(0) >= nt_ref[0])
    def _():
        ys_ref[...] = jnp.zeros(ys_ref.shape, F32)


def _moe_experts(xs, tile_g, tile_j, n_tiles, wg, wu, wd, lg, lb, tme):
    p, da = xs.shape
    d = da - LANE
    grp = lambda w: pl.BlockSpec((None,) + w.shape[1:], lambda i, tg, tj, nt: (tg[i], 0, 0, 0))
    full = lambda a: pl.BlockSpec(a.shape, lambda i, tg, tj, nt: (0,) * a.ndim)
    grid_spec = pltpu.PrefetchScalarGridSpec(
        num_scalar_prefetch=3,
        grid=(p // tme,),
        in_specs=[pl.BlockSpec((tme, da), lambda i, tg, tj, nt: (tj[i], 0)),
                  grp(wg), grp(wu), grp(wd), full(lg), full(lb)],
        out_specs=pl.BlockSpec((tme, d), lambda i, tg, tj, nt: (i, 0)),
    )
    return pl.pallas_call(
        _moe_expert_kernel,
        out_shape=jax.ShapeDtypeStruct((p, d), F32),
        grid_spec=grid_spec,
        compiler_params=_params(("arbitrary",)),
        name="moe_experts",
    )(tile_g, tile_j, n_tiles, xs, wg, wu, wd, lg, lb)


def _moe_combine_kernel(pos_ref, ys_hbm, o_ref, sem, *, wave):
    tm = o_ref.shape[0]

    def copy(r, s):
        return pltpu.make_async_copy(ys_hbm.at[pos_ref[0, r]], o_ref.at[r], sem.at[s])

    def wait_wave(s):
        rows = pl.ds(0, wave)
        pltpu.make_async_copy(ys_hbm.at[rows], o_ref.at[rows], sem.at[s]).wait()

    _row_copy_waves(tm, wave, lambda r, s: copy(r, s).start(), wait_wave)


def _moe_combine(ys, pos, n, tm=1024, wave=256):
    d = ys.shape[1]
    tm = min(tm, n)
    wave = min(wave, tm)
    return pl.pallas_call(
        functools.partial(_moe_combine_kernel, wave=wave),
        out_shape=jax.ShapeDtypeStruct((n, d), F32),
        grid=(n // tm,),
        in_specs=[pl.BlockSpec((1, tm), lambda i: (0, i), memory_space=pltpu.SMEM),
                  pl.BlockSpec(memory_space=pl.ANY)],
        out_specs=pl.BlockSpec((tm, d), lambda i: (i, 0)),
        scratch_shapes=[pltpu.SemaphoreType.DMA((2,))],
        compiler_params=_params(("arbitrary",)),
        name="moe_combine",
    )(pos, ys)


def _moe(h, wr, rb, wg, wu, wd, lg, lb, regroup=None, tme=512):
    n, d = h.shape
    tme = min(tme, n)
    xa, rt, cnt = _moe_route(h, wr, rb)
    counts = cnt[:MOE_GROUPS, 0]
    tiles = (counts + tme - 1) // tme
    first_tile = jnp.cumsum(tiles) - tiles
    gid, rank = rt[0], rt[1]
    pos = (first_tile[gid] * tme + rank)[None, :]
    n_slots = n // tme + MOE_GROUPS
    n_tiles = jnp.sum(tiles)
    fill = jnp.stack([first_tile * tme + counts, tiles * tme - counts,
                      jnp.zeros((MOE_GROUPS,), I32).at[0].set(n_tiles).at[1].set(n_slots - n_tiles)])
    slot = jnp.minimum(jnp.arange(n_slots, dtype=I32), n_tiles - 1)
    tile_g = jnp.sum(slot[:, None] >= (first_tile + tiles)[None, :], axis=1).astype(I32)
    xs = _moe_dispatch(xa, pos, fill, n_slots * tme, tme)
    ys = _moe_experts(xs, tile_g, slot, n_tiles[None].astype(I32), wg, wu, wd, lg, lb, tme)
    if regroup is not None:
        pos = pos.reshape(regroup).T.reshape(1, n)
    return _moe_combine(ys, pos, n)


def _s5_prep_kernel(lre_ref, lim_ref, ldt_ref, bre_ref, bim_ref, are_ref, aim_ref, bbre_ref,
                    bbim_ref):
    lr = jnp.minimum(lre_ref[...], -1e-4)
    li = lim_ref[...]
    dt = jnp.exp(ldt_ref[...])
    mag = jnp.exp(lr * dt)
    ab_re = mag * jnp.cos(li * dt)
    ab_im = mag * jnp.sin(li * dt)
    den = lr * lr + li * li
    nr = ab_re - 1.0
    coef_re = (nr * lr + ab_im * li) / den
    coef_im = (ab_im * lr - nr * li) / den
    are_ref[...] = ab_re
    aim_ref[...] = ab_im
    bbre_ref[...] = coef_re * bre_ref[...] - coef_im * bim_ref[...]
    bbim_ref[...] = coef_re * bim_ref[...] + coef_im * bre_ref[...]


def _s5_prep(lam_re, lam_im, log_dt, b_re, b_im):
    gp = C_GROUPS * C_STATE
    col = lambda a: a.reshape(gp, 1)
    ldt = jnp.broadcast_to(log_dt[:, None], (C_GROUPS, C_STATE))
    outs = pl.pallas_call(
        _s5_prep_kernel,
        out_shape=[jax.ShapeDtypeStruct((gp, 1), F32), jax.ShapeDtypeStruct((gp, 1), F32),
                   jax.ShapeDtypeStruct((gp, C_GROUP), F32), jax.ShapeDtypeStruct((gp, C_GROUP), F32)],
        name="s5_discretise",
    )(col(lam_re), col(lam_im), col(ldt), b_re.reshape(gp, C_GROUP), b_im.reshape(gp, C_GROUP))
    ab_re, ab_im, bb_re, bb_im = outs
    return (ab_re.reshape(C_GROUPS, C_STATE), ab_im.reshape(C_GROUPS, C_STATE),
            bb_re.reshape(C_GROUPS, C_STATE, C_GROUP), bb_im.reshape(C_GROUPS, C_STATE, C_GROUP))


def _s5_uproj_kernel(x_ref, w_ref, o_ref):
    o_ref[...] = _dot(x_ref[...].astype(BF16), w_ref[...])


def _s5_uproj(h2, w, tm=512):
    n, d = h2.shape
    return pl.pallas_call(
        _s5_uproj_kernel,
        out_shape=jax.ShapeDtypeStruct((n, d), F32),
        grid=(n // tm,),
        in_specs=[pl.BlockSpec((tm, d), lambda i: (i, 0)), pl.BlockSpec(w.shape, lambda i: (0, 0))],
        out_specs=pl.BlockSpec((tm, d), lambda i: (i, 0)),
        compiler_params=_params(("arbitrary",)),
        name="s5_in_proj",
    )(h2, w)


def _gelu_tanh(x):
    return 0.5 * x * (1.0 + jnp.tanh(math.sqrt(2.0 / math.pi) * (x + 0.044715 * (x * x * x))))


def _s5_scan_kernel(u_ref, bbre_ref, bbim_ref, are_ref, aim_ref, cm_ref, d_ref, g_ref,
                    st_ref, bure_ref, buim_ref, s_ref, *, bsz, tl, n_cb, sub):
    @pl.when(pl.program_id(0) == 0)
    def _():
        st_ref[...] = jnp.zeros(st_ref.shape, F32)

    cw = D_MODEL // n_cb
    sw = bure_ref.shape[2]

    def input_states(cb):
        ub = u_ref[:, cb * cw:(cb + 1) * cw].astype(BF16)
        bure_ref[cb % 2] = _dot(ub, bbre_ref[cb])
        buim_ref[cb % 2] = _dot(ub, bbim_ref[cb])

    input_states(0)
    for cb in range(n_cb):
        buf = cb % 2
        if cb + 1 < n_cb:
            input_states(cb + 1)
        for s0 in range(0, sw, sub):
            cs = slice(s0, s0 + sub)
            a_re = jnp.broadcast_to(are_ref[cb, :, cs], (bsz, sub))
            a_im = jnp.broadcast_to(aim_ref[cb, :, cs], (bsz, sub))
            s_re, s_im = st_ref[cb, 0, :, cs], st_ref[cb, 1, :, cs]
            for t in range(tl):
                rows = slice(t * bsz, (t + 1) * bsz)
                s_re, s_im = (a_re * s_re - a_im * s_im + bure_ref[buf, rows, cs],
                              a_re * s_im + a_im * s_re + buim_ref[buf, rows, cs])
                s_ref[buf, rows, cs] = s_re.astype(BF16)
                s_ref[buf, rows, sw + s0:sw + s0 + sub] = s_im.astype(BF16)
            st_ref[cb, 0, :, cs] = s_re
            st_ref[cb, 1, :, cs] = s_im
        ch = slice(cb * cw, (cb + 1) * cw)
        y = _dot(s_ref[buf], cm_ref[cb]) + d_ref[:, ch] * u_ref[:, ch]
        g_ref[:, ch] = _gelu_tanh(y).astype(g_ref.dtype)


def _s5_scan(u_lb, bbre, bbim, a_re, a_im, cmat, d_skip, bsz, seq, tl=32, sub=512):
    n, d = u_lb.shape
    n_cb, cw, sw = bbre.shape
    rows = tl * bsz
    full = lambda a: pl.BlockSpec(a.shape, lambda t: (0,) * a.ndim)
    return pl.pallas_call(
        functools.partial(_s5_scan_kernel, bsz=bsz, tl=tl, n_cb=n_cb, sub=sub),
        out_shape=jax.ShapeDtypeStruct((n, d), BF16),
        grid=(seq // tl,),
        in_specs=[pl.BlockSpec((rows, d), lambda t: (t, 0)), full(bbre), full(bbim), full(a_re),
                  full(a_im), full(cmat), full(d_skip)],
        out_specs=pl.BlockSpec((rows, d), lambda t: (t, 0)),
        scratch_shapes=[pltpu.VMEM((n_cb, 2, bsz, sw), F32), pltpu.VMEM((2, rows, sw), F32),
                        pltpu.VMEM((2, rows, sw), F32), pltpu.VMEM((2, rows, 2 * sw), BF16)],
        compiler_params=_params(("arbitrary",)),
        name="s5_scan",
    )(u_lb, bbre, bbim, a_re, a_im, cmat, d_skip)


def _s5_glu_kernel(g_ref, x_ref, w1_ref, w2_ref, wo_ref, lg_ref, lb_ref, out_ref):
    g = g_ref[...]
    z = _dot(g, w1_ref[...]) * _sigmoid(_dot(g, w2_ref[...]))
    m = _dot(z.astype(BF16), wo_ref[...])
    out_ref[...] = _layer_norm(DN_ALPHA * x_ref[...] + m, lg_ref[...], lb_ref[...])


def _s5_glu(g, h2, w1, w2, wo, lg, lb, tm=512):
    n, d = h2.shape
    row = pl.BlockSpec((tm, d), lambda i: (i, 0))
    full = lambda a: pl.BlockSpec(a.shape, lambda i: (0, 0))
    return pl.pallas_call(
        _s5_glu_kernel,
        out_shape=jax.ShapeDtypeStruct((n, d), F32),
        grid=(n // tm,),
        in_specs=[row, row, full(w1), full(w2), full(wo), full(lg), full(lb)],
        out_specs=row,
        compiler_params=_params(("arbitrary",)),
        name="s5_glu_out_ln",
    )(g, h2, w1, w2, wo, lg, lb)


def _pad_heads(w, heads, dim):
    d = w.shape[0]
    w = w.reshape(d, heads, dim)
    return jnp.pad(w, ((0, 0), (0, 0), (0, LANE - dim))).reshape(d, heads * LANE)


def _even_layer_weights(w_in, gate_w2, gate_b):
    splits = (A_HEADS * A_HEAD_DIM, A_KV_HEADS * A_HEAD_DIM, A_KV_HEADS * A_HEAD_DIM,
              IDX_HEADS * IDX_DIM, IDX_DIM, IDX_HEADS,
              B_HEADS * B_KEY_DIM, B_HEADS * B_KEY_DIM, B_HEADS * B_VAL_DIM, GATE_RANK,
              B_HEADS * B_VAL_DIM)
    offs = np.cumsum((0,) + splits)
    aq, ak, av, iq, ik, iw, bq, bk, bv, bg, br = [w_in[:, offs[k]:offs[k + 1]] for k in range(11)]
    d = w_in.shape[0]
    slab = jnp.concatenate(
        [ik, iw, bg, jnp.zeros((d, LANE - IDX_DIM - IDX_HEADS - GATE_RANK), w_in.dtype)], axis=1)
    w_a = jnp.concatenate([_pad_heads(aq, A_HEADS, A_HEAD_DIM),
                           _pad_heads(ak, A_KV_HEADS, A_HEAD_DIM), av], axis=1)
    w_f = jnp.concatenate([_pad_heads(iq, IDX_HEADS, IDX_DIM), _pad_heads(bq, B_HEADS, B_KEY_DIM),
                           _pad_heads(bk, B_HEADS, B_KEY_DIM), bv, br, slab], axis=1)
    w_all = jnp.concatenate([w_a, w_f], axis=1).astype(BF16)
    g0 = IDX_DIM + IDX_HEADS
    w2p = jnp.zeros((LANE, B_HEADS * LANE), F32).at[g0:g0 + GATE_RANK].set(
        _pad_heads(gate_w2, B_HEADS, B_KEY_DIM)).astype(BF16)
    gbp = _pad_heads(gate_b[None, :], B_HEADS, B_KEY_DIM)
    return w_all, (w_a.shape[1], w_f.shape[1]), w2p, gbp


def _even_mixer_ln(x2, bsz, seq, w_in, bias_tiles, gate_w2, gate_b, norm_g, w_out, ln_g, ln_b):
    w_all, widths, w2p, gbp = _even_layer_weights(w_in, gate_w2, gate_b)
    h_a, h_f = _in_proj(x2, w_all, widths, (BF16, F32))
    o_a = _dsa(h_a, h_f, bias_tiles, bsz, seq)
    o_b = _gla(h_f, w2p, gbp, norm_g[None, :], bsz, seq)
    return _outproj_ln(o_a, o_b, x2, w_out.astype(BF16), ln_g[None, :], ln_b[None, :])


def _block_diag(m, per):
    g, r, c = m.shape
    eye = jnp.eye(per, dtype=m.dtype)
    m = m.reshape(g // per, per, r, c)
    bd = m[:, :, :, None, :] * eye[None, :, None, :, None]
    return bd.reshape(g // per, per * r, per * c)


def _s5_mixer_ln(h2, bsz, seq, w_in, lam_re, lam_im, log_dt, b_re, b_im, c_re, c_im, d_skip,
                 glu_w1, glu_w2, w_out, ln_g, ln_b, groups_per_block=16):
    ab_re, ab_im, bb_re, bb_im = _s5_prep(lam_re, lam_im, log_dt, b_re, b_im)
    per = groups_per_block
    n_cb = C_GROUPS // per
    bbre = _block_diag(jnp.swapaxes(bb_re, 1, 2), per).astype(BF16)
    bbim = _block_diag(jnp.swapaxes(bb_im, 1, 2), per).astype(BF16)
    a_re = ab_re.reshape(n_cb, 1, per * C_STATE)
    a_im = ab_im.reshape(n_cb, 1, per * C_STATE)
    cre = _block_diag(jnp.swapaxes(c_re, 1, 2), per)
    cim = _block_diag(jnp.swapaxes(c_im, 1, 2), per)
    cmat = jnp.concatenate([cre, -cim], axis=1).astype(BF16)
    u = _s5_uproj(h2, w_in.astype(BF16))
    g = _s5_scan(u, bbre, bbim, a_re, a_im, cmat, d_skip[None, :], bsz, seq)
    return _s5_glu(g, h2, glu_w1.astype(BF16), glu_w2.astype(BF16), w_out.astype(BF16),
                   ln_g[None, :], ln_b[None, :])


def _moe_ln(h, r_coarse, rb_coarse, r_fine, rb_fine, w_gate, w_up, w_down, ln_g, ln_b,
            regroup=None):
    d = h.shape[1]
    rows = MOE_GROUPS + N_EXPERTS
    wr = jnp.concatenate([r_coarse.T, jnp.transpose(r_fine, (0, 2, 1)).reshape(N_EXPERTS, d),
                          jnp.zeros((32 - rows, d), F32)], axis=0)
    rb = jnp.concatenate([rb_coarse, rb_fine.reshape(N_EXPERTS), jnp.zeros((32 - rows,), F32)])
    return _moe(h, wr, rb[:, None], w_gate.astype(BF16), w_up.astype(BF16), w_down.astype(BF16),
                ln_g[None, :], ln_b[None, :], regroup)


def kernel(x, rel_bias, ab_w_in, gla_gate_w2, gla_gate_b, gla_norm_g, ab_w_out, s5_w_in, s5_lam_re, s5_lam_im, s5_log_dt, s5_b_re, s5_b_im, s5_c_re, s5_c_im, s5_d, s5_glu_w1, s5_glu_w2, s5_w_out, ln_mix_g, ln_mix_b, ln_ffn_g, ln_ffn_b, moe_r_coarse, moe_rb_coarse, moe_r_fine, moe_rb_fine, moe_w_gate, moe_w_up, moe_w_down):
    bsz, seq, d = x.shape
    h = x.reshape(bsz * seq, d)
    bias_tiles = _bias_tiles(rel_bias)
    time_major = False
    for layer in range(DEPTH):
        i = layer // 2
        assert time_major == (layer % 2 == 1)
        if layer % 2 == 0:
            h = _even_mixer_ln(h, bsz, seq, ab_w_in[i], bias_tiles, gla_gate_w2[i], gla_gate_b[i],
                               gla_norm_g[i], ab_w_out[i], ln_mix_g[layer], ln_mix_b[layer])
        else:
            h = _s5_mixer_ln(h, bsz, seq, s5_w_in[i], s5_lam_re[i], s5_lam_im[i], s5_log_dt[i],
                             s5_b_re[i], s5_b_im[i], s5_c_re[i], s5_c_im[i], s5_d[i],
                             s5_glu_w1[i], s5_glu_w2[i], s5_w_out[i],
                             ln_mix_g[layer], ln_mix_b[layer])
        want_time_major = layer + 1 < DEPTH and (layer + 1) % 2 == 1
        regroup = None
        if want_time_major != time_major:
            regroup = (seq, bsz) if time_major else (bsz, seq)
        h = _moe_ln(h, moe_r_coarse[layer], moe_rb_coarse[layer], moe_r_fine[layer],
                    moe_rb_fine[layer], moe_w_gate[layer], moe_w_up[layer], moe_w_down[layer],
                    ln_ffn_g[layer], ln_ffn_b[layer], regroup)
        time_major = want_time_major
    return h.reshape(bsz, seq, d).astype(x.dtype)
```

```python
import functools
import math

import numpy as np
import jax
import jax.numpy as jnp
from jax import lax
from jax.experimental import pallas as pl
from jax.experimental.pallas import tpu as pltpu

F32 = jnp.float32
BF16 = jnp.bfloat16
I32 = jnp.int32

D_MODEL = 1024
DEPTH = 2
DN_ALPHA = (2.0 * DEPTH) ** 0.25
LN_EPS = 1e-5
A_HEAD_DIM = 64
A_HEADS = 8
A_KV_HEADS = 2
IDX_HEADS = 4
IDX_DIM = 64
TOPK_MAX = 256
REL_BUCKETS = 32
REL_MAX_DIST = 128
B_HEADS = 4
B_VAL_DIM = 128
B_KEY_DIM = 64
GATE_RANK = 16
GATE_TAU = 16.0
GLA_CHUNK = 64
C_GROUP = 16
C_GROUPS = 64
C_STATE = 64
MOE_GROUPS = 4
MOE_EXPERTS = 4
MOE_HIDDEN = 512
N_EXPERTS = MOE_GROUPS * MOE_EXPERTS

LANE = 128
VMEM_LIMIT = 52 * 1024 * 1024
NEG_INF = float("-inf")
INT_MIN = -(2 ** 31)

_NT = (((1,), (1,)), ((), ()))
_TN = (((0,), (0,)), ((), ()))


def _dot(a, b):
    return jnp.dot(a, b, preferred_element_type=F32)


def _dot_nt(a, b):
    return lax.dot_general(a, b, _NT, preferred_element_type=F32)


def _dot_tn(a, b):
    return lax.dot_general(a, b, _TN, preferred_element_type=F32)


def _split3(x):
    hi = x.astype(BF16)
    r1 = x - hi.astype(F32)
    mid = r1.astype(BF16)
    lo = (r1 - mid.astype(F32)).astype(BF16)
    return hi, mid, lo


def _params(sem):
    return pltpu.CompilerParams(dimension_semantics=sem, vmem_limit_bytes=VMEM_LIMIT)


def _layer_norm(y, g, b):
    mu = jnp.mean(y, axis=-1, keepdims=True)
    d = y - mu
    var = jnp.mean(d * d, axis=-1, keepdims=True)
    return d * lax.rsqrt(var + LN_EPS) * g + b


def _sigmoid(x):
    return 1.0 / (1.0 + jnp.exp(-x))


def _bucket_thresholds():
    max_exact = REL_BUCKETS // 2
    nf = np.arange(max_exact, 4 * REL_MAX_DIST).astype(np.float32)
    large = max_exact + (np.log(nf / np.float32(max_exact))
                         / np.float32(math.log(REL_MAX_DIST / max_exact))
                         * np.float32(REL_BUCKETS - max_exact)).astype(np.int32)
    large = np.minimum(large, REL_BUCKETS - 1)
    return [int(nf[np.argmax(large >= max_exact + j)]) for j in range(1, REL_BUCKETS - max_exact)]


def _bias_kernel(rb_ref, out_ref):
    max_exact = REL_BUCKETS // 2
    srow = lax.broadcasted_iota(I32, (LANE, LANE), 0)
    tcol = lax.broadcasted_iota(I32, (LANE, LANE), 1)
    thr = _bucket_thresholds()
    for band in range(2):
        n = jnp.maximum(band * LANE + tcol - srow, 0)
        large = jnp.full((LANE, LANE), max_exact, I32)
        for t in thr:
            large = large + jnp.where(n >= t, 1, 0)
        bucket = jnp.where(n < max_exact, n, large)
        for h in range(A_HEADS):
            acc = jnp.zeros((LANE, LANE), F32)
            for bk in range(REL_BUCKETS):
                acc = acc + jnp.where(bucket == bk, rb_ref[bk, h], 0.0)
            out_ref[h, band] = acc
    for h in range(A_HEADS):
        out_ref[h, 2] = jnp.full((LANE, LANE), rb_ref[REL_BUCKETS - 1, h], F32)


def _bias_tiles(rel_bias):
    return pl.pallas_call(
        _bias_kernel,
        out_shape=jax.ShapeDtypeStruct((A_HEADS, 3, LANE, LANE), F32),
        in_specs=[pl.BlockSpec(memory_space=pltpu.SMEM)],
        name="rel_bias_tiles",
    )(rel_bias)


def _proj_kernel(x_ref, w_ref, *out_refs, widths, chunk):
    xb = x_ref[...].astype(BF16)
    c0 = 0
    for o_ref, width in zip(out_refs, widths):
        for c in range(0, width, chunk):
            ce = min(c + chunk, width)
            o_ref[:, c:ce] = _dot(xb, w_ref[:, c0 + c:c0 + ce]).astype(o_ref.dtype)
        c0 += width


def _in_proj(x2, w, widths, dtypes, tm=512, chunk=256):
    n, d = x2.shape
    return pl.pallas_call(
        functools.partial(_proj_kernel, widths=widths, chunk=chunk),
        out_shape=[jax.ShapeDtypeStruct((n, wd), dt) for wd, dt in zip(widths, dtypes)],
        grid=(n // tm,),
        in_specs=[pl.BlockSpec((tm, d), lambda i: (i, 0)),
                  pl.BlockSpec(w.shape, lambda i: (0, 0))],
        out_specs=[pl.BlockSpec((tm, wd), lambda i: (i, 0)) for wd in widths],
        compiler_params=_params(("arbitrary",)),
        name="in_proj",
    )(x2, w)


def _from_sortable(key):
    bits = jnp.where(key < 0, key ^ jnp.int32(0x7FFFFFFF), key)
    return lax.bitcast_convert_type(bits, F32)


def _dsa_kernel(q_ref, k_ref, v_ref, iq_ref, slab_ref, slabq_ref, bias_ref, o_ref,
                sc_ref, am_ref, vt_ref, acc_ref, lg_ref, tmax_ref, ot_ref, *, k_sel, seq):
    i = pl.program_id(1)
    t0 = i * LANE
    nkt = i + 1
    kt2 = 2 * LANE
    n2 = lax.shift_right_logical(nkt + 1, 1)
    rep = A_HEADS // A_KV_HEADS
    srow_2 = lax.broadcasted_iota(I32, (kt2, LANE), 0)
    tcol = t0 + lax.broadcasted_iota(I32, (kt2, LANE), 1)

    @pl.when(i == 0)
    def _():
        for jt in range(seq // kt2):
            vt_ref[jt] = v_ref[jt * kt2:(jt + 1) * kt2, :].astype(F32).T.astype(BF16)

    slab_t = slabq_ref[...].T
    w_t = slab_t[IDX_DIM:IDX_DIM + IDX_HEADS, :] * (IDX_HEADS ** -0.5)
    iqb = iq_ref[...].astype(BF16)
    iq_stack = jnp.concatenate([iqb[:, h * LANE:(h + 1) * LANE] for h in range(IDX_HEADS)], axis=0)

    def score_tile(j, carry):
        s0 = pl.multiple_of(j * kt2, kt2)
        ikt = slab_ref[pl.ds(s0, kt2), :].astype(BF16)
        d = _dot_nt(ikt, iq_stack) * (IDX_DIM ** -0.5)
        acc = jnp.zeros((kt2, LANE), F32)
        for h in range(IDX_HEADS):
            acc = acc + jnp.maximum(d[:, h * LANE:(h + 1) * LANE], 0.0) * w_t[h:h + 1, :]
        sc_ref[pl.ds(s0, kt2), :] = jnp.where(s0 + srow_2 <= tcol, acc, NEG_INF)
        return carry

    lax.fori_loop(0, n2, score_tile, 0)
    sc_ref[pl.ds(pl.multiple_of(nkt * LANE, LANE), LANE), :] = jnp.full((LANE, LANE), NEG_INF, F32)

    def count(pred_fn):
        def body(j, c):
            s0 = pl.multiple_of(j * kt2, kt2)
            st = sc_ref[pl.ds(s0, kt2), :]
            m = jnp.where(pred_fn(st, s0 + srow_2), 1.0, 0.0)
            parts = [m[r * 8:(r + 1) * 8] for r in range(kt2 // 8)]
            while len(parts) > 1:
                parts = [a + b for a, b in zip(parts[0::2], parts[1::2])]
            return c + parts[0]
        c = lax.fori_loop(0, n2, body, jnp.zeros((8, LANE), F32))
        return jnp.sum(c, axis=0, keepdims=True)

    def bcast(v):
        return jnp.broadcast_to(v, (kt2, LANE))

    def search():
        c0 = count(lambda st, s: st >= 0.0)
        key0 = jnp.where(c0 >= k_sel, 0, INT_MIN).astype(I32)

        def bit_body(bi, key):
            cand = key | lax.shift_left(jnp.int32(1), 30 - bi)
            cb = bcast(_from_sortable(cand))
            cnt = count(lambda st, s: st >= cb)
            return jnp.where(cnt >= k_sel, cand, key)

        ans = _from_sortable(lax.fori_loop(0, 31, bit_body, key0))
        ab = bcast(ans)
        cnt_ge = count(lambda st, s: st >= ab)
        cnt_gt = count(lambda st, s: st > ab)
        need = k_sel - cnt_gt

        def tie_search():
            def idx_body(bi, x):
                cand = x | lax.shift_left(jnp.int32(1), (seq.bit_length() - 2) - bi)
                cb = bcast(cand)
                f = count(lambda st, s: (st == ab) & (s < cb))
                return jnp.where(f < need, cand, x)
            return lax.fori_loop(0, seq.bit_length() - 1, idx_body, jnp.zeros((1, LANE), I32))

        cut = lax.cond(jnp.max(cnt_ge) > k_sel, tie_search,
                       lambda: jnp.full((1, LANE), seq - 1, I32))
        return ans, cut

    ans, cut = lax.cond(t0 >= k_sel, search,
                        lambda: (jnp.full((1, LANE), NEG_INF, F32),
                                 jnp.full((1, LANE), seq - 1, I32)))
    ans_b = bcast(ans)
    cut_b = bcast(cut)

    def mask_tile(j, carry):
        s0 = pl.multiple_of(j * kt2, kt2)
        st = sc_ref[pl.ds(s0, kt2), :]
        srow = s0 + srow_2
        sel = (st > ans_b) | ((st == ans_b) & (srow <= cut_b))
        am_ref[pl.ds(s0, kt2), :] = jnp.where(sel & (srow <= tcol), 0.0, NEG_INF)
        return carry

    lax.fori_loop(0, n2, mask_tile, 0)

    scale = jnp.asarray(A_HEAD_DIM ** -0.5, BF16)
    q_stacks = [jnp.concatenate([q_ref[:, (g * rep + r) * LANE:(g * rep + r + 1) * LANE]
                                 for r in range(rep)], axis=0) * scale for g in range(A_KV_HEADS)]
    acc_ref[...] = jnp.zeros(acc_ref.shape, F32)

    def logits_stage(j):
        slot = j & 1
        s0 = pl.multiple_of(j * kt2, kt2)
        am = am_ref[pl.ds(s0, kt2), :]
        band0 = jnp.clip(i - 2 * j, 0, 2)
        band1 = jnp.clip(i - 2 * j - 1, 0, 2)
        for g in range(A_KV_HEADS):
            kt = k_ref[pl.ds(s0, kt2), g * LANE:(g + 1) * LANE]
            lg_all = _dot_nt(kt, q_stacks[g])
            for r in range(rep):
                h = g * rep + r
                cs = slice(r * LANE, (r + 1) * LANE)
                bias = jnp.concatenate([bias_ref[h, band0], bias_ref[h, band1]], axis=0)
                lg = lg_all[:, cs] + bias + am
                lg_ref[slot, g, :, cs] = lg
                tmax_ref[slot, h:h + 1, :] = jnp.max(lg, axis=0, keepdims=True)

    def softmax_stage(j, carry):
        ms, ls = carry
        slot = j & 1
        new_ms, new_ls = [], []
        for g in range(A_KV_HEADS):
            ps, alphas = [], []
            for r in range(rep):
                h = g * rep + r
                m_new = jnp.maximum(ms[h], tmax_ref[slot, h:h + 1, :])
                m_safe = jnp.where(m_new == NEG_INF, 0.0, m_new)
                alpha = jnp.exp(ms[h] - m_safe)
                p = jnp.exp(lg_ref[slot, g, :, r * LANE:(r + 1) * LANE] - m_safe)
                new_ms.append(m_new)
                new_ls.append(alpha * ls[h] + jnp.sum(p, axis=0, keepdims=True))
                ps.append(p.astype(BF16))
                alphas.append(alpha)
            vt = vt_ref[j, g * A_HEAD_DIM:(g + 1) * A_HEAD_DIM, :]
            pv = _dot(vt, jnp.concatenate(ps, axis=1))
            acc_ref[g] = acc_ref[g] * jnp.concatenate(alphas, axis=1) + pv
        return tuple(new_ms), tuple(new_ls)

    def tile_body(j, carry):
        carry = softmax_stage(j, carry)
        logits_stage(j + 1)
        return carry

    init = (tuple(jnp.full((1, LANE), NEG_INF, F32) for _ in range(A_HEADS)),
            tuple(jnp.zeros((1, LANE), F32) for _ in range(A_HEADS)))
    logits_stage(0)
    carry = lax.fori_loop(0, n2 - 1, tile_body, init)
    _, ls = softmax_stage(n2 - 1, carry)
    for g in range(A_KV_HEADS):
        for r in range(rep):
            h = g * rep + r
            ot_ref[h * A_HEAD_DIM:(h + 1) * A_HEAD_DIM, :] = (
                acc_ref[g, :, r * LANE:(r + 1) * LANE] / ls[h])
    o_ref[...] = ot_ref[...].T.astype(o_ref.dtype)


def _dsa(h_a, h_f, bias_tiles, bsz, seq):
    n = bsz * seq
    nq = seq // LANE
    k_sel = min(TOPK_MAX, seq // 4)
    assert k_sel % LANE == 0 and seq % (2 * LANE) == 0
    qw = A_HEADS * LANE
    slab_blk = (h_f.shape[1] - LANE) // LANE
    return pl.pallas_call(
        functools.partial(_dsa_kernel, k_sel=k_sel, seq=seq),
        out_shape=jax.ShapeDtypeStruct((n, A_HEADS * A_HEAD_DIM), BF16),
        grid=(bsz, nq),
        in_specs=[
            pl.BlockSpec((LANE, qw), lambda b, i: (b * nq + i, 0)),
            pl.BlockSpec((seq, A_KV_HEADS * LANE), lambda b, i: (b, qw // (A_KV_HEADS * LANE))),
            pl.BlockSpec((seq, LANE), lambda b, i: (b, (qw + A_KV_HEADS * LANE) // LANE)),
            pl.BlockSpec((LANE, IDX_HEADS * LANE), lambda b, i: (b * nq + i, 0)),
            pl.BlockSpec((seq, LANE), lambda b, i: (b, slab_blk)),
            pl.BlockSpec((LANE, LANE), lambda b, i: (b * nq + i, slab_blk)),
            pl.BlockSpec((A_HEADS, 3, LANE, LANE), lambda b, i: (0, 0, 0, 0)),
        ],
        out_specs=pl.BlockSpec((LANE, A_HEADS * A_HEAD_DIM), lambda b, i: (b * nq + i, 0)),
        scratch_shapes=[
            pltpu.VMEM((seq + LANE, LANE), F32),
            pltpu.VMEM((seq, LANE), F32),
            pltpu.VMEM((nq // 2, LANE, 2 * LANE), BF16),
            pltpu.VMEM((A_KV_HEADS, A_HEAD_DIM, (A_HEADS // A_KV_HEADS) * LANE), F32),
            pltpu.VMEM((2, A_KV_HEADS, 2 * LANE, (A_HEADS // A_KV_HEADS) * LANE), F32),
            pltpu.VMEM((2, A_HEADS, LANE), F32),
            pltpu.VMEM((A_HEADS * A_HEAD_DIM, LANE), F32),
        ],
        compiler_params=_params(("arbitrary", "arbitrary")),
        name="dsa_attention",
    )(h_a, h_a, h_a, h_f, h_f, h_f, bias_tiles)


def _gla_kernel(bq_ref, bk_ref, bv_ref, br_ref, slab_ref, w2_ref, gb_ref, ng_ref, o_ref,
                st_ref, *, n_chunks):
    @pl.when(pl.program_id(1) == 0)
    def _():
        st_ref[...] = jnp.zeros(st_ref.shape, F32)

    ch = GLA_CHUNK
    rws = n_chunks * ch
    shift = ch.bit_length() - 1
    assert 1 << shift == ch
    row = lax.broadcasted_iota(I32, (rws, rws), 0)
    col = lax.broadcasted_iota(I32, (rws, rws), 1)
    tril = (lax.shift_right_logical(row, shift) == lax.shift_right_logical(col, shift)) & (row >= col)
    tri = jnp.where(tril, 1.0, 0.0).astype(BF16)
    gate = _dot(slab_ref[...].astype(BF16), w2_ref[...]) + gb_ref[...]
    log_a = (jnp.minimum(gate, 0.0) - jnp.log1p(jnp.exp(-jnp.abs(gate)))) / GATE_TAU
    hi, mid, lo = _split3(log_a)
    cum = _dot(tri, hi) + _dot(tri, mid) + _dot(tri, lo)
    lasts = [cum[(c + 1) * ch - 1:(c + 1) * ch, :] for c in range(n_chunks)]
    last = jnp.concatenate([jnp.broadcast_to(l, (ch, l.shape[1])) for l in lasts], axis=0)
    q = bq_ref[...] * (B_KEY_DIM ** -0.5)
    k = bk_ref[...]
    q_dec = (q * jnp.exp(cum)).astype(BF16)
    k_inv = (k * jnp.exp(-cum)).astype(BF16)
    k_end = (k * jnp.exp(last - cum)).astype(BF16)
    decays = [jnp.exp(l) for l in lasts]
    for h in range(B_HEADS):
        cs = slice(h * LANE, (h + 1) * LANE)
        v = bv_ref[:, cs].astype(BF16)
        sc = jnp.where(tril, _dot_nt(q_dec[:, cs], k_inv[:, cs]), 0.0)
        o_intra = _dot(sc.astype(BF16), v)
        st = st_ref[h]
        parts = []
        for c in range(n_chunks):
            rows = slice(c * ch, (c + 1) * ch)
            parts.append(o_intra[rows] + _dot_nt(q_dec[rows, cs], st.astype(BF16)))
            st = decays[c][:, cs] * st + _dot_tn(v[rows], k_end[rows, cs])
        st_ref[h] = st
        o = jnp.concatenate(parts, axis=0)
        mu = jnp.mean(o, axis=-1, keepdims=True)
        dlt = o - mu
        var = jnp.mean(dlt * dlt, axis=-1, keepdims=True)
        on = dlt * lax.rsqrt(var + LN_EPS) * ng_ref[:, cs]
        r = br_ref[:, cs]
        o_ref[:, cs] = (on * (r * _sigmoid(r))).astype(o_ref.dtype)


def _gla(h_f, w2p, gbp, norm_g, bsz, seq, rows_per_step=256):
    n = bsz * seq
    hw = B_HEADS * LANE
    steps = seq // rows_per_step
    slab_blk = (h_f.shape[1] - LANE) // LANE
    blk = lambda cb: pl.BlockSpec((rows_per_step, hw), lambda b, s, cb=cb: (b * steps + s, cb))
    return pl.pallas_call(
        functools.partial(_gla_kernel, n_chunks=rows_per_step // GLA_CHUNK),
        out_shape=jax.ShapeDtypeStruct((n, hw), BF16),
        grid=(bsz, steps),
        in_specs=[blk(1), blk(2), blk(3), blk(4),
                  pl.BlockSpec((rows_per_step, LANE), lambda b, s: (b * steps + s, slab_blk)),
                  pl.BlockSpec((LANE, hw), lambda b, s: (0, 0)),
                  pl.BlockSpec((1, hw), lambda b, s: (0, 0)),
                  pl.BlockSpec((1, hw), lambda b, s: (0, 0))],
        out_specs=pl.BlockSpec((rows_per_step, hw), lambda b, s: (b * steps + s, 0)),
        scratch_shapes=[pltpu.VMEM((B_HEADS, LANE, LANE), F32)],
        compiler_params=_params(("arbitrary", "arbitrary")),
        name="gla_attention",
    )(h_f, h_f, h_f, h_f, h_f, w2p, gbp, norm_g)


def _outproj_ln_kernel(oa_ref, ob_ref, x_ref, w_ref, g_ref, b_ref, out_ref):
    ka = oa_ref.shape[1]
    m = _dot(oa_ref[...], w_ref[0:ka, :]) + _dot(ob_ref[...], w_ref[ka:, :])
    out_ref[...] = _layer_norm(DN_ALPHA * x_ref[...] + m, g_ref[...], b_ref[...])


def _outproj_ln(o_a, o_b, x2, w, g, b, tm=512):
    n, d = x2.shape
    row = lambda width: pl.BlockSpec((tm, width), lambda i: (i, 0))
    full = lambda a: pl.BlockSpec(a.shape, lambda i: (0, 0))
    return pl.pallas_call(
        _outproj_ln_kernel,
        out_shape=jax.ShapeDtypeStruct((n, d), F32),
        grid=(n // tm,),
        in_specs=[row(o_a.shape[1]), row(o_b.shape[1]), row(d), full(w), full(g), full(b)],
        out_specs=row(d),
        compiler_params=_params(("arbitrary",)),
        name="out_proj_ln",
    )(o_a, o_b, x2, w, g, b)


def _route(lt):
    gl = [lt[g:g + 1, :] for g in range(MOE_GROUPS)]
    best, gsel = gl[0], jnp.zeros_like(gl[0], dtype=I32)
    for g in range(1, MOE_GROUPS):
        better = gl[g] > best
        gsel = jnp.where(better, g, gsel)
        best = jnp.where(better, gl[g], best)
    denom = sum(jnp.exp(x - best) for x in gl)
    g_w = 1.0 / denom
    fl = []
    for e in range(MOE_EXPERTS):
        acc = jnp.zeros_like(best)
        for g in range(MOE_GROUPS):
            r = MOE_GROUPS + g * MOE_EXPERTS + e
            acc = jnp.where(gsel == g, lt[r:r + 1, :], acc)
        fl.append(acc)
    v1, i1 = fl[0], jnp.zeros_like(gsel)
    for e in range(1, MOE_EXPERTS):
        better = fl[e] > v1
        i1 = jnp.where(better, e, i1)
        v1 = jnp.where(better, fl[e], v1)
    v2, i2 = jnp.full_like(v1, NEG_INF), jnp.zeros_like(gsel)
    for e in range(MOE_EXPERTS):
        better = (fl[e] > v2) & (i1 != e)
        i2 = jnp.where(better, e, i2)
        v2 = jnp.where(better, fl[e], v2)
    e2 = jnp.exp(v2 - v1)
    w1 = (1.0 / (1.0 + e2)) * g_w
    w2 = (e2 / (1.0 + e2)) * g_w
    e_gate = [jnp.where(i1 == e, w1, 0.0) + jnp.where(i2 == e, w2, 0.0)
              for e in range(MOE_EXPERTS)]
    return gsel, e_gate


def _moe_route_kernel(x_ref, wr_ref, rb_ref, xa_ref, rt_ref, cnt_ref, tri_ref, carry_ref):
    i = pl.program_id(0)
    tm, d = x_ref.shape

    @pl.when(i == 0)
    def _():
        carry_ref[...] = jnp.zeros(carry_ref.shape, F32)
        before = (lax.broadcasted_iota(I32, (tm, tm), 0) < lax.broadcasted_iota(I32, (tm, tm), 1))
        tri_ref[...] = jnp.where(before, 1.0, 0.0).astype(BF16)

    x = x_ref[...]
    xs = _split3(x)
    ws = _split3(wr_ref[...])
    lt = jnp.zeros((wr_ref.shape[0], tm), F32)
    for a, b in ((2, 0), (0, 2), (1, 1), (1, 0), (0, 1), (0, 0)):
        lt = lt + _dot_nt(ws[a], xs[b])
    lt = lt + rb_ref[...]
    gsel, e_gate = _route(lt)

    row8 = lax.broadcasted_iota(I32, (8, tm), 0)
    onehot = jnp.where(row8 == gsel, 1.0, 0.0)
    earlier = _dot(onehot.astype(BF16), tri_ref[...])
    rank = jnp.sum(onehot * (earlier + carry_ref[:, 0:1]), axis=0, keepdims=True)
    rt_ref[...] = jnp.where(row8 == 0, gsel, jnp.where(row8 == 1, rank.astype(I32), 0))
    carry_ref[...] = carry_ref[...] + jnp.sum(onehot, axis=1, keepdims=True)
    cnt_ref[...] = carry_ref[...].astype(I32)

    erow = lax.broadcasted_iota(I32, (LANE, tm), 0)
    gt = jnp.zeros((LANE, tm), F32)
    for e in range(MOE_EXPERTS):
        gt = jnp.where(erow == e, e_gate[e], gt)
    xa_ref[:, :d] = x
    xa_ref[:, d:] = gt.T


def _moe_route(h, wr, rb, tm=1024):
    n, d = h.shape
    tm = min(tm, n)
    full = lambda a: pl.BlockSpec(a.shape, lambda i: (0,) * a.ndim)
    return pl.pallas_call(
        _moe_route_kernel,
        out_shape=[jax.ShapeDtypeStruct((n, d + LANE), F32), jax.ShapeDtypeStruct((8, n), I32),
                   jax.ShapeDtypeStruct((8, LANE), I32)],
        grid=(n // tm,),
        in_specs=[pl.BlockSpec((tm, d), lambda i: (i, 0)), full(wr), full(rb)],
        out_specs=[pl.BlockSpec((tm, d + LANE), lambda i: (i, 0)),
                   pl.BlockSpec((8, tm), lambda i: (0, i)),
                   pl.BlockSpec((8, LANE), lambda i: (0, 0))],
        scratch_shapes=[pltpu.VMEM((tm, tm), BF16), pltpu.VMEM((8, LANE), F32)],
        compiler_params=_params(("arbitrary",)),
        name="moe_route",
    )(h, wr, rb)


def _row_copy_waves(n_rows, wave, start_row, wait_wave):
    n_waves = n_rows // wave
    for w in range(n_waves + 1):
        if w < n_waves:
            lax.fori_loop(w * wave, (w + 1) * wave, lambda r, c, w=w: (start_row(r, w % 2), c)[1], 0,
                          unroll=8)
        if w >= 1:
            wait_wave((w - 1) % 2)


def _moe_dispatch_kernel(pos_ref, fill_ref, xa_ref, xs_hbm, zero_ref, sem, *, wave):
    i = pl.program_id(0)
    tm = pos_ref.shape[1]
    tme = zero_ref.shape[0]

    def copy(r, s):
        return pltpu.make_async_copy(xa_ref.at[r], xs_hbm.at[pos_ref[0, r]], sem.at[s])

    def wait_wave(s):
        rows = pl.ds(0, wave)
        pltpu.make_async_copy(xa_ref.at[rows], xs_hbm.at[rows], sem.at[s]).wait()

    _row_copy_waves(tm, wave, lambda r, s: copy(r, s).start(), wait_wave)

    @pl.when(i == pl.num_programs(0) - 1)
    def _():
        zero_ref[...] = jnp.zeros(zero_ref.shape, F32)
        for g in range(MOE_GROUPS):
            first, count = fill_ref[0, g], fill_ref[1, g]

            def zcopy(r, first=first):
                return pltpu.make_async_copy(zero_ref.at[0], xs_hbm.at[first + r], sem.at[0])

            lax.fori_loop(0, count, lambda r, c: (zcopy(r).start(), c)[1], 0)
            lax.fori_loop(0, count, lambda r, c: (zcopy(r).wait(), c)[1], 0)
        n_tiles = fill_ref[2, 0]
        for k in range(MOE_GROUPS):
            @pl.when(k < fill_ref[2, 1])
            def _(k=k):
                tail = xs_hbm.at[pl.ds(pl.multiple_of((n_tiles + k) * tme, tme), tme)]
                cp = pltpu.make_async_copy(zero_ref, tail, sem.at[0])
                cp.start()
                cp.wait()


def _moe_dispatch(xa, pos, fill, n_sorted, tme, tm=1024, wave=256):
    n, da = xa.shape
    tm = min(tm, n)
    wave = min(wave, tm)
    return pl.pallas_call(
        functools.partial(_moe_dispatch_kernel, wave=wave),
        out_shape=jax.ShapeDtypeStruct((n_sorted, da), F32),
        grid=(n // tm,),
        in_specs=[pl.BlockSpec((1, tm), lambda i: (0, i), memory_space=pltpu.SMEM),
                  pl.BlockSpec(memory_space=pltpu.SMEM),
                  pl.BlockSpec((tm, da), lambda i: (i, 0))],
        out_specs=pl.BlockSpec(memory_space=pl.ANY),
        scratch_shapes=[pltpu.VMEM((tme, da), F32), pltpu.SemaphoreType.DMA((2,))],
        compiler_params=_params(("arbitrary",)),
        name="moe_dispatch",
    )(pos, fill, xa)


def _moe_expert_kernel(tg_ref, tj_ref, nt_ref, xs_ref, wg_ref, wu_ref, wd_ref, lg_ref, lb_ref,
                       ys_ref):
    d = ys_ref.shape[1]

    @pl.when(pl.program_id(0) < nt_ref[0])
    def _():
        x = xs_ref[:, :d]
        gates = xs_ref[:, d:]
        xb = x.astype(BF16)
        y = jnp.zeros(x.shape, F32)
        for e in range(MOE_EXPERTS):
            hg = _dot(xb, wg_ref[e])
            hu = _dot(xb, wu_ref[e])
            hid = hg * _sigmoid(hg) * hu * gates[:, e:e + 1]
            y = y + _dot(hid.astype(BF16), wd_ref[e])
        ys_ref[...] = _layer_norm(DN_ALPHA * x + y, lg_ref[...], lb_ref[...])

    @pl.when(pl.program_id(0) >= nt_ref[0])
    def _():
        ys_ref[...] = jnp.zeros(ys_ref.shape, F32)


def _moe_experts(xs, tile_g, tile_j, n_tiles, wg, wu, wd, lg, lb, tme):
    p, da = xs.shape
    d = da - LANE
    grp = lambda w: pl.BlockSpec((None,) + w.shape[1:], lambda i, tg, tj, nt: (tg[i], 0, 0, 0))
    full = lambda a: pl.BlockSpec(a.shape, lambda i, tg, tj, nt: (0,) * a.ndim)
    grid_spec = pltpu.PrefetchScalarGridSpec(
        num_scalar_prefetch=3,
        grid=(p // tme,),
        in_specs=[pl.BlockSpec((tme, da), lambda i, tg, tj, nt: (tj[i], 0)),
                  grp(wg), grp(wu), grp(wd), full(lg), full(lb)],
        out_specs=pl.BlockSpec((tme, d), lambda i, tg, tj, nt: (i, 0)),
    )
    return pl.pallas_call(
        _moe_expert_kernel,
        out_shape=jax.ShapeDtypeStruct((p, d), F32),
        grid_spec=grid_spec,
        compiler_params=_params(("arbitrary",)),
        name="moe_experts",
    )(tile_g, tile_j, n_tiles, xs, wg, wu, wd, lg, lb)


def _moe_combine_kernel(pos_ref, ys_hbm, o_ref, sem, *, wave):
    tm = o_ref.shape[0]

    def copy(r, s):
        return pltpu.make_async_copy(ys_hbm.at[pos_ref[0, r]], o_ref.at[r], sem.at[s])

    def wait_wave(s):
        rows = pl.ds(0, wave)
        pltpu.make_async_copy(ys_hbm.at[rows], o_ref.at[rows], sem.at[s]).wait()

    _row_copy_waves(tm, wave, lambda r, s: copy(r, s).start(), wait_wave)


def _moe_combine(ys, pos, n, tm=1024, wave=256):
    d = ys.shape[1]
    tm = min(tm, n)
    wave = min(wave, tm)
    return pl.pallas_call(
        functools.partial(_moe_combine_kernel, wave=wave),
        out_shape=jax.ShapeDtypeStruct((n, d), F32),
        grid=(n // tm,),
        in_specs=[pl.BlockSpec((1, tm), lambda i: (0, i), memory_space=pltpu.SMEM),
                  pl.BlockSpec(memory_space=pl.ANY)],
        out_specs=pl.BlockSpec((tm, d), lambda i: (i, 0)),
        scratch_shapes=[pltpu.SemaphoreType.DMA((2,))],
        compiler_params=_params(("arbitrary",)),
        name="moe_combine",
    )(pos, ys)


def _moe(h, wr, rb, wg, wu, wd, lg, lb, regroup=None, tme=512):
    n, d = h.shape
    tme = min(tme, n)
    xa, rt, cnt = _moe_route(h, wr, rb)
    counts = cnt[:MOE_GROUPS, 0]
    tiles = (counts + tme - 1) // tme
    first_tile = jnp.cumsum(tiles) - tiles
    gid, rank = rt[0], rt[1]
    pos = (first_tile[gid] * tme + rank)[None, :]
    n_slots = n // tme + MOE_GROUPS
    n_tiles = jnp.sum(tiles)
    fill = jnp.stack([first_tile * tme + counts, tiles * tme - counts,
                      jnp.zeros((MOE_GROUPS,), I32).at[0].set(n_tiles).at[1].set(n_slots - n_tiles)])
    slot = jnp.minimum(jnp.arange(n_slots, dtype=I32), n_tiles - 1)
    tile_g = jnp.sum(slot[:, None] >= (first_tile + tiles)[None, :], axis=1).astype(I32)
    xs = _moe_dispatch(xa, pos, fill, n_slots * tme, tme)
    ys = _moe_experts(xs, tile_g, slot, n_tiles[None].astype(I32), wg, wu, wd, lg, lb, tme)
    if regroup is not None:
        pos = pos.reshape(regroup).T.reshape(1, n)
    return _moe_combine(ys, pos, n)


def _s5_prep_kernel(lre_ref, lim_ref, ldt_ref, bre_ref, bim_ref, are_ref, aim_ref, bbre_ref,
                    bbim_ref):
    lr = jnp.minimum(lre_ref[...], -1e-4)
    li = lim_ref[...]
    dt = jnp.exp(ldt_ref[...])
    mag = jnp.exp(lr * dt)
    ab_re = mag * jnp.cos(li * dt)
    ab_im = mag * jnp.sin(li * dt)
    den = lr * lr + li * li
    nr = ab_re - 1.0
    coef_re = (nr * lr + ab_im * li) / den
    coef_im = (ab_im * lr - nr * li) / den
    are_ref[...] = ab_re
    aim_ref[...] = ab_im
    bbre_ref[...] = coef_re * bre_ref[...] - coef_im * bim_ref[...]
    bbim_ref[...] = coef_re * bim_ref[...] + coef_im * bre_ref[...]


def _s5_prep(lam_re, lam_im, log_dt, b_re, b_im):
    gp = C_GROUPS * C_STATE
    col = lambda a: a.reshape(gp, 1)
    ldt = jnp.broadcast_to(log_dt[:, None], (C_GROUPS, C_STATE))
    outs = pl.pallas_call(
        _s5_prep_kernel,
        out_shape=[jax.ShapeDtypeStruct((gp, 1), F32), jax.ShapeDtypeStruct((gp, 1), F32),
                   jax.ShapeDtypeStruct((gp, C_GROUP), F32), jax.ShapeDtypeStruct((gp, C_GROUP), F32)],
        name="s5_discretise",
    )(col(lam_re), col(lam_im), col(ldt), b_re.reshape(gp, C_GROUP), b_im.reshape(gp, C_GROUP))
    ab_re, ab_im, bb_re, bb_im = outs
    return (ab_re.reshape(C_GROUPS, C_STATE), ab_im.reshape(C_GROUPS, C_STATE),
            bb_re.reshape(C_GROUPS, C_STATE, C_GROUP), bb_im.reshape(C_GROUPS, C_STATE, C_GROUP))


def _s5_uproj_kernel(x_ref, w_ref, o_ref):
    o_ref[...] = _dot(x_ref[...].astype(BF16), w_ref[...])


def _s5_uproj(h2, w, tm=512):
    n, d = h2.shape
    return pl.pallas_call(
        _s5_uproj_kernel,
        out_shape=jax.ShapeDtypeStruct((n, d), F32),
        grid=(n // tm,),
        in_specs=[pl.BlockSpec((tm, d), lambda i: (i, 0)), pl.BlockSpec(w.shape, lambda i: (0, 0))],
        out_specs=pl.BlockSpec((tm, d), lambda i: (i, 0)),
        compiler_params=_params(("arbitrary",)),
        name="s5_in_proj",
    )(h2, w)


def _gelu_tanh(x):
    return 0.5 * x * (1.0 + jnp.tanh(math.sqrt(2.0 / math.pi) * (x + 0.044715 * (x * x * x))))


def _s5_scan_kernel(u_ref, bbre_ref, bbim_ref, are_ref, aim_ref, cm_ref, d_ref, g_ref,
                    st_ref, bure_ref, buim_ref, s_ref, *, bsz, tl, n_cb, sub):
    @pl.when(pl.program_id(0) == 0)
    def _():
        st_ref[...] = jnp.zeros(st_ref.shape, F32)

    cw = D_MODEL // n_cb
    sw = bure_ref.shape[2]

    def input_states(cb):
        ub = u_ref[:, cb * cw:(cb + 1) * cw].astype(BF16)
        bure_ref[cb % 2] = _dot(ub, bbre_ref[cb])
        buim_ref[cb % 2] = _dot(ub, bbim_ref[cb])

    input_states(0)
    for cb in range(n_cb):
        buf = cb % 2
        if cb + 1 < n_cb:
            input_states(cb + 1)
        for s0 in range(0, sw, sub):
            cs = slice(s0, s0 + sub)
            a_re = jnp.broadcast_to(are_ref[cb, :, cs], (bsz, sub))
            a_im = jnp.broadcast_to(aim_ref[cb, :, cs], (bsz, sub))
            s_re, s_im = st_ref[cb, 0, :, cs], st_ref[cb, 1, :, cs]
            for t in range(tl):
                rows = slice(t * bsz, (t + 1) * bsz)
                s_re, s_im = (a_re * s_re - a_im * s_im + bure_ref[buf, rows, cs],
                              a_re * s_im + a_im * s_re + buim_ref[buf, rows, cs])
                s_ref[buf, rows, cs] = s_re.astype(BF16)
                s_ref[buf, rows, sw + s0:sw + s0 + sub] = s_im.astype(BF16)
            st_ref[cb, 0, :, cs] = s_re
            st_ref[cb, 1, :, cs] = s_im
        ch = slice(cb * cw, (cb + 1) * cw)
        y = _dot(s_ref[buf], cm_ref[cb]) + d_ref[:, ch] * u_ref[:, ch]
        g_ref[:, ch] = _gelu_tanh(y).astype(g_ref.dtype)


def _s5_scan(u_lb, bbre, bbim, a_re, a_im, cmat, d_skip, bsz, seq, tl=32, sub=512):
    n, d = u_lb.shape
    n_cb, cw, sw = bbre.shape
    rows = tl * bsz
    full = lambda a: pl.BlockSpec(a.shape, lambda t: (0,) * a.ndim)
    return pl.pallas_call(
        functools.partial(_s5_scan_kernel, bsz=bsz, tl=tl, n_cb=n_cb, sub=sub),
        out_shape=jax.ShapeDtypeStruct((n, d), BF16),
        grid=(seq // tl,),
        in_specs=[pl.BlockSpec((rows, d), lambda t: (t, 0)), full(bbre), full(bbim), full(a_re),
                  full(a_im), full(cmat), full(d_skip)],
        out_specs=pl.BlockSpec((rows, d), lambda t: (t, 0)),
        scratch_shapes=[pltpu.VMEM((n_cb, 2, bsz, sw), F32), pltpu.VMEM((2, rows, sw), F32),
                        pltpu.VMEM((2, rows, sw), F32), pltpu.VMEM((2, rows, 2 * sw), BF16)],
        compiler_params=_params(("arbitrary",)),
        name="s5_scan",
    )(u_lb, bbre, bbim, a_re, a_im, cmat, d_skip)


def _s5_glu_kernel(g_ref, x_ref, w1_ref, w2_ref, wo_ref, lg_ref, lb_ref, out_ref):
    g = g_ref[...]
    z = _dot(g, w1_ref[...]) * _sigmoid(_dot(g, w2_ref[...]))
    m = _dot(z.astype(BF16), wo_ref[...])
    out_ref[...] = _layer_norm(DN_ALPHA * x_ref[...] + m, lg_ref[...], lb_ref[...])


def _s5_glu(g, h2, w1, w2, wo, lg, lb, tm=512):
    n, d = h2.shape
    row = pl.BlockSpec((tm, d), lambda i: (i, 0))
    full = lambda a: pl.BlockSpec(a.shape, lambda i: (0, 0))
    return pl.pallas_call(
        _s5_glu_kernel,
        out_shape=jax.ShapeDtypeStruct((n, d), F32),
        grid=(n // tm,),
        in_specs=[row, row, full(w1), full(w2), full(wo), full(lg), full(lb)],
        out_specs=row,
        compiler_params=_params(("arbitrary",)),
        name="s5_glu_out_ln",
    )(g, h2, w1, w2, wo, lg, lb)


def _pad_heads(w, heads, dim):
    d = w.shape[0]
    w = w.reshape(d, heads, dim)
    return jnp.pad(w, ((0, 0), (0, 0), (0, LANE - dim))).reshape(d, heads * LANE)


def _even_layer_weights(w_in, gate_w2, gate_b):
    splits = (A_HEADS * A_HEAD_DIM, A_KV_HEADS * A_HEAD_DIM, A_KV_HEADS * A_HEAD_DIM,
              IDX_HEADS * IDX_DIM, IDX_DIM, IDX_HEADS,
              B_HEADS * B_KEY_DIM, B_HEADS * B_KEY_DIM, B_HEADS * B_VAL_DIM, GATE_RANK,
              B_HEADS * B_VAL_DIM)
    offs = np.cumsum((0,) + splits)
    aq, ak, av, iq, ik, iw, bq, bk, bv, bg, br = [w_in[:, offs[k]:offs[k + 1]] for k in range(11)]
    d = w_in.shape[0]
    slab = jnp.concatenate(
        [ik, iw, bg, jnp.zeros((d, LANE - IDX_DIM - IDX_HEADS - GATE_RANK), w_in.dtype)], axis=1)
    w_a = jnp.concatenate([_pad_heads(aq, A_HEADS, A_HEAD_DIM),
                           _pad_heads(ak, A_KV_HEADS, A_HEAD_DIM), av], axis=1)
    w_f = jnp.concatenate([_pad_heads(iq, IDX_HEADS, IDX_DIM), _pad_heads(bq, B_HEADS, B_KEY_DIM),
                           _pad_heads(bk, B_HEADS, B_KEY_DIM), bv, br, slab], axis=1)
    w_all = jnp.concatenate([w_a, w_f], axis=1).astype(BF16)
    g0 = IDX_DIM + IDX_HEADS
    w2p = jnp.zeros((LANE, B_HEADS * LANE), F32).at[g0:g0 + GATE_RANK].set(
        _pad_heads(gate_w2, B_HEADS, B_KEY_DIM)).astype(BF16)
    gbp = _pad_heads(gate_b[None, :], B_HEADS, B_KEY_DIM)
    return w_all, (w_a.shape[1], w_f.shape[1]), w2p, gbp


def _even_mixer_ln(x2, bsz, seq, w_in, bias_tiles, gate_w2, gate_b, norm_g, w_out, ln_g, ln_b):
    w_all, widths, w2p, gbp = _even_layer_weights(w_in, gate_w2, gate_b)
    h_a, h_f = _in_proj(x2, w_all, widths, (BF16, F32))
    o_a = _dsa(h_a, h_f, bias_tiles, bsz, seq)
    o_b = _gla(h_f, w2p, gbp, norm_g[None, :], bsz, seq)
    return _outproj_ln(o_a, o_b, x2, w_out.astype(BF16), ln_g[None, :], ln_b[None, :])


def _block_diag(m, per):
    g, r, c = m.shape
    eye = jnp.eye(per, dtype=m.dtype)
    m = m.reshape(g // per, per, r, c)
    bd = m[:, :, :, None, :] * eye[None, :, None, :, None]
    return bd.reshape(g // per, per * r, per * c)


def _s5_mixer_ln(h2, bsz, seq, w_in, lam_re, lam_im, log_dt, b_re, b_im, c_re, c_im, d_skip,
                 glu_w1, glu_w2, w_out, ln_g, ln_b, groups_per_block=16):
    ab_re, ab_im, bb_re, bb_im = _s5_prep(lam_re, lam_im, log_dt, b_re, b_im)
    per = groups_per_block
    n_cb = C_GROUPS // per
    bbre = _block_diag(jnp.swapaxes(bb_re, 1, 2), per).astype(BF16)
    bbim = _block_diag(jnp.swapaxes(bb_im, 1, 2), per).astype(BF16)
    a_re = ab_re.reshape(n_cb, 1, per * C_STATE)
    a_im = ab_im.reshape(n_cb, 1, per * C_STATE)
    cre = _block_diag(jnp.swapaxes(c_re, 1, 2), per)
    cim = _block_diag(jnp.swapaxes(c_im, 1, 2), per)
    cmat = jnp.concatenate([cre, -cim], axis=1).astype(BF16)
    u = _s5_uproj(h2, w_in.astype(BF16))
    g = _s5_scan(u, bbre, bbim, a_re, a_im, cmat, d_skip[None, :], bsz, seq)
    return _s5_glu(g, h2, glu_w1.astype(BF16), glu_w2.astype(BF16), w_out.astype(BF16),
                   ln_g[None, :], ln_b[None, :])


def _moe_ln(h, r_coarse, rb_coarse, r_fine, rb_fine, w_gate, w_up, w_down, ln_g, ln_b,
            regroup=None):
    d = h.shape[1]
    rows = MOE_GROUPS + N_EXPERTS
    wr = jnp.concatenate([r_coarse.T, jnp.transpose(r_fine, (0, 2, 1)).reshape(N_EXPERTS, d),
                          jnp.zeros((32 - rows, d), F32)], axis=0)
    rb = jnp.concatenate([rb_coarse, rb_fine.reshape(N_EXPERTS), jnp.zeros((32 - rows,), F32)])
    return _moe(h, wr, rb[:, None], w_gate.astype(BF16), w_up.astype(BF16), w_down.astype(BF16),
                ln_g[None, :], ln_b[None, :], regroup)


def kernel(x, rel_bias, ab_w_in, gla_gate_w2, gla_gate_b, gla_norm_g, ab_w_out, s5_w_in, s5_lam_re, s5_lam_im, s5_log_dt, s5_b_re, s5_b_im, s5_c_re, s5_c_im, s5_d, s5_glu_w1, s5_glu_w2, s5_w_out, ln_mix_g, ln_mix_b, ln_ffn_g, ln_ffn_b, moe_r_coarse, moe_rb_coarse, moe_r_fine, moe_rb_fine, moe_w_gate, moe_w_up, moe_w_down):
    bsz, seq, d = x.shape
    h = x.reshape(bsz * seq, d)
    bias_tiles = _bias_tiles(rel_bias)
    time_major = False
    for layer in range(DEPTH):
        i = layer // 2
        assert time_major == (layer % 2 == 1)
        if layer % 2 == 0:
            h = _even_mixer_ln(h, bsz, seq, ab_w_in[i], bias_tiles, gla_gate_w2[i], gla_gate_b[i],
                               gla_norm_g[i], ab_w_out[i], ln_mix_g[layer], ln_mix_b[layer])
        else:
            h = _s5_mixer_ln(h, bsz, seq, s5_w_in[i], s5_lam_re[i], s5_lam_im[i], s5_log_dt[i],
                             s5_b_re[i], s5_b_im[i], s5_c_re[i], s5_c_im[i], s5_d[i],
                             s5_glu_w1[i], s5_glu_w2[i], s5_w_out[i],
                             ln_mix_g[layer], ln_mix_b[layer])
        want_time_major = layer + 1 < DEPTH and (layer + 1) % 2 == 1
        regroup = None
        if want_time_major != time_major:
            regroup = (seq, bsz) if time_major else (bsz, seq)
        h = _moe_ln(h, moe_r_coarse[layer], moe_rb_coarse[layer], moe_r_fine[layer],
                    moe_rb_fine[layer], moe_w_gate[layer], moe_w_up[layer], moe_w_down[layer],
                    ln_ffn_g[layer], ln_ffn_b[layer], regroup)
        time_major = want_time_major
    return h.reshape(bsz, seq, d).astype(x.dtype)
```

```python
import functools
import math

import numpy as np
import jax
import jax.numpy as jnp
from jax import lax
from jax.experimental import pallas as pl
from jax.experimental.pallas import tpu as pltpu

F32 = jnp.float32
BF16 = jnp.bfloat16
I32 = jnp.int32

D_MODEL = 1024
DEPTH = 2
DN_ALPHA = (2.0 * DEPTH) ** 0.25
LN_EPS = 1e-5
A_HEAD_DIM = 64
A_HEADS = 8
A_KV_HEADS = 2
IDX_HEADS = 4
IDX_DIM = 64
TOPK_MAX = 256
REL_BUCKETS = 32
REL_MAX_DIST = 128
B_HEADS = 4
B_VAL_DIM = 128
B_KEY_DIM = 64
GATE_RANK = 16
GATE_TAU = 16.0
GLA_CHUNK = 64
C_GROUP = 16
C_GROUPS = 64
C_STATE = 64
MOE_GROUPS = 4
MOE_EXPERTS = 4
MOE_HIDDEN = 512
N_EXPERTS = MOE_GROUPS * MOE_EXPERTS

LANE = 128
VMEM_LIMIT = 52 * 1024 * 1024
MOE_VMEM_LIMIT = 58 * 1024 * 1024
NEG_INF = float("-inf")
INT_MIN = -(2 ** 31)

_NT = (((1,), (1,)), ((), ()))
_TN = (((0,), (0,)), ((), ()))


def _dot(a, b):
    return jnp.dot(a, b, preferred_element_type=F32)


def _dot_nt(a, b):
    return lax.dot_general(a, b, _NT, preferred_element_type=F32)


def _dot_tn(a, b):
    return lax.dot_general(a, b, _TN, preferred_element_type=F32)


def _split3(x):
    hi = x.astype(BF16)
    r1 = x - hi.astype(F32)
    mid = r1.astype(BF16)
    lo = (r1 - mid.astype(F32)).astype(BF16)
    return hi, mid, lo


def _params(sem):
    return pltpu.CompilerParams(dimension_semantics=sem, vmem_limit_bytes=VMEM_LIMIT)


def _layer_norm(y, g, b):
    mu = jnp.mean(y, axis=-1, keepdims=True)
    d = y - mu
    var = jnp.mean(d * d, axis=-1, keepdims=True)
    return d * lax.rsqrt(var + LN_EPS) * g + b


def _sigmoid(x):
    return 1.0 / (1.0 + jnp.exp(-x))


def _bucket_thresholds():
    max_exact = REL_BUCKETS // 2
    nf = np.arange(max_exact, 4 * REL_MAX_DIST).astype(np.float32)
    large = max_exact + (np.log(nf / np.float32(max_exact))
                         / np.float32(math.log(REL_MAX_DIST / max_exact))
                         * np.float32(REL_BUCKETS - max_exact)).astype(np.int32)
    large = np.minimum(large, REL_BUCKETS - 1)
    return [int(nf[np.argmax(large >= max_exact + j)]) for j in range(1, REL_BUCKETS - max_exact)]


def _bias_kernel(rb_ref, out_ref):
    max_exact = REL_BUCKETS // 2
    srow = lax.broadcasted_iota(I32, (LANE, LANE), 0)
    tcol = lax.broadcasted_iota(I32, (LANE, LANE), 1)
    thr = _bucket_thresholds()
    for band in range(2):
        n = jnp.maximum(band * LANE + tcol - srow, 0)
        large = jnp.full((LANE, LANE), max_exact, I32)
        for t in thr:
            large = large + jnp.where(n >= t, 1, 0)
        bucket = jnp.where(n < max_exact, n, large)
        for h in range(A_HEADS):
            acc = jnp.zeros((LANE, LANE), F32)
            for bk in range(REL_BUCKETS):
                acc = acc + jnp.where(bucket == bk, rb_ref[bk, h], 0.0)
            out_ref[h, band] = acc
    for h in range(A_HEADS):
        out_ref[h, 2] = jnp.full((LANE, LANE), rb_ref[REL_BUCKETS - 1, h], F32)


def _bias_tiles(rel_bias):
    return pl.pallas_call(
        _bias_kernel,
        out_shape=jax.ShapeDtypeStruct((A_HEADS, 3, LANE, LANE), F32),
        in_specs=[pl.BlockSpec(memory_space=pltpu.SMEM)],
        name="rel_bias_tiles",
    )(rel_bias)


def _proj_kernel(x_ref, w_ref, *out_refs, widths, chunk):
    xb = x_ref[...].astype(BF16)
    c0 = 0
    for o_ref, width in zip(out_refs, widths):
        for c in range(0, width, chunk):
            ce = min(c + chunk, width)
            o_ref[:, c:ce] = _dot(xb, w_ref[:, c0 + c:c0 + ce]).astype(o_ref.dtype)
        c0 += width


def _in_proj(x2, w, widths, dtypes, tm=512, chunk=256):
    n, d = x2.shape
    return pl.pallas_call(
        functools.partial(_proj_kernel, widths=widths, chunk=chunk),
        out_shape=[jax.ShapeDtypeStruct((n, wd), dt) for wd, dt in zip(widths, dtypes)],
        grid=(n // tm,),
        in_specs=[pl.BlockSpec((tm, d), lambda i: (i, 0)),
                  pl.BlockSpec(w.shape, lambda i: (0, 0))],
        out_specs=[pl.BlockSpec((tm, wd), lambda i: (i, 0)) for wd in widths],
        compiler_params=_params(("arbitrary",)),
        name="in_proj",
    )(x2, w)


def _sortable(x):
    bits = lax.bitcast_convert_type(x, I32)
    return jnp.where(bits < 0, bits ^ jnp.int32(0x7FFFFFFF), bits)


def _dsa_kernel(q_ref, k_ref, v_ref, iq_ref, slab_ref, slabq_ref, bias_ref, o_ref,
                key_ref, am_ref, vt_ref, acc_ref, lg_ref, tmax_ref, ot_ref, *, k_sel, seq):
    i = pl.program_id(1)
    t0 = i * LANE
    nkt = i + 1
    kt2 = 2 * LANE
    n2 = lax.shift_right_logical(nkt + 1, 1)
    rep = A_HEADS // A_KV_HEADS
    srow_2 = lax.broadcasted_iota(I32, (kt2, LANE), 0)
    tcol = t0 + lax.broadcasted_iota(I32, (kt2, LANE), 1)

    @pl.when(i == 0)
    def _():
        for jt in range(seq // kt2):
            vt_ref[jt] = v_ref[jt * kt2:(jt + 1) * kt2, :].astype(F32).T.astype(BF16)

    slab_t = slabq_ref[...].T
    w_t = slab_t[IDX_DIM:IDX_DIM + IDX_HEADS, :] * (IDX_HEADS ** -0.5)
    iqb = iq_ref[...].astype(BF16)
    iq_stack = jnp.concatenate([iqb[:, h * LANE:(h + 1) * LANE] for h in range(IDX_HEADS)], axis=0)

    def score_tile(j, carry):
        s0 = pl.multiple_of(j * kt2, kt2)
        ikt = slab_ref[pl.ds(s0, kt2), :].astype(BF16)
        d = _dot_nt(ikt, iq_stack) * (IDX_DIM ** -0.5)
        acc = jnp.zeros((kt2, LANE), F32)
        for h in range(IDX_HEADS):
            acc = acc + jnp.maximum(d[:, h * LANE:(h + 1) * LANE], 0.0) * w_t[h:h + 1, :]
        acc = jnp.where(s0 + srow_2 <= tcol, acc, NEG_INF)
        key_ref[pl.ds(s0, kt2), :] = _sortable(acc)
        return carry

    lax.fori_loop(0, n2, score_tile, 0)
    key_ref[pl.ds(pl.multiple_of(nkt * LANE, LANE), LANE), :] = jnp.full((LANE, LANE), INT_MIN, I32)

    def count(pred_fn):
        def body(j, c):
            s0 = pl.multiple_of(j * kt2, kt2)
            kt = key_ref[pl.ds(s0, kt2), :]
            m = jnp.where(pred_fn(kt, s0 + srow_2), 1, 0)
            return c + jnp.sum(m.reshape(kt2 // 8, 8, LANE), axis=0)
        c = lax.fori_loop(0, n2, body, jnp.zeros((8, LANE), I32))
        return jnp.sum(c, axis=0, keepdims=True)

    def bcast(v):
        return jnp.broadcast_to(v, (kt2, LANE))

    def search():
        c0 = count(lambda kt, s: kt >= 0)
        ans0 = jnp.where(c0 >= k_sel, 0, INT_MIN).astype(I32)

        def bit_body(bi, ans):
            cand = ans | lax.shift_left(jnp.int32(1), 30 - bi)
            cb = bcast(cand)
            cnt = count(lambda kt, s: kt >= cb)
            return jnp.where(cnt >= k_sel, cand, ans)

        ans = lax.fori_loop(0, 31, bit_body, ans0)
        ab = bcast(ans)
        cnt_ge = count(lambda kt, s: kt >= ab)
        cnt_gt = count(lambda kt, s: kt > ab)
        need = k_sel - cnt_gt

        def tie_search():
            def idx_body(bi, x):
                cand = x | lax.shift_left(jnp.int32(1), (seq.bit_length() - 2) - bi)
                cb = bcast(cand)
                f = count(lambda kt, s: (kt == ab) & (s < cb))
                return jnp.where(f < need, cand, x)
            return lax.fori_loop(0, seq.bit_length() - 1, idx_body, jnp.zeros((1, LANE), I32))

        cut = lax.cond(jnp.max(cnt_ge) > k_sel, tie_search,
                       lambda: jnp.full((1, LANE), seq - 1, I32))
        return ans, cut

    ans, cut = lax.cond(t0 >= k_sel, search,
                        lambda: (jnp.full((1, LANE), INT_MIN, I32),
                                 jnp.full((1, LANE), seq - 1, I32)))
    ans_b = bcast(ans)
    cut_b = bcast(cut)

    def mask_tile(j, carry):
        s0 = pl.multiple_of(j * kt2, kt2)
        kt = key_ref[pl.ds(s0, kt2), :]
        srow = s0 + srow_2
        sel = (kt > ans_b) | ((kt == ans_b) & (srow <= cut_b))
        am_ref[pl.ds(s0, kt2), :] = jnp.where(sel & (srow <= tcol), 0.0, NEG_INF)
        return carry

    lax.fori_loop(0, n2, mask_tile, 0)

    scale = jnp.asarray(A_HEAD_DIM ** -0.5, BF16)
    q_stacks = [jnp.concatenate([q_ref[:, (g * rep + r) * LANE:(g * rep + r + 1) * LANE]
                                 for r in range(rep)], axis=0) * scale for g in range(A_KV_HEADS)]
    acc_ref[...] = jnp.zeros(acc_ref.shape, F32)

    def logits_stage(j):
        slot = j & 1
        s0 = pl.multiple_of(j * kt2, kt2)
        am = am_ref[pl.ds(s0, kt2), :]
        band0 = jnp.clip(i - 2 * j, 0, 2)
        band1 = jnp.clip(i - 2 * j - 1, 0, 2)
        for g in range(A_KV_HEADS):
            kt = k_ref[pl.ds(s0, kt2), g * LANE:(g + 1) * LANE]
            lg_all = _dot_nt(kt, q_stacks[g])
            for r in range(rep):
                h = g * rep + r
                cs = slice(r * LANE, (r + 1) * LANE)
                bias = jnp.concatenate([bias_ref[h, band0], bias_ref[h, band1]], axis=0)
                lg = lg_all[:, cs] + bias + am
                lg_ref[slot, g, :, cs] = lg
                tmax_ref[slot, h:h + 1, :] = jnp.max(lg, axis=0, keepdims=True)

    def softmax_stage(j, carry):
        ms, ls = carry
        slot = j & 1
        new_ms, new_ls = [], []
        for g in range(A_KV_HEADS):
            ps, alphas = [], []
            for r in range(rep):
                h = g * rep + r
                m_new = jnp.maximum(ms[h], tmax_ref[slot, h:h + 1, :])
                m_safe = jnp.where(m_new == NEG_INF, 0.0, m_new)
                alpha = jnp.exp(ms[h] - m_safe)
                p = jnp.exp(lg_ref[slot, g, :, r * LANE:(r + 1) * LANE] - m_safe)
                new_ms.append(m_new)
                new_ls.append(alpha * ls[h] + jnp.sum(p, axis=0, keepdims=True))
                ps.append(p.astype(BF16))
                alphas.append(alpha)
            vt = vt_ref[j, g * A_HEAD_DIM:(g + 1) * A_HEAD_DIM, :]
            pv = _dot(vt, jnp.concatenate(ps, axis=1))
            acc_ref[g] = acc_ref[g] * jnp.concatenate(alphas, axis=1) + pv
        return tuple(new_ms), tuple(new_ls)

    def tile_body(j, carry):
        carry = softmax_stage(j, carry)
        logits_stage(j + 1)
        return carry

    init = (tuple(jnp.full((1, LANE), NEG_INF, F32) for _ in range(A_HEADS)),
            tuple(jnp.zeros((1, LANE), F32) for _ in range(A_HEADS)))
    logits_stage(0)
    carry = lax.fori_loop(0, n2 - 1, tile_body, init)
    _, ls = softmax_stage(n2 - 1, carry)
    for g in range(A_KV_HEADS):
        for r in range(rep):
            h = g * rep + r
            ot_ref[h * A_HEAD_DIM:(h + 1) * A_HEAD_DIM, :] = (
                acc_ref[g, :, r * LANE:(r + 1) * LANE] / ls[h])
    o_ref[...] = ot_ref[...].T.astype(o_ref.dtype)


def _dsa(h_a, h_f, bias_tiles, bsz, seq):
    n = bsz * seq
    nq = seq // LANE
    k_sel = min(TOPK_MAX, seq // 4)
    assert k_sel % LANE == 0 and seq % (2 * LANE) == 0
    qw = A_HEADS * LANE
    slab_blk = (h_f.shape[1] - LANE) // LANE
    return pl.pallas_call(
        functools.partial(_dsa_kernel, k_sel=k_sel, seq=seq),
        out_shape=jax.ShapeDtypeStruct((n, A_HEADS * A_HEAD_DIM), BF16),
        grid=(bsz, nq),
        in_specs=[
            pl.BlockSpec((LANE, qw), lambda b, i: (b * nq + i, 0)),
            pl.BlockSpec((seq, A_KV_HEADS * LANE), lambda b, i: (b, qw // (A_KV_HEADS * LANE))),
            pl.BlockSpec((seq, LANE), lambda b, i: (b, (qw + A_KV_HEADS * LANE) // LANE)),
            pl.BlockSpec((LANE, IDX_HEADS * LANE), lambda b, i: (b * nq + i, 0)),
            pl.BlockSpec((seq, LANE), lambda b, i: (b, slab_blk)),
            pl.BlockSpec((LANE, LANE), lambda b, i: (b * nq + i, slab_blk)),
            pl.BlockSpec((A_HEADS, 3, LANE, LANE), lambda b, i: (0, 0, 0, 0)),
        ],
        out_specs=pl.BlockSpec((LANE, A_HEADS * A_HEAD_DIM), lambda b, i: (b * nq + i, 0)),
        scratch_shapes=[
            pltpu.VMEM((seq + LANE, LANE), I32),
            pltpu.VMEM((seq, LANE), F32),
            pltpu.VMEM((nq // 2, LANE, 2 * LANE), BF16),
            pltpu.VMEM((A_KV_HEADS, A_HEAD_DIM, (A_HEADS // A_KV_HEADS) * LANE), F32),
            pltpu.VMEM((2, A_KV_HEADS, 2 * LANE, (A_HEADS // A_KV_HEADS) * LANE), F32),
            pltpu.VMEM((2, A_HEADS, LANE), F32),
            pltpu.VMEM((A_HEADS * A_HEAD_DIM, LANE), F32),
        ],
        compiler_params=_params(("arbitrary", "arbitrary")),
        name="dsa_attention",
    )(h_a, h_a, h_a, h_f, h_f, h_f, bias_tiles)


def _gla_kernel(bq_ref, bk_ref, bv_ref, br_ref, slab_ref, w2_ref, gb_ref, ng_ref, o_ref,
                st_ref, *, n_chunks):
    @pl.when(pl.program_id(1) == 0)
    def _():
        st_ref[...] = jnp.zeros(st_ref.shape, F32)

    ch = GLA_CHUNK
    rws = n_chunks * ch
    shift = ch.bit_length() - 1
    assert 1 << shift == ch
    row = lax.broadcasted_iota(I32, (rws, rws), 0)
    col = lax.broadcasted_iota(I32, (rws, rws), 1)
    tril = (lax.shift_right_logical(row, shift) == lax.shift_right_logical(col, shift)) & (row >= col)
    tri = jnp.where(tril, 1.0, 0.0).astype(BF16)
    gate = _dot(slab_ref[...].astype(BF16), w2_ref[...]) + gb_ref[...]
    log_a = (jnp.minimum(gate, 0.0) - jnp.log1p(jnp.exp(-jnp.abs(gate)))) / GATE_TAU
    hi, mid, lo = _split3(log_a)
    cum = _dot(tri, hi) + _dot(tri, mid) + _dot(tri, lo)
    lasts = [cum[(c + 1) * ch - 1:(c + 1) * ch, :] for c in range(n_chunks)]
    last = jnp.concatenate([jnp.broadcast_to(l, (ch, l.shape[1])) for l in lasts], axis=0)
    q = bq_ref[...] * (B_KEY_DIM ** -0.5)
    k = bk_ref[...]
    q_dec = (q * jnp.exp(cum)).astype(BF16)
    k_inv = (k * jnp.exp(-cum)).astype(BF16)
    k_end = (k * jnp.exp(last - cum)).astype(BF16)
    decays = [jnp.exp(l) for l in lasts]
    for h in range(B_HEADS):
        cs = slice(h * LANE, (h + 1) * LANE)
        v = bv_ref[:, cs].astype(BF16)
        sc = jnp.where(tril, _dot_nt(q_dec[:, cs], k_inv[:, cs]), 0.0)
        o_intra = _dot(sc.astype(BF16), v)
        st = st_ref[h]
        parts = []
        for c in range(n_chunks):
            rows = slice(c * ch, (c + 1) * ch)
            parts.append(o_intra[rows] + _dot_nt(q_dec[rows, cs], st.astype(BF16)))
            st = decays[c][:, cs] * st + _dot_tn(v[rows], k_end[rows, cs])
        st_ref[h] = st
        o = jnp.concatenate(parts, axis=0)
        mu = jnp.mean(o, axis=-1, keepdims=True)
        dlt = o - mu
        var = jnp.mean(dlt * dlt, axis=-1, keepdims=True)
        on = dlt * lax.rsqrt(var + LN_EPS) * ng_ref[:, cs]
        r = br_ref[:, cs]
        o_ref[:, cs] = (on * (r * _sigmoid(r))).astype(o_ref.dtype)


def _gla(h_f, w2p, gbp, norm_g, bsz, seq, rows_per_step=256):
    n = bsz * seq
    hw = B_HEADS * LANE
    steps = seq // rows_per_step
    slab_blk = (h_f.shape[1] - LANE) // LANE
    blk = lambda cb: pl.BlockSpec((rows_per_step, hw), lambda b, s, cb=cb: (b * steps + s, cb))
    return pl.pallas_call(
        functools.partial(_gla_kernel, n_chunks=rows_per_step // GLA_CHUNK),
        out_shape=jax.ShapeDtypeStruct((n, hw), BF16),
        grid=(bsz, steps),
        in_specs=[blk(1), blk(2), blk(3), blk(4),
                  pl.BlockSpec((rows_per_step, LANE), lambda b, s: (b * steps + s, slab_blk)),
                  pl.BlockSpec((LANE, hw), lambda b, s: (0, 0)),
                  pl.BlockSpec((1, hw), lambda b, s: (0, 0)),
                  pl.BlockSpec((1, hw), lambda b, s: (0, 0))],
        out_specs=pl.BlockSpec((rows_per_step, hw), lambda b, s: (b * steps + s, 0)),
        scratch_shapes=[pltpu.VMEM((B_HEADS, LANE, LANE), F32)],
        compiler_params=_params(("arbitrary", "arbitrary")),
        name="gla_attention",
    )(h_f, h_f, h_f, h_f, h_f, w2p, gbp, norm_g)


def _outproj_ln_kernel(oa_ref, ob_ref, x_ref, w_ref, g_ref, b_ref, out_ref):
    ka = oa_ref.shape[1]
    m = _dot(oa_ref[...], w_ref[0:ka, :]) + _dot(ob_ref[...], w_ref[ka:, :])
    out_ref[...] = _layer_norm(DN_ALPHA * x_ref[...] + m, g_ref[...], b_ref[...])


def _outproj_ln(o_a, o_b, x2, w, g, b, tm=512):
    n, d = x2.shape
    row = lambda width: pl.BlockSpec((tm, width), lambda i: (i, 0))
    full = lambda a: pl.BlockSpec(a.shape, lambda i: (0, 0))
    return pl.pallas_call(
        _outproj_ln_kernel,
        out_shape=jax.ShapeDtypeStruct((n, d), F32),
        grid=(n // tm,),
        in_specs=[row(o_a.shape[1]), row(o_b.shape[1]), row(d), full(w), full(g), full(b)],
        out_specs=row(d),
        compiler_params=_params(("arbitrary",)),
        name="out_proj_ln",
    )(o_a, o_b, x2, w, g, b)


def _route(lt):
    gl = [lt[g:g + 1, :] for g in range(MOE_GROUPS)]
    best, gsel = gl[0], jnp.zeros_like(gl[0], dtype=I32)
    for g in range(1, MOE_GROUPS):
        better = gl[g] > best
        gsel = jnp.where(better, g, gsel)
        best = jnp.where(better, gl[g], best)
    denom = sum(jnp.exp(x - best) for x in gl)
    g_w = 1.0 / denom
    fl = []
    for e in range(MOE_EXPERTS):
        acc = jnp.zeros_like(best)
        for g in range(MOE_GROUPS):
            r = MOE_GROUPS + g * MOE_EXPERTS + e
            acc = jnp.where(gsel == g, lt[r:r + 1, :], acc)
        fl.append(acc)
    v1, i1 = fl[0], jnp.zeros_like(gsel)
    for e in range(1, MOE_EXPERTS):
        better = fl[e] > v1
        i1 = jnp.where(better, e, i1)
        v1 = jnp.where(better, fl[e], v1)
    v2, i2 = jnp.full_like(v1, NEG_INF), jnp.zeros_like(gsel)
    for e in range(MOE_EXPERTS):
        better = (fl[e] > v2) & (i1 != e)
        i2 = jnp.where(better, e, i2)
        v2 = jnp.where(better, fl[e], v2)
    e2 = jnp.exp(v2 - v1)
    w1 = (1.0 / (1.0 + e2)) * g_w
    w2 = (e2 / (1.0 + e2)) * g_w
    e_gate = [jnp.where(i1 == e, w1, 0.0) + jnp.where(i2 == e, w2, 0.0)
              for e in range(MOE_EXPERTS)]
    return gsel, e_gate


def _moe_route_kernel(x_ref, wr_ref, rb_ref, xa_ref, rt_ref, cnt_ref, tri_ref, carry_ref):
    i = pl.program_id(0)
    tm, d = x_ref.shape

    @pl.when(i == 0)
    def _():
        carry_ref[...] = jnp.zeros(carry_ref.shape, F32)
        before = (lax.broadcasted_iota(I32, (tm, tm), 0) < lax.broadcasted_iota(I32, (tm, tm), 1))
        tri_ref[...] = jnp.where(before, 1.0, 0.0).astype(BF16)

    x = x_ref[...]
    xs = _split3(x)
    ws = _split3(wr_ref[...])
    lt = jnp.zeros((wr_ref.shape[0], tm), F32)
    for a, b in ((2, 0), (0, 2), (1, 1), (1, 0), (0, 1), (0, 0)):
        lt = lt + _dot_nt(ws[a], xs[b])
    lt = lt + rb_ref[...]
    gsel, e_gate = _route(lt)

    row8 = lax.broadcasted_iota(I32, (8, tm), 0)
    onehot = jnp.where(row8 == gsel, 1.0, 0.0)
    earlier = _dot(onehot.astype(BF16), tri_ref[...])
    rank = jnp.sum(onehot * (earlier + carry_ref[:, 0:1]), axis=0, keepdims=True)
    rt_ref[...] = jnp.where(row8 == 0, gsel, jnp.where(row8 == 1, rank.astype(I32), 0))
    carry_ref[...] = carry_ref[...] + jnp.sum(onehot, axis=1, keepdims=True)
    cnt_ref[...] = carry_ref[...].astype(I32)

    erow = lax.broadcasted_iota(I32, (LANE, tm), 0)
    gt = jnp.zeros((LANE, tm), F32)
    for e in range(MOE_EXPERTS):
        gt = jnp.where(erow == e, e_gate[e], gt)
    xa_ref[:, :d] = x
    xa_ref[:, d:] = gt.T


def _moe_route(h, wr, rb, tm=1024):
    n, d = h.shape
    tm = min(tm, n)
    full = lambda a: pl.BlockSpec(a.shape, lambda i: (0,) * a.ndim)
    return pl.pallas_call(
        _moe_route_kernel,
        out_shape=[jax.ShapeDtypeStruct((n, d + LANE), F32), jax.ShapeDtypeStruct((8, n), I32),
                   jax.ShapeDtypeStruct((8, LANE), I32)],
        grid=(n // tm,),
        in_specs=[pl.BlockSpec((tm, d), lambda i: (i, 0)), full(wr), full(rb)],
        out_specs=[pl.BlockSpec((tm, d + LANE), lambda i: (i, 0)),
                   pl.BlockSpec((8, tm), lambda i: (0, i)),
                   pl.BlockSpec((8, LANE), lambda i: (0, 0))],
        scratch_shapes=[pltpu.VMEM((tm, tm), BF16), pltpu.VMEM((8, LANE), F32)],
        compiler_params=_params(("arbitrary",)),
        name="moe_route",
    )(h, wr, rb)


def _row_copy_waves(n_rows, wave, start_row, wait_wave):
    n_waves = n_rows // wave
    for w in range(n_waves + 1):
        if w < n_waves:
            lax.fori_loop(w * wave, (w + 1) * wave, lambda r, c, w=w: (start_row(r, w % 2), c)[1], 0,
                          unroll=8)
        if w >= 1:
            wait_wave((w - 1) % 2)


def _moe_dispatch_kernel(pos_ref, fill_ref, xa_ref, xs_hbm, zero_ref, sem, *, wave):
    i = pl.program_id(0)
    tm = pos_ref.shape[1]
    tme = zero_ref.shape[0]

    def copy(r, s):
        return pltpu.make_async_copy(xa_ref.at[r], xs_hbm.at[pos_ref[0, r]], sem.at[s])

    def wait_wave(s):
        rows = pl.ds(0, wave)
        pltpu.make_async_copy(xa_ref.at[rows], xs_hbm.at[rows], sem.at[s]).wait()

    _row_copy_waves(tm, wave, lambda r, s: copy(r, s).start(), wait_wave)

    @pl.when(i == pl.num_programs(0) - 1)
    def _():
        zero_ref[...] = jnp.zeros(zero_ref.shape, F32)
        for g in range(MOE_GROUPS):
            first, count = fill_ref[0, g], fill_ref[1, g]

            def zcopy(r, first=first):
                return pltpu.make_async_copy(zero_ref.at[0], xs_hbm.at[first + r], sem.at[0])

            lax.fori_loop(0, count, lambda r, c: (zcopy(r).start(), c)[1], 0)
            lax.fori_loop(0, count, lambda r, c: (zcopy(r).wait(), c)[1], 0)
        n_tiles = fill_ref[2, 0]
        for k in range(MOE_GROUPS):
            @pl.when(k < fill_ref[2, 1])
            def _(k=k):
                tail = xs_hbm.at[pl.ds(pl.multiple_of((n_tiles + k) * tme, tme), tme)]
                cp = pltpu.make_async_copy(zero_ref, tail, sem.at[0])
                cp.start()
                cp.wait()


def _moe_dispatch(xa, pos, fill, n_sorted, tme, tm=1024, wave=256):
    n, da = xa.shape
    tm = min(tm, n)
    wave = min(wave, tm)
    return pl.pallas_call(
        functools.partial(_moe_dispatch_kernel, wave=wave),
        out_shape=jax.ShapeDtypeStruct((n_sorted, da), F32),
        grid=(n // tm,),
        in_specs=[pl.BlockSpec((1, tm), lambda i: (0, i), memory_space=pltpu.SMEM),
                  pl.BlockSpec(memory_space=pltpu.SMEM),
                  pl.BlockSpec((tm, da), lambda i: (i, 0))],
        out_specs=pl.BlockSpec(memory_space=pl.ANY),
        scratch_shapes=[pltpu.VMEM((tme, da), F32), pltpu.SemaphoreType.DMA((2,))],
        compiler_params=_params(("arbitrary",)),
        name="moe_dispatch",
    )(pos, fill, xa)


def _moe_expert_kernel(tg_ref, tj_ref, nt_ref, xs_ref, wg_ref, wu_ref, wd_ref, lg_ref, lb_ref,
                       ys_ref, wgb_ref, wub_ref, wdb_ref):
    d = ys_ref.shape[1]
    i = pl.program_id(0)

    @pl.when((i == 0) | (tg_ref[i] != tg_ref[jnp.maximum(i - 1, 0)]))
    def _():
        for e in range(MOE_EXPERTS):
            wgb_ref[e] = wg_ref[e].astype(BF16)
            wub_ref[e] = wu_ref[e].astype(BF16)
            wdb_ref[e] = wd_ref[e].astype(BF16)

    @pl.when(i < nt_ref[0])
    def _():
        x = xs_ref[:, :d]
        gates = xs_ref[:, d:]
        xb = x.astype(BF16)
        y = jnp.zeros(x.shape, F32)
        for e in range(MOE_EXPERTS):
            hg = _dot(xb, wgb_ref[e])
            hu = _dot(xb, wub_ref[e])
            hid = hg * _sigmoid(hg) * hu * gates[:, e:e + 1]
            y = y + _dot(hid.astype(BF16), wdb_ref[e])
        ys_ref[...] = _layer_norm(DN_ALPHA * x + y, lg_ref[...], lb_ref[...])

    @pl.when(pl.program_id(0) >= nt_ref[0])
    def _():
        ys_ref[...] = jnp.zeros(ys_ref.shape, F32)


def _moe_experts(xs, tile_g, tile_j, n_tiles, wg, wu, wd, lg, lb, tme):
    p, da = xs.shape
    d = da - LANE
    grp = lambda w: pl.BlockSpec((None,) + w.shape[1:], lambda i, tg, tj, nt: (tg[i], 0, 0, 0),
                                 pipeline_mode=pl.Buffered(1))
    full = lambda a: pl.BlockSpec(a.shape, lambda i, tg, tj, nt: (0,) * a.ndim)
    grid_spec = pltpu.PrefetchScalarGridSpec(
        num_scalar_prefetch=3,
        grid=(p // tme,),
        in_specs=[pl.BlockSpec((tme, da), lambda i, tg, tj, nt: (tj[i], 0)),
                  grp(wg), grp(wu), grp(wd), full(lg), full(lb)],
        out_specs=pl.BlockSpec((tme, d), lambda i, tg, tj, nt: (i, 0)),
        scratch_shapes=[pltpu.VMEM(w.shape[1:], BF16) for w in (wg, wu, wd)],
    )
    return pl.pallas_call(
        _moe_expert_kernel,
        out_shape=jax.ShapeDtypeStruct((p, d), F32),
        grid_spec=grid_spec,
        compiler_params=pltpu.CompilerParams(dimension_semantics=("arbitrary",),
                                             vmem_limit_bytes=MOE_VMEM_LIMIT),
        name="moe_experts",
    )(tile_g, tile_j, n_tiles, xs, wg, wu, wd, lg, lb)


def _moe_combine_kernel(pos_ref, ys_hbm, o_ref, sem, *, wave):
    tm = o_ref.shape[0]

    def copy(r, s):
        return pltpu.make_async_copy(ys_hbm.at[pos_ref[0, r]], o_ref.at[r], sem.at[s])

    def wait_wave(s):
        rows = pl.ds(0, wave)
        pltpu.make_async_copy(ys_hbm.at[rows], o_ref.at[rows], sem.at[s]).wait()

    _row_copy_waves(tm, wave, lambda r, s: copy(r, s).start(), wait_wave)


def _moe_combine(ys, pos, n, tm=1024, wave=256):
    d = ys.shape[1]
    tm = min(tm, n)
    wave = min(wave, tm)
    return pl.pallas_call(
        functools.partial(_moe_combine_kernel, wave=wave),
        out_shape=jax.ShapeDtypeStruct((n, d), F32),
        grid=(n // tm,),
        in_specs=[pl.BlockSpec((1, tm), lambda i: (0, i), memory_space=pltpu.SMEM),
                  pl.BlockSpec(memory_space=pl.ANY)],
        out_specs=pl.BlockSpec((tm, d), lambda i: (i, 0)),
        scratch_shapes=[pltpu.SemaphoreType.DMA((2,))],
        compiler_params=_params(("arbitrary",)),
        name="moe_combine",
    )(pos, ys)


def _moe(h, wr, rb, wg, wu, wd, lg, lb, regroup=None, tme=512):
    n, d = h.shape
    tme = min(tme, n)
    xa, rt, cnt = _moe_route(h, wr, rb)
    counts = cnt[:MOE_GROUPS, 0]
    tiles = (counts + tme - 1) // tme
    first_tile = jnp.cumsum(tiles) - tiles
    gid, rank = rt[0], rt[1]
    pos = (first_tile[gid] * tme + rank)[None, :]
    n_slots = n // tme + MOE_GROUPS
    n_tiles = jnp.sum(tiles)
    fill = jnp.stack([first_tile * tme + counts, tiles * tme - counts,
                      jnp.zeros((MOE_GROUPS,), I32).at[0].set(n_tiles).at[1].set(n_slots - n_tiles)])
    slot = jnp.minimum(jnp.arange(n_slots, dtype=I32), n_tiles - 1)
    tile_g = jnp.sum(slot[:, None] >= (first_tile + tiles)[None, :], axis=1).astype(I32)
    xs = _moe_dispatch(xa, pos, fill, n_slots * tme, tme)
    ys = _moe_experts(xs, tile_g, slot, n_tiles[None].astype(I32), wg, wu, wd, lg, lb, tme)
    if regroup is not None:
        pos = pos.reshape(regroup).T.reshape(1, n)
    return _moe_combine(ys, pos, n)


def _s5_prep_kernel(lre_ref, lim_ref, ldt_ref, bre_ref, bim_ref, are_ref, aim_ref, bbre_ref,
                    bbim_ref):
    lr = jnp.minimum(lre_ref[...], -1e-4)
    li = lim_ref[...]
    dt = jnp.exp(ldt_ref[...])
    mag = jnp.exp(lr * dt)
    ab_re = mag * jnp.cos(li * dt)
    ab_im = mag * jnp.sin(li * dt)
    den = lr * lr + li * li
    nr = ab_re - 1.0
    coef_re = (nr * lr + ab_im * li) / den
    coef_im = (ab_im * lr - nr * li) / den
    are_ref[...] = ab_re
    aim_ref[...] = ab_im
    bbre_ref[...] = coef_re * bre_ref[...] - coef_im * bim_ref[...]
    bbim_ref[...] = coef_re * bim_ref[...] + coef_im * bre_ref[...]


def _s5_prep(lam_re, lam_im, log_dt, b_re, b_im):
    gp = C_GROUPS * C_STATE
    col = lambda a: a.reshape(gp, 1)
    ldt = jnp.broadcast_to(log_dt[:, None], (C_GROUPS, C_STATE))
    outs = pl.pallas_call(
        _s5_prep_kernel,
        out_shape=[jax.ShapeDtypeStruct((gp, 1), F32), jax.ShapeDtypeStruct((gp, 1), F32),
                   jax.ShapeDtypeStruct((gp, C_GROUP), F32), jax.ShapeDtypeStruct((gp, C_GROUP), F32)],
        name="s5_discretise",
    )(col(lam_re), col(lam_im), col(ldt), b_re.reshape(gp, C_GROUP), b_im.reshape(gp, C_GROUP))
    ab_re, ab_im, bb_re, bb_im = outs
    return (ab_re.reshape(C_GROUPS, C_STATE), ab_im.reshape(C_GROUPS, C_STATE),
            bb_re.reshape(C_GROUPS, C_STATE, C_GROUP), bb_im.reshape(C_GROUPS, C_STATE, C_GROUP))


def _s5_uproj_kernel(x_ref, w_ref, o_ref):
    o_ref[...] = _dot(x_ref[...].astype(BF16), w_ref[...])


def _s5_uproj(h2, w, tm=512):
    n, d = h2.shape
    return pl.pallas_call(
        _s5_uproj_kernel,
        out_shape=jax.ShapeDtypeStruct((n, d), F32),
        grid=(n // tm,),
        in_specs=[pl.BlockSpec((tm, d), lambda i: (i, 0)), pl.BlockSpec(w.shape, lambda i: (0, 0))],
        out_specs=pl.BlockSpec((tm, d), lambda i: (i, 0)),
        compiler_params=_params(("arbitrary",)),
        name="s5_in_proj",
    )(h2, w)


def _gelu_tanh(x):
    return 0.5 * x * (1.0 + jnp.tanh(math.sqrt(2.0 / math.pi) * (x + 0.044715 * (x * x * x))))


def _s5_scan_kernel(u_ref, bbre_ref, bbim_ref, are_ref, aim_ref, cm_ref, d_ref, g_ref,
                    st_ref, bure_ref, buim_ref, s_ref, *, bsz, tl, n_cb, sub):
    @pl.when(pl.program_id(0) == 0)
    def _():
        st_ref[...] = jnp.zeros(st_ref.shape, F32)

    cw = D_MODEL // n_cb
    sw = bure_ref.shape[2]

    def input_states(cb):
        ub = u_ref[:, cb * cw:(cb + 1) * cw].astype(BF16)
        bure_ref[cb % 2] = _dot(ub, bbre_ref[cb])
        buim_ref[cb % 2] = _dot(ub, bbim_ref[cb])

    input_states(0)
    for cb in range(n_cb):
        buf = cb % 2
        if cb + 1 < n_cb:
            input_states(cb + 1)
        for s0 in range(0, sw, sub):
            cs = slice(s0, s0 + sub)
            a_re = jnp.broadcast_to(are_ref[cb, :, cs], (bsz, sub))
            a_im = jnp.broadcast_to(aim_ref[cb, :, cs], (bsz, sub))
            s_re, s_im = st_ref[cb, 0, :, cs], st_ref[cb, 1, :, cs]
            for t in range(tl):
                rows = slice(t * bsz, (t + 1) * bsz)
                s_re, s_im = (a_re * s_re - a_im * s_im + bure_ref[buf, rows, cs],
                              a_re * s_im + a_im * s_re + buim_ref[buf, rows, cs])
                s_ref[buf, rows, cs] = s_re.astype(BF16)
                s_ref[buf, rows, sw + s0:sw + s0 + sub] = s_im.astype(BF16)
            st_ref[cb, 0, :, cs] = s_re
            st_ref[cb, 1, :, cs] = s_im
        ch = slice(cb * cw, (cb + 1) * cw)
        y = _dot(s_ref[buf], cm_ref[cb]) + d_ref[:, ch] * u_ref[:, ch]
        g_ref[:, ch] = _gelu_tanh(y).astype(g_ref.dtype)


def _s5_scan(u_lb, bbre, bbim, a_re, a_im, cmat, d_skip, bsz, seq, tl=32, sub=512):
    n, d = u_lb.shape
    n_cb, cw, sw = bbre.shape
    rows = tl * bsz
    full = lambda a: pl.BlockSpec(a.shape, lambda t: (0,) * a.ndim)
    return pl.pallas_call(
        functools.partial(_s5_scan_kernel, bsz=bsz, tl=tl, n_cb=n_cb, sub=sub),
        out_shape=jax.ShapeDtypeStruct((n, d), BF16),
        grid=(seq // tl,),
        in_specs=[pl.BlockSpec((rows, d), lambda t: (t, 0)), full(bbre), full(bbim), full(a_re),
                  full(a_im), full(cmat), full(d_skip)],
        out_specs=pl.BlockSpec((rows, d), lambda t: (t, 0)),
        scratch_shapes=[pltpu.VMEM((n_cb, 2, bsz, sw), F32), pltpu.VMEM((2, rows, sw), F32),
                        pltpu.VMEM((2, rows, sw), F32), pltpu.VMEM((2, rows, 2 * sw), BF16)],
        compiler_params=_params(("arbitrary",)),
        name="s5_scan",
    )(u_lb, bbre, bbim, a_re, a_im, cmat, d_skip)


def _s5_glu_kernel(g_ref, x_ref, w1_ref, w2_ref, wo_ref, lg_ref, lb_ref, out_ref):
    g = g_ref[...]
    z = _dot(g, w1_ref[...]) * _sigmoid(_dot(g, w2_ref[...]))
    m = _dot(z.astype(BF16), wo_ref[...])
    out_ref[...] = _layer_norm(DN_ALPHA * x_ref[...] + m, lg_ref[...], lb_ref[...])


def _s5_glu(g, h2, w1, w2, wo, lg, lb, tm=512):
    n, d = h2.shape
    row = pl.BlockSpec((tm, d), lambda i: (i, 0))
    full = lambda a: pl.BlockSpec(a.shape, lambda i: (0, 0))
    return pl.pallas_call(
        _s5_glu_kernel,
        out_shape=jax.ShapeDtypeStruct((n, d), F32),
        grid=(n // tm,),
        in_specs=[row, row, full(w1), full(w2), full(wo), full(lg), full(lb)],
        out_specs=row,
        compiler_params=_params(("arbitrary",)),
        name="s5_glu_out_ln",
    )(g, h2, w1, w2, wo, lg, lb)


def _pad_heads(w, heads, dim):
    d = w.shape[0]
    w = w.reshape(d, heads, dim)
    return jnp.pad(w, ((0, 0), (0, 0), (0, LANE - dim))).reshape(d, heads * LANE)


def _even_layer_weights(w_in, gate_w2, gate_b):
    splits = (A_HEADS * A_HEAD_DIM, A_KV_HEADS * A_HEAD_DIM, A_KV_HEADS * A_HEAD_DIM,
              IDX_HEADS * IDX_DIM, IDX_DIM, IDX_HEADS,
              B_HEADS * B_KEY_DIM, B_HEADS * B_KEY_DIM, B_HEADS * B_VAL_DIM, GATE_RANK,
              B_HEADS * B_VAL_DIM)
    offs = np.cumsum((0,) + splits)
    aq, ak, av, iq, ik, iw, bq, bk, bv, bg, br = [w_in[:, offs[k]:offs[k + 1]] for k in range(11)]
    d = w_in.shape[0]
    slab = jnp.concatenate(
        [ik, iw, bg, jnp.zeros((d, LANE - IDX_DIM - IDX_HEADS - GATE_RANK), w_in.dtype)], axis=1)
    w_a = jnp.concatenate([_pad_heads(aq, A_HEADS, A_HEAD_DIM),
                           _pad_heads(ak, A_KV_HEADS, A_HEAD_DIM), av], axis=1)
    w_f = jnp.concatenate([_pad_heads(iq, IDX_HEADS, IDX_DIM), _pad_heads(bq, B_HEADS, B_KEY_DIM),
                           _pad_heads(bk, B_HEADS, B_KEY_DIM), bv, br, slab], axis=1)
    w_all = jnp.concatenate([w_a, w_f], axis=1).astype(BF16)
    g0 = IDX_DIM + IDX_HEADS
    w2p = jnp.zeros((LANE, B_HEADS * LANE), F32).at[g0:g0 + GATE_RANK].set(
        _pad_heads(gate_w2, B_HEADS, B_KEY_DIM)).astype(BF16)
    gbp = _pad_heads(gate_b[None, :], B_HEADS, B_KEY_DIM)
    return w_all, (w_a.shape[1], w_f.shape[1]), w2p, gbp


def _even_mixer_ln(x2, bsz, seq, w_in, bias_tiles, gate_w2, gate_b, norm_g, w_out, ln_g, ln_b):
    w_all, widths, w2p, gbp = _even_layer_weights(w_in, gate_w2, gate_b)
    h_a, h_f = _in_proj(x2, w_all, widths, (BF16, F32))
    o_a = _dsa(h_a, h_f, bias_tiles, bsz, seq)
    o_b = _gla(h_f, w2p, gbp, norm_g[None, :], bsz, seq)
    return _outproj_ln(o_a, o_b, x2, w_out.astype(BF16), ln_g[None, :], ln_b[None, :])


def _block_diag(m, per):
    g, r, c = m.shape
    eye = jnp.eye(per, dtype=m.dtype)
    m = m.reshape(g // per, per, r, c)
    bd = m[:, :, :, None, :] * eye[None, :, None, :, None]
    return bd.reshape(g // per, per * r, per * c)


def _s5_mixer_ln(h2, bsz, seq, w_in, lam_re, lam_im, log_dt, b_re, b_im, c_re, c_im, d_skip,
                 glu_w1, glu_w2, w_out, ln_g, ln_b, groups_per_block=16):
    ab_re, ab_im, bb_re, bb_im = _s5_prep(lam_re, lam_im, log_dt, b_re, b_im)
    per = groups_per_block
    n_cb = C_GROUPS // per
    bbre = _block_diag(jnp.swapaxes(bb_re, 1, 2), per).astype(BF16)
    bbim = _block_diag(jnp.swapaxes(bb_im, 1, 2), per).astype(BF16)
    a_re = ab_re.reshape(n_cb, 1, per * C_STATE)
    a_im = ab_im.reshape(n_cb, 1, per * C_STATE)
    cre = _block_diag(jnp.swapaxes(c_re, 1, 2), per)
    cim = _block_diag(jnp.swapaxes(c_im, 1, 2), per)
    cmat = jnp.concatenate([cre, -cim], axis=1).astype(BF16)
    u = _s5_uproj(h2, w_in.astype(BF16))
    g = _s5_scan(u, bbre, bbim, a_re, a_im, cmat, d_skip[None, :], bsz, seq)
    return _s5_glu(g, h2, glu_w1.astype(BF16), glu_w2.astype(BF16), w_out.astype(BF16),
                   ln_g[None, :], ln_b[None, :])


def _moe_ln(h, r_coarse, rb_coarse, r_fine, rb_fine, w_gate, w_up, w_down, ln_g, ln_b,
            regroup=None):
    d = h.shape[1]
    rows = MOE_GROUPS + N_EXPERTS
    wr = jnp.concatenate([r_coarse.T, jnp.transpose(r_fine, (0, 2, 1)).reshape(N_EXPERTS, d),
                          jnp.zeros((32 - rows, d), F32)], axis=0)
    rb = jnp.concatenate([rb_coarse, rb_fine.reshape(N_EXPERTS), jnp.zeros((32 - rows,), F32)])
    return _moe(h, wr, rb[:, None], w_gate, w_up, w_down, ln_g[None, :], ln_b[None, :], regroup)


def kernel(x, rel_bias, ab_w_in, gla_gate_w2, gla_gate_b, gla_norm_g, ab_w_out, s5_w_in, s5_lam_re, s5_lam_im, s5_log_dt, s5_b_re, s5_b_im, s5_c_re, s5_c_im, s5_d, s5_glu_w1, s5_glu_w2, s5_w_out, ln_mix_g, ln_mix_b, ln_ffn_g, ln_ffn_b, moe_r_coarse, moe_rb_coarse, moe_r_fine, moe_rb_fine, moe_w_gate, moe_w_up, moe_w_down):
    bsz, seq, d = x.shape
    h = x.reshape(bsz * seq, d)
    bias_tiles = _bias_tiles(rel_bias)
    time_major = False
    for layer in range(DEPTH):
        i = layer // 2
        assert time_major == (layer % 2 == 1)
        if layer % 2 == 0:
            h = _even_mixer_ln(h, bsz, seq, ab_w_in[i], bias_tiles, gla_gate_w2[i], gla_gate_b[i],
                               gla_norm_g[i], ab_w_out[i], ln_mix_g[layer], ln_mix_b[layer])
        else:
            h = _s5_mixer_ln(h, bsz, seq, s5_w_in[i], s5_lam_re[i], s5_lam_im[i], s5_log_dt[i],
                             s5_b_re[i], s5_b_im[i], s5_c_re[i], s5_c_im[i], s5_d[i],
                             s5_glu_w1[i], s5_glu_w2[i], s5_w_out[i],
                             ln_mix_g[layer], ln_mix_b[layer])
        want_time_major = layer + 1 < DEPTH and (layer + 1) % 2 == 1
        regroup = None
        if want_time_major != time_major:
            regroup = (seq, bsz) if time_major else (bsz, seq)
        h = _moe_ln(h, moe_r_coarse[layer], moe_rb_coarse[layer], moe_r_fine[layer],
                    moe_rb_fine[layer], moe_w_gate[layer], moe_w_up[layer], moe_w_down[layer],
                    ln_ffn_g[layer], ln_ffn_b[layer], regroup)
        time_major = want_time_major
    return h.reshape(bsz, seq, d).astype(x.dtype)
```

```python
import functools
import math

import numpy as np
import jax
import jax.numpy as jnp
from jax import lax
from jax.experimental import pallas as pl
from jax.experimental.pallas import tpu as pltpu

F32 = jnp.float32
BF16 = jnp.bfloat16
I32 = jnp.int32

D_MODEL = 1024
DEPTH = 2
DN_ALPHA = (2.0 * DEPTH) ** 0.25
LN_EPS = 1e-5
A_HEAD_DIM = 64
A_HEADS = 8
A_KV_HEADS = 2
IDX_HEADS = 4
IDX_DIM = 64
TOPK_MAX = 256
REL_BUCKETS = 32
REL_MAX_DIST = 128
B_HEADS = 4
B_VAL_DIM = 128
B_KEY_DIM = 64
GATE_RANK = 16
GATE_TAU = 16.0
GLA_CHUNK = 64
C_GROUP = 16
C_GROUPS = 64
C_STATE = 64
MOE_GROUPS = 4
MOE_EXPERTS = 4
MOE_HIDDEN = 512
N_EXPERTS = MOE_GROUPS * MOE_EXPERTS

LANE = 128
VMEM_LIMIT = 52 * 1024 * 1024
MOE_VMEM_LIMIT = 58 * 1024 * 1024
NEG_INF = float("-inf")
INT_MIN = -(2 ** 31)

_NT = (((1,), (1,)), ((), ()))
_TN = (((0,), (0,)), ((), ()))


def _dot(a, b):
    return jnp.dot(a, b, preferred_element_type=F32)


def _dot_nt(a, b):
    return lax.dot_general(a, b, _NT, preferred_element_type=F32)


def _dot_tn(a, b):
    return lax.dot_general(a, b, _TN, preferred_element_type=F32)


def _split3(x):
    hi = x.astype(BF16)
    r1 = x - hi.astype(F32)
    mid = r1.astype(BF16)
    lo = (r1 - mid.astype(F32)).astype(BF16)
    return hi, mid, lo


def _params(sem):
    return pltpu.CompilerParams(dimension_semantics=sem, vmem_limit_bytes=VMEM_LIMIT)


def _layer_norm(y, g, b):
    mu = jnp.mean(y, axis=-1, keepdims=True)
    d = y - mu
    var = jnp.mean(d * d, axis=-1, keepdims=True)
    return d * lax.rsqrt(var + LN_EPS) * g + b


def _sigmoid(x):
    return 1.0 / (1.0 + jnp.exp(-x))


def _bucket_thresholds():
    max_exact = REL_BUCKETS // 2
    nf = np.arange(max_exact, 4 * REL_MAX_DIST).astype(np.float32)
    large = max_exact + (np.log(nf / np.float32(max_exact))
                         / np.float32(math.log(REL_MAX_DIST / max_exact))
                         * np.float32(REL_BUCKETS - max_exact)).astype(np.int32)
    large = np.minimum(large, REL_BUCKETS - 1)
    return [int(nf[np.argmax(large >= max_exact + j)]) for j in range(1, REL_BUCKETS - max_exact)]


def _bias_kernel(rb_ref, out_ref):
    max_exact = REL_BUCKETS // 2
    srow = lax.broadcasted_iota(I32, (LANE, LANE), 0)
    tcol = lax.broadcasted_iota(I32, (LANE, LANE), 1)
    thr = _bucket_thresholds()
    for band in range(2):
        n = jnp.maximum(band * LANE + tcol - srow, 0)
        large = jnp.full((LANE, LANE), max_exact, I32)
        for t in thr:
            large = large + jnp.where(n >= t, 1, 0)
        bucket = jnp.where(n < max_exact, n, large)
        for h in range(A_HEADS):
            acc = jnp.zeros((LANE, LANE), F32)
            for bk in range(REL_BUCKETS):
                acc = acc + jnp.where(bucket == bk, rb_ref[bk, h], 0.0)
            out_ref[h, band] = acc
    for h in range(A_HEADS):
        out_ref[h, 2] = jnp.full((LANE, LANE), rb_ref[REL_BUCKETS - 1, h], F32)


def _bias_tiles(rel_bias):
    return pl.pallas_call(
        _bias_kernel,
        out_shape=jax.ShapeDtypeStruct((A_HEADS, 3, LANE, LANE), F32),
        in_specs=[pl.BlockSpec(memory_space=pltpu.SMEM)],
        name="rel_bias_tiles",
    )(rel_bias)


def _proj_kernel(x_ref, w_ref, *out_refs, widths, chunk):
    xb = x_ref[...].astype(BF16)
    c0 = 0
    for o_ref, width in zip(out_refs, widths):
        for c in range(0, width, chunk):
            ce = min(c + chunk, width)
            o_ref[:, c:ce] = _dot(xb, w_ref[:, c0 + c:c0 + ce]).astype(o_ref.dtype)
        c0 += width


def _in_proj(x2, w, widths, dtypes, tm=512, chunk=256):
    n, d = x2.shape
    return pl.pallas_call(
        functools.partial(_proj_kernel, widths=widths, chunk=chunk),
        out_shape=[jax.ShapeDtypeStruct((n, wd), dt) for wd, dt in zip(widths, dtypes)],
        grid=(n // tm,),
        in_specs=[pl.BlockSpec((tm, d), lambda i: (i, 0)),
                  pl.BlockSpec(w.shape, lambda i: (0, 0))],
        out_specs=[pl.BlockSpec((tm, wd), lambda i: (i, 0)) for wd in widths],
        compiler_params=_params(("arbitrary",)),
        name="in_proj",
    )(x2, w)


def _sortable(x):
    bits = lax.bitcast_convert_type(x, I32)
    return jnp.where(bits < 0, bits ^ jnp.int32(0x7FFFFFFF), bits)


def _dsa_kernel(q_ref, k_ref, v_ref, iq_ref, slab_ref, slabq_ref, bias_ref, o_ref,
                key_ref, am_ref, vt_ref, acc_ref, lg_ref, tmax_ref, ot_ref, *, k_sel, seq):
    i = pl.program_id(1)
    t0 = i * LANE
    nkt = i + 1
    kt2 = 2 * LANE
    n2 = lax.shift_right_logical(nkt + 1, 1)
    rep = A_HEADS // A_KV_HEADS
    srow_2 = lax.broadcasted_iota(I32, (kt2, LANE), 0)
    tcol = t0 + lax.broadcasted_iota(I32, (kt2, LANE), 1)

    @pl.when(i == 0)
    def _():
        for jt in range(seq // kt2):
            vt_ref[jt] = v_ref[jt * kt2:(jt + 1) * kt2, :].astype(F32).T.astype(BF16)

    slab_t = slabq_ref[...].T
    w_t = slab_t[IDX_DIM:IDX_DIM + IDX_HEADS, :] * (IDX_HEADS ** -0.5)
    iqb = iq_ref[...].astype(BF16)
    iq_stack = jnp.concatenate([iqb[:, h * LANE:(h + 1) * LANE] for h in range(IDX_HEADS)], axis=0)

    def score_tile(j, carry):
        s0 = pl.multiple_of(j * kt2, kt2)
        ikt = slab_ref[pl.ds(s0, kt2), :].astype(BF16)
        d = _dot_nt(ikt, iq_stack) * (IDX_DIM ** -0.5)
        acc = jnp.zeros((kt2, LANE), F32)
        for h in range(IDX_HEADS):
            acc = acc + jnp.maximum(d[:, h * LANE:(h + 1) * LANE], 0.0) * w_t[h:h + 1, :]
        acc = jnp.where(s0 + srow_2 <= tcol, acc, NEG_INF)
        key_ref[pl.ds(s0, kt2), :] = _sortable(acc)
        return carry

    lax.fori_loop(0, n2, score_tile, 0)
    key_ref[pl.ds(pl.multiple_of(nkt * LANE, LANE), LANE), :] = jnp.full((LANE, LANE), INT_MIN, I32)

    def count(pred_fn):
        def body(j, c):
            s0 = pl.multiple_of(j * kt2, kt2)
            kt = key_ref[pl.ds(s0, kt2), :]
            m = jnp.where(pred_fn(kt, s0 + srow_2), 1, 0)
            return c + jnp.sum(m.reshape(kt2 // 8, 8, LANE), axis=0)
        c = lax.fori_loop(0, n2, body, jnp.zeros((8, LANE), I32))
        return jnp.sum(c, axis=0, keepdims=True)

    def bcast(v):
        return jnp.broadcast_to(v, (kt2, LANE))

    def search():
        c0 = count(lambda kt, s: kt >= 0)
        ans0 = jnp.where(c0 >= k_sel, 0, INT_MIN).astype(I32)

        def bit_body(bi, ans):
            cand = ans | lax.shift_left(jnp.int32(1), 30 - bi)
            cb = bcast(cand)
            cnt = count(lambda kt, s: kt >= cb)
            return jnp.where(cnt >= k_sel, cand, ans)

        ans = lax.fori_loop(0, 31, bit_body, ans0)
        ab = bcast(ans)
        cnt_ge = count(lambda kt, s: kt >= ab)
        cnt_gt = count(lambda kt, s: kt > ab)
        need = k_sel - cnt_gt

        def tie_search():
            def idx_body(bi, x):
                cand = x | lax.shift_left(jnp.int32(1), (seq.bit_length() - 2) - bi)
                cb = bcast(cand)
                f = count(lambda kt, s: (kt == ab) & (s < cb))
                return jnp.where(f < need, cand, x)
            return lax.fori_loop(0, seq.bit_length() - 1, idx_body, jnp.zeros((1, LANE), I32))

        cut = lax.cond(jnp.max(cnt_ge) > k_sel, tie_search,
                       lambda: jnp.full((1, LANE), seq - 1, I32))
        return ans, cut

    ans, cut = lax.cond(t0 >= k_sel, search,
                        lambda: (jnp.full((1, LANE), INT_MIN, I32),
                                 jnp.full((1, LANE), seq - 1, I32)))
    ans_b = bcast(ans)
    cut_b = bcast(cut)

    def mask_tile(j, carry):
        s0 = pl.multiple_of(j * kt2, kt2)
        kt = key_ref[pl.ds(s0, kt2), :]
        srow = s0 + srow_2
        sel = (kt > ans_b) | ((kt == ans_b) & (srow <= cut_b))
        am_ref[pl.ds(s0, kt2), :] = jnp.where(sel & (srow <= tcol), 0.0, NEG_INF)
        return carry

    lax.fori_loop(0, n2, mask_tile, 0)

    scale = jnp.asarray(A_HEAD_DIM ** -0.5, BF16)
    q_stacks = [jnp.concatenate([q_ref[:, (g * rep + r) * LANE:(g * rep + r + 1) * LANE]
                                 for r in range(rep)], axis=0) * scale for g in range(A_KV_HEADS)]
    acc_ref[...] = jnp.zeros(acc_ref.shape, F32)

    def logits_stage(j):
        slot = j & 1
        s0 = pl.multiple_of(j * kt2, kt2)
        am = am_ref[pl.ds(s0, kt2), :]
        band0 = jnp.clip(i - 2 * j, 0, 2)
        band1 = jnp.clip(i - 2 * j - 1, 0, 2)
        for g in range(A_KV_HEADS):
            kt = k_ref[pl.ds(s0, kt2), g * LANE:(g + 1) * LANE]
            lg_all = _dot_nt(kt, q_stacks[g])
            for r in range(rep):
                h = g * rep + r
                cs = slice(r * LANE, (r + 1) * LANE)
                bias = jnp.concatenate([bias_ref[h, band0], bias_ref[h, band1]], axis=0)
                lg = lg_all[:, cs] + bias + am
                lg_ref[slot, g, :, cs] = lg
                tmax_ref[slot, h:h + 1, :] = jnp.max(lg, axis=0, keepdims=True)

    def softmax_stage(j, carry):
        ms, ls = carry
        slot = j & 1
        new_ms, new_ls = [], []
        for g in range(A_KV_HEADS):
            ps, alphas = [], []
            for r in range(rep):
                h = g * rep + r
                m_new = jnp.maximum(ms[h], tmax_ref[slot, h:h + 1, :])
                m_safe = jnp.where(m_new == NEG_INF, 0.0, m_new)
                alpha = jnp.exp(ms[h] - m_safe)
                p = jnp.exp(lg_ref[slot, g, :, r * LANE:(r + 1) * LANE] - m_safe)
                new_ms.append(m_new)
                new_ls.append(alpha * ls[h] + jnp.sum(p, axis=0, keepdims=True))
                ps.append(p.astype(BF16))
                alphas.append(alpha)
            vt = vt_ref[j, g * A_HEAD_DIM:(g + 1) * A_HEAD_DIM, :]
            pv = _dot(vt, jnp.concatenate(ps, axis=1))
            acc_ref[g] = acc_ref[g] * jnp.concatenate(alphas, axis=1) + pv
        return tuple(new_ms), tuple(new_ls)

    def tile_body(j, carry):
        carry = softmax_stage(j, carry)
        logits_stage(j + 1)
        return carry

    init = (tuple(jnp.full((1, LANE), NEG_INF, F32) for _ in range(A_HEADS)),
            tuple(jnp.zeros((1, LANE), F32) for _ in range(A_HEADS)))
    logits_stage(0)
    carry = lax.fori_loop(0, n2 - 1, tile_body, init)
    _, ls = softmax_stage(n2 - 1, carry)
    for g in range(A_KV_HEADS):
        for r in range(rep):
            h = g * rep + r
            ot_ref[h * A_HEAD_DIM:(h + 1) * A_HEAD_DIM, :] = (
                acc_ref[g, :, r * LANE:(r + 1) * LANE] / ls[h])
    o_ref[...] = ot_ref[...].T.astype(o_ref.dtype)


def _dsa(h_a, h_f, bias_tiles, bsz, seq):
    n = bsz * seq
    nq = seq // LANE
    k_sel = min(TOPK_MAX, seq // 4)
    assert k_sel % LANE == 0 and seq % (2 * LANE) == 0
    qw = A_HEADS * LANE
    slab_blk = (h_f.shape[1] - LANE) // LANE
    return pl.pallas_call(
        functools.partial(_dsa_kernel, k_sel=k_sel, seq=seq),
        out_shape=jax.ShapeDtypeStruct((n, A_HEADS * A_HEAD_DIM), BF16),
        grid=(bsz, nq),
        in_specs=[
            pl.BlockSpec((LANE, qw), lambda b, i: (b * nq + i, 0)),
            pl.BlockSpec((seq, A_KV_HEADS * LANE), lambda b, i: (b, qw // (A_KV_HEADS * LANE))),
            pl.BlockSpec((seq, LANE), lambda b, i: (b, (qw + A_KV_HEADS * LANE) // LANE)),
            pl.BlockSpec((LANE, IDX_HEADS * LANE), lambda b, i: (b * nq + i, 0)),
            pl.BlockSpec((seq, LANE), lambda b, i: (b, slab_blk)),
            pl.BlockSpec((LANE, LANE), lambda b, i: (b * nq + i, slab_blk)),
            pl.BlockSpec((A_HEADS, 3, LANE, LANE), lambda b, i: (0, 0, 0, 0)),
        ],
        out_specs=pl.BlockSpec((LANE, A_HEADS * A_HEAD_DIM), lambda b, i: (b * nq + i, 0)),
        scratch_shapes=[
            pltpu.VMEM((seq + LANE, LANE), I32),
            pltpu.VMEM((seq, LANE), F32),
            pltpu.VMEM((nq // 2, LANE, 2 * LANE), BF16),
            pltpu.VMEM((A_KV_HEADS, A_HEAD_DIM, (A_HEADS // A_KV_HEADS) * LANE), F32),
            pltpu.VMEM((2, A_KV_HEADS, 2 * LANE, (A_HEADS // A_KV_HEADS) * LANE), F32),
            pltpu.VMEM((2, A_HEADS, LANE), F32),
            pltpu.VMEM((A_HEADS * A_HEAD_DIM, LANE), F32),
        ],
        compiler_params=_params(("arbitrary", "arbitrary")),
        name="dsa_attention",
    )(h_a, h_a, h_a, h_f, h_f, h_f, bias_tiles)


def _gla_kernel(bq_ref, bk_ref, bv_ref, br_ref, slab_ref, w2_ref, gb_ref, ng_ref, o_ref,
                st_ref, *, n_chunks):
    @pl.when(pl.program_id(1) == 0)
    def _():
        st_ref[...] = jnp.zeros(st_ref.shape, F32)

    ch = GLA_CHUNK
    rws = n_chunks * ch
    shift = ch.bit_length() - 1
    assert 1 << shift == ch
    row = lax.broadcasted_iota(I32, (rws, rws), 0)
    col = lax.broadcasted_iota(I32, (rws, rws), 1)
    tril = (lax.shift_right_logical(row, shift) == lax.shift_right_logical(col, shift)) & (row >= col)
    tri = jnp.where(tril, 1.0, 0.0).astype(BF16)
    gate = _dot(slab_ref[...].astype(BF16), w2_ref[...]) + gb_ref[...]
    log_a = (jnp.minimum(gate, 0.0) - jnp.log1p(jnp.exp(-jnp.abs(gate)))) / GATE_TAU
    hi, mid, lo = _split3(log_a)
    cum = _dot(tri, hi) + _dot(tri, mid) + _dot(tri, lo)
    lasts = [cum[(c + 1) * ch - 1:(c + 1) * ch, :] for c in range(n_chunks)]
    last = jnp.concatenate([jnp.broadcast_to(l, (ch, l.shape[1])) for l in lasts], axis=0)
    q = bq_ref[...] * (B_KEY_DIM ** -0.5)
    k = bk_ref[...]
    q_dec = (q * jnp.exp(cum)).astype(BF16)
    k_inv = (k * jnp.exp(-cum)).astype(BF16)
    k_end = (k * jnp.exp(last - cum)).astype(BF16)
    decays = [jnp.exp(l) for l in lasts]
    for h in range(B_HEADS):
        cs = slice(h * LANE, (h + 1) * LANE)
        v = bv_ref[:, cs].astype(BF16)
        sc = jnp.where(tril, _dot_nt(q_dec[:, cs], k_inv[:, cs]), 0.0)
        o_intra = _dot(sc.astype(BF16), v)
        st = st_ref[h]
        parts = []
        for c in range(n_chunks):
            rows = slice(c * ch, (c + 1) * ch)
            parts.append(o_intra[rows] + _dot_nt(q_dec[rows, cs], st.astype(BF16)))
            st = decays[c][:, cs] * st + _dot_tn(v[rows], k_end[rows, cs])
        st_ref[h] = st
        o = jnp.concatenate(parts, axis=0)
        mu = jnp.mean(o, axis=-1, keepdims=True)
        dlt = o - mu
        var = jnp.mean(dlt * dlt, axis=-1, keepdims=True)
        on = dlt * lax.rsqrt(var + LN_EPS) * ng_ref[:, cs]
        r = br_ref[:, cs]
        o_ref[:, cs] = (on * (r * _sigmoid(r))).astype(o_ref.dtype)


def _gla(h_f, w2p, gbp, norm_g, bsz, seq, rows_per_step=256):
    n = bsz * seq
    hw = B_HEADS * LANE
    steps = seq // rows_per_step
    slab_blk = (h_f.shape[1] - LANE) // LANE
    blk = lambda cb: pl.BlockSpec((rows_per_step, hw), lambda b, s, cb=cb: (b * steps + s, cb))
    return pl.pallas_call(
        functools.partial(_gla_kernel, n_chunks=rows_per_step // GLA_CHUNK),
        out_shape=jax.ShapeDtypeStruct((n, hw), BF16),
        grid=(bsz, steps),
        in_specs=[blk(1), blk(2), blk(3), blk(4),
                  pl.BlockSpec((rows_per_step, LANE), lambda b, s: (b * steps + s, slab_blk)),
                  pl.BlockSpec((LANE, hw), lambda b, s: (0, 0)),
                  pl.BlockSpec((1, hw), lambda b, s: (0, 0)),
                  pl.BlockSpec((1, hw), lambda b, s: (0, 0))],
        out_specs=pl.BlockSpec((rows_per_step, hw), lambda b, s: (b * steps + s, 0)),
        scratch_shapes=[pltpu.VMEM((B_HEADS, LANE, LANE), F32)],
        compiler_params=_params(("arbitrary", "arbitrary")),
        name="gla_attention",
    )(h_f, h_f, h_f, h_f, h_f, w2p, gbp, norm_g)


def _outproj_ln_kernel(oa_ref, ob_ref, x_ref, w_ref, g_ref, b_ref, out_ref):
    ka = oa_ref.shape[1]
    m = _dot(oa_ref[...], w_ref[0:ka, :]) + _dot(ob_ref[...], w_ref[ka:, :])
    out_ref[...] = _layer_norm(DN_ALPHA * x_ref[...] + m, g_ref[...], b_ref[...])


def _outproj_ln(o_a, o_b, x2, w, g, b, tm=512):
    n, d = x2.shape
    row = lambda width: pl.BlockSpec((tm, width), lambda i: (i, 0))
    full = lambda a: pl.BlockSpec(a.shape, lambda i: (0, 0))
    return pl.pallas_call(
        _outproj_ln_kernel,
        out_shape=jax.ShapeDtypeStruct((n, d), F32),
        grid=(n // tm,),
        in_specs=[row(o_a.shape[1]), row(o_b.shape[1]), row(d), full(w), full(g), full(b)],
        out_specs=row(d),
        compiler_params=_params(("arbitrary",)),
        name="out_proj_ln",
    )(o_a, o_b, x2, w, g, b)


def _route(lt):
    gl = [lt[g:g + 1, :] for g in range(MOE_GROUPS)]
    best, gsel = gl[0], jnp.zeros_like(gl[0], dtype=I32)
    for g in range(1, MOE_GROUPS):
        better = gl[g] > best
        gsel = jnp.where(better, g, gsel)
        best = jnp.where(better, gl[g], best)
    denom = sum(jnp.exp(x - best) for x in gl)
    g_w = 1.0 / denom
    fl = []
    for e in range(MOE_EXPERTS):
        acc = jnp.zeros_like(best)
        for g in range(MOE_GROUPS):
            r = MOE_GROUPS + g * MOE_EXPERTS + e
            acc = jnp.where(gsel == g, lt[r:r + 1, :], acc)
        fl.append(acc)
    v1, i1 = fl[0], jnp.zeros_like(gsel)
    for e in range(1, MOE_EXPERTS):
        better = fl[e] > v1
        i1 = jnp.where(better, e, i1)
        v1 = jnp.where(better, fl[e], v1)
    v2, i2 = jnp.full_like(v1, NEG_INF), jnp.zeros_like(gsel)
    for e in range(MOE_EXPERTS):
        better = (fl[e] > v2) & (i1 != e)
        i2 = jnp.where(better, e, i2)
        v2 = jnp.where(better, fl[e], v2)
    e2 = jnp.exp(v2 - v1)
    w1 = (1.0 / (1.0 + e2)) * g_w
    w2 = (e2 / (1.0 + e2)) * g_w
    e_gate = [jnp.where(i1 == e, w1, 0.0) + jnp.where(i2 == e, w2, 0.0)
              for e in range(MOE_EXPERTS)]
    return gsel, e_gate


def _moe_route_kernel(x_ref, wr_ref, rb_ref, xa_ref, rt_ref, cnt_ref, tri_ref, carry_ref):
    i = pl.program_id(0)
    tm, d = x_ref.shape

    @pl.when(i == 0)
    def _():
        carry_ref[...] = jnp.zeros(carry_ref.shape, F32)
        before = (lax.broadcasted_iota(I32, (tm, tm), 0) < lax.broadcasted_iota(I32, (tm, tm), 1))
        tri_ref[...] = jnp.where(before, 1.0, 0.0).astype(BF16)

    x = x_ref[...]
    xs = _split3(x)
    ws = _split3(wr_ref[...])
    lt = jnp.zeros((wr_ref.shape[0], tm), F32)
    for a, b in ((2, 0), (0, 2), (1, 1), (1, 0), (0, 1), (0, 0)):
        lt = lt + _dot_nt(ws[a], xs[b])
    lt = lt + rb_ref[...]
    gsel, e_gate = _route(lt)

    row8 = lax.broadcasted_iota(I32, (8, tm), 0)
    onehot = jnp.where(row8 == gsel, 1.0, 0.0)
    earlier = _dot(onehot.astype(BF16), tri_ref[...])
    rank = jnp.sum(onehot * (earlier + carry_ref[:, 0:1]), axis=0, keepdims=True)
    rt_ref[...] = jnp.where(row8 == 0, gsel, jnp.where(row8 == 1, rank.astype(I32), 0))
    carry_ref[...] = carry_ref[...] + jnp.sum(onehot, axis=1, keepdims=True)
    cnt_ref[...] = carry_ref[...].astype(I32)

    erow = lax.broadcasted_iota(I32, (LANE, tm), 0)
    gt = jnp.zeros((LANE, tm), F32)
    for e in range(MOE_EXPERTS):
        gt = jnp.where(erow == e, e_gate[e], gt)
    xa_ref[:, :d] = x
    xa_ref[:, d:] = gt.T


def _moe_route(h, wr, rb, tm=1024):
    n, d = h.shape
    tm = min(tm, n)
    full = lambda a: pl.BlockSpec(a.shape, lambda i: (0,) * a.ndim)
    return pl.pallas_call(
        _moe_route_kernel,
        out_shape=[jax.ShapeDtypeStruct((n, d + LANE), F32), jax.ShapeDtypeStruct((8, n), I32),
                   jax.ShapeDtypeStruct((8, LANE), I32)],
        grid=(n // tm,),
        in_specs=[pl.BlockSpec((tm, d), lambda i: (i, 0)), full(wr), full(rb)],
        out_specs=[pl.BlockSpec((tm, d + LANE), lambda i: (i, 0)),
                   pl.BlockSpec((8, tm), lambda i: (0, i)),
                   pl.BlockSpec((8, LANE), lambda i: (0, 0))],
        scratch_shapes=[pltpu.VMEM((tm, tm), BF16), pltpu.VMEM((8, LANE), F32)],
        compiler_params=_params(("arbitrary",)),
        name="moe_route",
    )(h, wr, rb)


def _row_copy_waves(n_rows, wave, start_row, wait_wave):
    n_waves = n_rows // wave
    for w in range(n_waves + 1):
        if w < n_waves:
            lax.fori_loop(w * wave, (w + 1) * wave, lambda r, c, w=w: (start_row(r, w % 2), c)[1], 0,
                          unroll=8)
        if w >= 1:
            wait_wave((w - 1) % 2)


def _moe_dispatch_kernel(pos_ref, fill_ref, xa_ref, xs_hbm, zero_ref, sem, *, wave):
    i = pl.program_id(0)
    tm = pos_ref.shape[1]
    tme = zero_ref.shape[0]

    def copy(r, s):
        return pltpu.make_async_copy(xa_ref.at[r], xs_hbm.at[pos_ref[0, r]], sem.at[s])

    def wait_wave(s):
        rows = pl.ds(0, wave)
        pltpu.make_async_copy(xa_ref.at[rows], xs_hbm.at[rows], sem.at[s]).wait()

    _row_copy_waves(tm, wave, lambda r, s: copy(r, s).start(), wait_wave)

    @pl.when(i == pl.num_programs(0) - 1)
    def _():
        zero_ref[...] = jnp.zeros(zero_ref.shape, F32)
        for g in range(MOE_GROUPS):
            first, count = fill_ref[0, g], fill_ref[1, g]

            def zcopy(r, first=first):
                return pltpu.make_async_copy(zero_ref.at[0], xs_hbm.at[first + r], sem.at[0])

            lax.fori_loop(0, count, lambda r, c: (zcopy(r).start(), c)[1], 0)
            lax.fori_loop(0, count, lambda r, c: (zcopy(r).wait(), c)[1], 0)
        n_tiles = fill_ref[2, 0]
        for k in range(MOE_GROUPS):
            @pl.when(k < fill_ref[2, 1])
            def _(k=k):
                tail = xs_hbm.at[pl.ds(pl.multiple_of((n_tiles + k) * tme, tme), tme)]
                cp = pltpu.make_async_copy(zero_ref, tail, sem.at[0])
                cp.start()
                cp.wait()


def _moe_dispatch(xa, pos, fill, n_sorted, tme, tm=1024, wave=256):
    n, da = xa.shape
    tm = min(tm, n)
    wave = min(wave, tm)
    return pl.pallas_call(
        functools.partial(_moe_dispatch_kernel, wave=wave),
        out_shape=jax.ShapeDtypeStruct((n_sorted, da), F32),
        grid=(n // tm,),
        in_specs=[pl.BlockSpec((1, tm), lambda i: (0, i), memory_space=pltpu.SMEM),
                  pl.BlockSpec(memory_space=pltpu.SMEM),
                  pl.BlockSpec((tm, da), lambda i: (i, 0))],
        out_specs=pl.BlockSpec(memory_space=pl.ANY),
        scratch_shapes=[pltpu.VMEM((tme, da), F32), pltpu.SemaphoreType.DMA((2,))],
        compiler_params=_params(("arbitrary",)),
        name="moe_dispatch",
    )(pos, fill, xa)


def _moe_expert_kernel(tg_ref, tj_ref, nt_ref, xs_ref, wg_ref, wu_ref, wd_ref, lg_ref, lb_ref,
                       ys_ref, wgb_ref, wub_ref, wdb_ref):
    d = ys_ref.shape[1]
    i = pl.program_id(0)

    @pl.when((i == 0) | (tg_ref[i] != tg_ref[jnp.maximum(i - 1, 0)]))
    def _():
        for e in range(MOE_EXPERTS):
            wgb_ref[e] = wg_ref[e].astype(BF16)
            wub_ref[e] = wu_ref[e].astype(BF16)
            wdb_ref[e] = wd_ref[e].astype(BF16)

    @pl.when(i < nt_ref[0])
    def _():
        x = xs_ref[:, :d]
        gates = xs_ref[:, d:]
        xb = x.astype(BF16)
        y = jnp.zeros(x.shape, F32)
        for e in range(MOE_EXPERTS):
            hg = _dot(xb, wgb_ref[e])
            hu = _dot(xb, wub_ref[e])
            hid = hg * _sigmoid(hg) * hu * gates[:, e:e + 1]
            y = y + _dot(hid.astype(BF16), wdb_ref[e])
        ys_ref[...] = _layer_norm(DN_ALPHA * x + y, lg_ref[...], lb_ref[...])

    @pl.when(pl.program_id(0) >= nt_ref[0])
    def _():
        ys_ref[...] = jnp.zeros(ys_ref.shape, F32)


def _moe_experts(xs, tile_g, tile_j, n_tiles, layer, wg, wu, wd, lg, lb, tme):
    p, da = xs.shape
    d = da - LANE
    grp = lambda w: pl.BlockSpec((None, None) + w.shape[2:],
                                 lambda i, tg, tj, nt: (layer, tg[i], 0, 0, 0),
                                 pipeline_mode=pl.Buffered(1))
    full = lambda a: pl.BlockSpec(a.shape, lambda i, tg, tj, nt: (0,) * a.ndim)
    grid_spec = pltpu.PrefetchScalarGridSpec(
        num_scalar_prefetch=3,
        grid=(p // tme,),
        in_specs=[pl.BlockSpec((tme, da), lambda i, tg, tj, nt: (tj[i], 0)),
                  grp(wg), grp(wu), grp(wd), full(lg), full(lb)],
        out_specs=pl.BlockSpec((tme, d), lambda i, tg, tj, nt: (i, 0)),
        scratch_shapes=[pltpu.VMEM(w.shape[2:], BF16) for w in (wg, wu, wd)],
    )
    return pl.pallas_call(
        _moe_expert_kernel,
        out_shape=jax.ShapeDtypeStruct((p, d), F32),
        grid_spec=grid_spec,
        compiler_params=pltpu.CompilerParams(dimension_semantics=("arbitrary",),
                                             vmem_limit_bytes=MOE_VMEM_LIMIT),
        name="moe_experts",
    )(tile_g, tile_j, n_tiles, xs, wg, wu, wd, lg, lb)


def _moe_combine_kernel(pos_ref, ys_hbm, o_ref, sem, *, wave):
    tm = o_ref.shape[0]

    def copy(r, s):
        return pltpu.make_async_copy(ys_hbm.at[pos_ref[0, r]], o_ref.at[r], sem.at[s])

    def wait_wave(s):
        rows = pl.ds(0, wave)
        pltpu.make_async_copy(ys_hbm.at[rows], o_ref.at[rows], sem.at[s]).wait()

    _row_copy_waves(tm, wave, lambda r, s: copy(r, s).start(), wait_wave)


def _moe_combine(ys, pos, n, tm=1024, wave=256):
    d = ys.shape[1]
    tm = min(tm, n)
    wave = min(wave, tm)
    return pl.pallas_call(
        functools.partial(_moe_combine_kernel, wave=wave),
        out_shape=jax.ShapeDtypeStruct((n, d), F32),
        grid=(n // tm,),
        in_specs=[pl.BlockSpec((1, tm), lambda i: (0, i), memory_space=pltpu.SMEM),
                  pl.BlockSpec(memory_space=pl.ANY)],
        out_specs=pl.BlockSpec((tm, d), lambda i: (i, 0)),
        scratch_shapes=[pltpu.SemaphoreType.DMA((2,))],
        compiler_params=_params(("arbitrary",)),
        name="moe_combine",
    )(pos, ys)


def _moe(h, wr, rb, layer, wg, wu, wd, lg, lb, regroup=None, tme=512):
    n, d = h.shape
    tme = min(tme, n)
    xa, rt, cnt = _moe_route(h, wr, rb)
    counts = cnt[:MOE_GROUPS, 0]
    tiles = (counts + tme - 1) // tme
    first_tile = jnp.cumsum(tiles) - tiles
    gid, rank = rt[0], rt[1]
    pos = (first_tile[gid] * tme + rank)[None, :]
    n_slots = n // tme + MOE_GROUPS
    n_tiles = jnp.sum(tiles)
    fill = jnp.stack([first_tile * tme + counts, tiles * tme - counts,
                      jnp.zeros((MOE_GROUPS,), I32).at[0].set(n_tiles).at[1].set(n_slots - n_tiles)])
    slot = jnp.minimum(jnp.arange(n_slots, dtype=I32), n_tiles - 1)
    tile_g = jnp.sum(slot[:, None] >= (first_tile + tiles)[None, :], axis=1).astype(I32)
    xs = _moe_dispatch(xa, pos, fill, n_slots * tme, tme)
    ys = _moe_experts(xs, tile_g, slot, n_tiles[None].astype(I32), layer, wg, wu, wd, lg, lb, tme)
    if regroup is not None:
        pos = pos.reshape(regroup).T.reshape(1, n)
    return _moe_combine(ys, pos, n)


def _s5_prep_kernel(lre_ref, lim_ref, ldt_ref, bre_ref, bim_ref, are_ref, aim_ref, bbre_ref,
                    bbim_ref):
    lr = jnp.minimum(lre_ref[...], -1e-4)
    li = lim_ref[...]
    dt = jnp.exp(ldt_ref[...])
    mag = jnp.exp(lr * dt)
    ab_re = mag * jnp.cos(li * dt)
    ab_im = mag * jnp.sin(li * dt)
    den = lr * lr + li * li
    nr = ab_re - 1.0
    coef_re = (nr * lr + ab_im * li) / den
    coef_im = (ab_im * lr - nr * li) / den
    are_ref[...] = ab_re
    aim_ref[...] = ab_im
    bbre_ref[...] = coef_re * bre_ref[...] - coef_im * bim_ref[...]
    bbim_ref[...] = coef_re * bim_ref[...] + coef_im * bre_ref[...]


def _s5_prep(lam_re, lam_im, log_dt, b_re, b_im):
    gp = C_GROUPS * C_STATE
    col = lambda a: a.reshape(gp, 1)
    ldt = jnp.broadcast_to(log_dt[:, None], (C_GROUPS, C_STATE))
    outs = pl.pallas_call(
        _s5_prep_kernel,
        out_shape=[jax.ShapeDtypeStruct((gp, 1), F32), jax.ShapeDtypeStruct((gp, 1), F32),
                   jax.ShapeDtypeStruct((gp, C_GROUP), F32), jax.ShapeDtypeStruct((gp, C_GROUP), F32)],
        name="s5_discretise",
    )(col(lam_re), col(lam_im), col(ldt), b_re.reshape(gp, C_GROUP), b_im.reshape(gp, C_GROUP))
    ab_re, ab_im, bb_re, bb_im = outs
    return (ab_re.reshape(C_GROUPS, C_STATE), ab_im.reshape(C_GROUPS, C_STATE),
            bb_re.reshape(C_GROUPS, C_STATE, C_GROUP), bb_im.reshape(C_GROUPS, C_STATE, C_GROUP))


def _s5_uproj_kernel(x_ref, w_ref, o_ref):
    o_ref[...] = _dot(x_ref[...].astype(BF16), w_ref[...])


def _s5_uproj(h2, w, tm=512):
    n, d = h2.shape
    return pl.pallas_call(
        _s5_uproj_kernel,
        out_shape=jax.ShapeDtypeStruct((n, d), F32),
        grid=(n // tm,),
        in_specs=[pl.BlockSpec((tm, d), lambda i: (i, 0)), pl.BlockSpec(w.shape, lambda i: (0, 0))],
        out_specs=pl.BlockSpec((tm, d), lambda i: (i, 0)),
        compiler_params=_params(("arbitrary",)),
        name="s5_in_proj",
    )(h2, w)


def _gelu_tanh(x):
    return 0.5 * x * (1.0 + jnp.tanh(math.sqrt(2.0 / math.pi) * (x + 0.044715 * (x * x * x))))


def _s5_scan_kernel(u_ref, bbre_ref, bbim_ref, are_ref, aim_ref, cm_ref, d_ref, g_ref,
                    st_ref, bure_ref, buim_ref, s_ref, *, bsz, tl, n_cb, sub):
    @pl.when(pl.program_id(0) == 0)
    def _():
        st_ref[...] = jnp.zeros(st_ref.shape, F32)

    cw = D_MODEL // n_cb
    sw = bure_ref.shape[2]

    def input_states(cb):
        ub = u_ref[:, cb * cw:(cb + 1) * cw].astype(BF16)
        bure_ref[cb % 2] = _dot(ub, bbre_ref[cb])
        buim_ref[cb % 2] = _dot(ub, bbim_ref[cb])

    input_states(0)
    for cb in range(n_cb):
        buf = cb % 2
        if cb + 1 < n_cb:
            input_states(cb + 1)
        for s0 in range(0, sw, sub):
            cs = slice(s0, s0 + sub)
            a_re = jnp.broadcast_to(are_ref[cb, :, cs], (bsz, sub))
            a_im = jnp.broadcast_to(aim_ref[cb, :, cs], (bsz, sub))
            s_re, s_im = st_ref[cb, 0, :, cs], st_ref[cb, 1, :, cs]
            for t in range(tl):
                rows = slice(t * bsz, (t + 1) * bsz)
                s_re, s_im = (a_re * s_re - a_im * s_im + bure_ref[buf, rows, cs],
                              a_re * s_im + a_im * s_re + buim_ref[buf, rows, cs])
                s_ref[buf, rows, cs] = s_re.astype(BF16)
                s_ref[buf, rows, sw + s0:sw + s0 + sub] = s_im.astype(BF16)
            st_ref[cb, 0, :, cs] = s_re
            st_ref[cb, 1, :, cs] = s_im
        ch = slice(cb * cw, (cb + 1) * cw)
        y = _dot(s_ref[buf], cm_ref[cb]) + d_ref[:, ch] * u_ref[:, ch]
        g_ref[:, ch] = _gelu_tanh(y).astype(g_ref.dtype)


def _s5_scan(u_lb, bbre, bbim, a_re, a_im, cmat, d_skip, bsz, seq, tl=32, sub=512):
    n, d = u_lb.shape
    n_cb, cw, sw = bbre.shape
    rows = tl * bsz
    full = lambda a: pl.BlockSpec(a.shape, lambda t: (0,) * a.ndim)
    return pl.pallas_call(
        functools.partial(_s5_scan_kernel, bsz=bsz, tl=tl, n_cb=n_cb, sub=sub),
        out_shape=jax.ShapeDtypeStruct((n, d), BF16),
        grid=(seq // tl,),
        in_specs=[pl.BlockSpec((rows, d), lambda t: (t, 0)), full(bbre), full(bbim), full(a_re),
                  full(a_im), full(cmat), full(d_skip)],
        out_specs=pl.BlockSpec((rows, d), lambda t: (t, 0)),
        scratch_shapes=[pltpu.VMEM((n_cb, 2, bsz, sw), F32), pltpu.VMEM((2, rows, sw), F32),
                        pltpu.VMEM((2, rows, sw), F32), pltpu.VMEM((2, rows, 2 * sw), BF16)],
        compiler_params=_params(("arbitrary",)),
        name="s5_scan",
    )(u_lb, bbre, bbim, a_re, a_im, cmat, d_skip)


def _s5_glu_kernel(g_ref, x_ref, w1_ref, w2_ref, wo_ref, lg_ref, lb_ref, out_ref):
    g = g_ref[...]
    z = _dot(g, w1_ref[...]) * _sigmoid(_dot(g, w2_ref[...]))
    m = _dot(z.astype(BF16), wo_ref[...])
    out_ref[...] = _layer_norm(DN_ALPHA * x_ref[...] + m, lg_ref[...], lb_ref[...])


def _s5_glu(g, h2, w1, w2, wo, lg, lb, tm=512):
    n, d = h2.shape
    row = pl.BlockSpec((tm, d), lambda i: (i, 0))
    full = lambda a: pl.BlockSpec(a.shape, lambda i: (0, 0))
    return pl.pallas_call(
        _s5_glu_kernel,
        out_shape=jax.ShapeDtypeStruct((n, d), F32),
        grid=(n // tm,),
        in_specs=[row, row, full(w1), full(w2), full(wo), full(lg), full(lb)],
        out_specs=row,
        compiler_params=_params(("arbitrary",)),
        name="s5_glu_out_ln",
    )(g, h2, w1, w2, wo, lg, lb)


def _pad_heads(w, heads, dim):
    d = w.shape[0]
    w = w.reshape(d, heads, dim)
    return jnp.pad(w, ((0, 0), (0, 0), (0, LANE - dim))).reshape(d, heads * LANE)


def _even_layer_weights(w_in, gate_w2, gate_b):
    splits = (A_HEADS * A_HEAD_DIM, A_KV_HEADS * A_HEAD_DIM, A_KV_HEADS * A_HEAD_DIM,
              IDX_HEADS * IDX_DIM, IDX_DIM, IDX_HEADS,
              B_HEADS * B_KEY_DIM, B_HEADS * B_KEY_DIM, B_HEADS * B_VAL_DIM, GATE_RANK,
              B_HEADS * B_VAL_DIM)
    offs = np.cumsum((0,) + splits)
    aq, ak, av, iq, ik, iw, bq, bk, bv, bg, br = [w_in[:, offs[k]:offs[k + 1]] for k in range(11)]
    d = w_in.shape[0]
    slab = jnp.concatenate(
        [ik, iw, bg, jnp.zeros((d, LANE - IDX_DIM - IDX_HEADS - GATE_RANK), w_in.dtype)], axis=1)
    w_a = jnp.concatenate([_pad_heads(aq, A_HEADS, A_HEAD_DIM),
                           _pad_heads(ak, A_KV_HEADS, A_HEAD_DIM), av], axis=1)
    w_f = jnp.concatenate([_pad_heads(iq, IDX_HEADS, IDX_DIM), _pad_heads(bq, B_HEADS, B_KEY_DIM),
                           _pad_heads(bk, B_HEADS, B_KEY_DIM), bv, br, slab], axis=1)
    w_all = jnp.concatenate([w_a, w_f], axis=1).astype(BF16)
    g0 = IDX_DIM + IDX_HEADS
    w2p = jnp.zeros((LANE, B_HEADS * LANE), F32).at[g0:g0 + GATE_RANK].set(
        _pad_heads(gate_w2, B_HEADS, B_KEY_DIM)).astype(BF16)
    gbp = _pad_heads(gate_b[None, :], B_HEADS, B_KEY_DIM)
    return w_all, (w_a.shape[1], w_f.shape[1]), w2p, gbp


def _even_mixer_ln(x2, bsz, seq, w_in, bias_tiles, gate_w2, gate_b, norm_g, w_out, ln_g, ln_b):
    w_all, widths, w2p, gbp = _even_layer_weights(w_in, gate_w2, gate_b)
    h_a, h_f = _in_proj(x2, w_all, widths, (BF16, F32))
    o_a = _dsa(h_a, h_f, bias_tiles, bsz, seq)
    o_b = _gla(h_f, w2p, gbp, norm_g[None, :], bsz, seq)
    return _outproj_ln(o_a, o_b, x2, w_out.astype(BF16), ln_g[None, :], ln_b[None, :])


def _block_diag(m, per):
    g, r, c = m.shape
    eye = jnp.eye(per, dtype=m.dtype)
    m = m.reshape(g // per, per, r, c)
    bd = m[:, :, :, None, :] * eye[None, :, None, :, None]
    return bd.reshape(g // per, per * r, per * c)


def _s5_mixer_ln(h2, bsz, seq, w_in, lam_re, lam_im, log_dt, b_re, b_im, c_re, c_im, d_skip,
                 glu_w1, glu_w2, w_out, ln_g, ln_b, groups_per_block=16):
    ab_re, ab_im, bb_re, bb_im = _s5_prep(lam_re, lam_im, log_dt, b_re, b_im)
    per = groups_per_block
    n_cb = C_GROUPS // per
    bbre = _block_diag(jnp.swapaxes(bb_re, 1, 2), per).astype(BF16)
    bbim = _block_diag(jnp.swapaxes(bb_im, 1, 2), per).astype(BF16)
    a_re = ab_re.reshape(n_cb, 1, per * C_STATE)
    a_im = ab_im.reshape(n_cb, 1, per * C_STATE)
    cre = _block_diag(jnp.swapaxes(c_re, 1, 2), per)
    cim = _block_diag(jnp.swapaxes(c_im, 1, 2), per)
    cmat = jnp.concatenate([cre, -cim], axis=1).astype(BF16)
    u = _s5_uproj(h2, w_in.astype(BF16))
    g = _s5_scan(u, bbre, bbim, a_re, a_im, cmat, d_skip[None, :], bsz, seq)
    return _s5_glu(g, h2, glu_w1.astype(BF16), glu_w2.astype(BF16), w_out.astype(BF16),
                   ln_g[None, :], ln_b[None, :])


def _moe_ln(h, layer, r_coarse, rb_coarse, r_fine, rb_fine, w_gate, w_up, w_down, ln_g, ln_b,
            regroup=None):
    d = h.shape[1]
    rows = MOE_GROUPS + N_EXPERTS
    wr = jnp.concatenate([r_coarse.T, jnp.transpose(r_fine, (0, 2, 1)).reshape(N_EXPERTS, d),
                          jnp.zeros((32 - rows, d), F32)], axis=0)
    rb = jnp.concatenate([rb_coarse, rb_fine.reshape(N_EXPERTS), jnp.zeros((32 - rows,), F32)])
    return _moe(h, wr, rb[:, None], layer, w_gate, w_up, w_down, ln_g[None, :], ln_b[None, :],
                regroup)


def kernel(x, rel_bias, ab_w_in, gla_gate_w2, gla_gate_b, gla_norm_g, ab_w_out, s5_w_in, s5_lam_re, s5_lam_im, s5_log_dt, s5_b_re, s5_b_im, s5_c_re, s5_c_im, s5_d, s5_glu_w1, s5_glu_w2, s5_w_out, ln_mix_g, ln_mix_b, ln_ffn_g, ln_ffn_b, moe_r_coarse, moe_rb_coarse, moe_r_fine, moe_rb_fine, moe_w_gate, moe_w_up, moe_w_down):
    bsz, seq, d = x.shape
    h = x.reshape(bsz * seq, d)
    bias_tiles = _bias_tiles(rel_bias)
    time_major = False
    for layer in range(DEPTH):
        i = layer // 2
        assert time_major == (layer % 2 == 1)
        if layer % 2 == 0:
            h = _even_mixer_ln(h, bsz, seq, ab_w_in[i], bias_tiles, gla_gate_w2[i], gla_gate_b[i],
                               gla_norm_g[i], ab_w_out[i], ln_mix_g[layer], ln_mix_b[layer])
        else:
            h = _s5_mixer_ln(h, bsz, seq, s5_w_in[i], s5_lam_re[i], s5_lam_im[i], s5_log_dt[i],
                             s5_b_re[i], s5_b_im[i], s5_c_re[i], s5_c_im[i], s5_d[i],
                             s5_glu_w1[i], s5_glu_w2[i], s5_w_out[i],
                             ln_mix_g[layer], ln_mix_b[layer])
        want_time_major = layer + 1 < DEPTH and (layer + 1) % 2 == 1
        regroup = None
        if want_time_major != time_major:
            regroup = (seq, bsz) if time_major else (bsz, seq)
        h = _moe_ln(h, layer, moe_r_coarse[layer], moe_rb_coarse[layer], moe_r_fine[layer],
                    moe_rb_fine[layer], moe_w_gate, moe_w_up, moe_w_down,
                    ln_ffn_g[layer], ln_ffn_b[layer], regroup)
        time_major = want_time_major
    return h.reshape(bsz, seq, d).astype(x.dtype)
```

```python
import functools
import math

import numpy as np
import jax
import jax.numpy as jnp
from jax import lax
from jax.experimental import pallas as pl
from jax.experimental.pallas import tpu as pltpu

F32 = jnp.float32
BF16 = jnp.bfloat16
I32 = jnp.int32

D_MODEL = 1024
DEPTH = 2
DN_ALPHA = (2.0 * DEPTH) ** 0.25
LN_EPS = 1e-5
A_HEAD_DIM = 64
A_HEADS = 8
A_KV_HEADS = 2
IDX_HEADS = 4
IDX_DIM = 64
TOPK_MAX = 256
REL_BUCKETS = 32
REL_MAX_DIST = 128
B_HEADS = 4
B_VAL_DIM = 128
B_KEY_DIM = 64
GATE_RANK = 16
GATE_TAU = 16.0
GLA_CHUNK = 64
C_GROUP = 16
C_GROUPS = 64
C_STATE = 64
MOE_GROUPS = 4
MOE_EXPERTS = 4
MOE_HIDDEN = 512
N_EXPERTS = MOE_GROUPS * MOE_EXPERTS

LANE = 128
VMEM_LIMIT = 52 * 1024 * 1024
MOE_VMEM_LIMIT = 58 * 1024 * 1024
NEG_INF = float("-inf")
INT_MIN = -(2 ** 31)

_NT = (((1,), (1,)), ((), ()))
_TN = (((0,), (0,)), ((), ()))


def _dot(a, b):
    return jnp.dot(a, b, preferred_element_type=F32)


def _dot_nt(a, b):
    return lax.dot_general(a, b, _NT, preferred_element_type=F32)


def _dot_tn(a, b):
    return lax.dot_general(a, b, _TN, preferred_element_type=F32)


def _split3(x):
    hi = x.astype(BF16)
    r1 = x - hi.astype(F32)
    mid = r1.astype(BF16)
    lo = (r1 - mid.astype(F32)).astype(BF16)
    return hi, mid, lo


def _params(sem):
    return pltpu.CompilerParams(dimension_semantics=sem, vmem_limit_bytes=VMEM_LIMIT)


def _layer_norm(y, g, b):
    mu = jnp.mean(y, axis=-1, keepdims=True)
    d = y - mu
    var = jnp.mean(d * d, axis=-1, keepdims=True)
    return d * lax.rsqrt(var + LN_EPS) * g + b


def _sigmoid(x):
    return 1.0 / (1.0 + jnp.exp(-x))


def _bucket_thresholds():
    max_exact = REL_BUCKETS // 2
    nf = np.arange(max_exact, 4 * REL_MAX_DIST).astype(np.float32)
    large = max_exact + (np.log(nf / np.float32(max_exact))
                         / np.float32(math.log(REL_MAX_DIST / max_exact))
                         * np.float32(REL_BUCKETS - max_exact)).astype(np.int32)
    large = np.minimum(large, REL_BUCKETS - 1)
    return [int(nf[np.argmax(large >= max_exact + j)]) for j in range(1, REL_BUCKETS - max_exact)]


def _bias_kernel(rb_ref, out_ref):
    max_exact = REL_BUCKETS // 2
    srow = lax.broadcasted_iota(I32, (LANE, LANE), 0)
    tcol = lax.broadcasted_iota(I32, (LANE, LANE), 1)
    thr = _bucket_thresholds()
    for band in range(2):
        n = jnp.maximum(band * LANE + tcol - srow, 0)
        large = jnp.full((LANE, LANE), max_exact, I32)
        for t in thr:
            large = large + jnp.where(n >= t, 1, 0)
        bucket = jnp.where(n < max_exact, n, large)
        for h in range(A_HEADS):
            acc = jnp.zeros((LANE, LANE), F32)
            for bk in range(REL_BUCKETS):
                acc = acc + jnp.where(bucket == bk, rb_ref[bk, h], 0.0)
            out_ref[h, band] = acc
    for h in range(A_HEADS):
        out_ref[h, 2] = jnp.full((LANE, LANE), rb_ref[REL_BUCKETS - 1, h], F32)


def _bias_tiles(rel_bias):
    return pl.pallas_call(
        _bias_kernel,
        out_shape=jax.ShapeDtypeStruct((A_HEADS, 3, LANE, LANE), F32),
        in_specs=[pl.BlockSpec(memory_space=pltpu.SMEM)],
        name="rel_bias_tiles",
    )(rel_bias)


def _proj_kernel(x_ref, w_ref, *out_refs, widths, chunk):
    xb = x_ref[...].astype(BF16)
    c0 = 0
    for o_ref, width in zip(out_refs, widths):
        for c in range(0, width, chunk):
            ce = min(c + chunk, width)
            o_ref[:, c:ce] = _dot(xb, w_ref[:, c0 + c:c0 + ce]).astype(o_ref.dtype)
        c0 += width


def _in_proj(x2, w, widths, dtypes, tm=512, chunk=256):
    n, d = x2.shape
    return pl.pallas_call(
        functools.partial(_proj_kernel, widths=widths, chunk=chunk),
        out_shape=[jax.ShapeDtypeStruct((n, wd), dt) for wd, dt in zip(widths, dtypes)],
        grid=(n // tm,),
        in_specs=[pl.BlockSpec((tm, d), lambda i: (i, 0)),
                  pl.BlockSpec(w.shape, lambda i: (0, 0))],
        out_specs=[pl.BlockSpec((tm, wd), lambda i: (i, 0)) for wd in widths],
        compiler_params=_params(("arbitrary",)),
        name="in_proj",
    )(x2, w)


def _sortable(x):
    bits = lax.bitcast_convert_type(x, I32)
    return jnp.where(bits < 0, bits ^ jnp.int32(0x7FFFFFFF), bits)


def _dsa_kernel(q_ref, k_ref, v_ref, iq_ref, slab_ref, slabq_ref, bias_ref, o_ref,
                key_ref, am_ref, vt_ref, acc_ref, lg_ref, tmax_ref, ot_ref, *, k_sel, seq):
    i = pl.program_id(1)
    t0 = i * LANE
    nkt = i + 1
    kt2 = 2 * LANE
    n2 = lax.shift_right_logical(nkt + 1, 1)
    rep = A_HEADS // A_KV_HEADS
    srow_2 = lax.broadcasted_iota(I32, (kt2, LANE), 0)
    tcol = t0 + lax.broadcasted_iota(I32, (kt2, LANE), 1)

    @pl.when(i == 0)
    def _():
        for jt in range(seq // kt2):
            vt_ref[jt] = v_ref[jt * kt2:(jt + 1) * kt2, :].astype(F32).T.astype(BF16)

    slab_t = slabq_ref[...].T
    w_t = slab_t[IDX_DIM:IDX_DIM + IDX_HEADS, :] * (IDX_HEADS ** -0.5) * (IDX_DIM ** -0.5)
    iqb = iq_ref[...].astype(BF16)
    iq_stack = jnp.concatenate([iqb[:, h * LANE:(h + 1) * LANE] for h in range(IDX_HEADS)], axis=0)

    def score_tile(j, carry):
        s0 = pl.multiple_of(j * kt2, kt2)
        ikt = slab_ref[pl.ds(s0, kt2), :].astype(BF16)
        d = _dot_nt(ikt, iq_stack)
        acc = jnp.zeros((kt2, LANE), F32)
        for h in range(IDX_HEADS):
            acc = acc + jnp.maximum(d[:, h * LANE:(h + 1) * LANE], 0.0) * w_t[h:h + 1, :]
        acc = jnp.where(s0 + srow_2 <= tcol, acc, NEG_INF)
        key_ref[pl.ds(s0, kt2), :] = _sortable(acc)
        return carry

    lax.fori_loop(0, n2, score_tile, 0)
    key_ref[pl.ds(pl.multiple_of(nkt * LANE, LANE), LANE), :] = jnp.full((LANE, LANE), INT_MIN, I32)

    def count(pred_fn):
        def variant(n_chunks):
            def run():
                c = jnp.zeros((8, LANE), I32)
                for j in range(n_chunks):
                    kt = key_ref[j * kt2:(j + 1) * kt2, :]
                    m = jnp.where(pred_fn(kt, j * kt2 + srow_2), 1, 0)
                    c = c + jnp.sum(m.reshape(kt2 // 8, 8, LANE), axis=0)
                return jnp.sum(c, axis=0, keepdims=True)
            return run
        return lax.switch(n2 - 1, [variant(m) for m in range(1, seq // kt2 + 1)])

    def bcast(v):
        return jnp.broadcast_to(v, (kt2, LANE))

    def search():
        c0 = count(lambda kt, s: kt >= 0)
        ans0 = jnp.where(c0 >= k_sel, 0, INT_MIN).astype(I32)

        def bit_body(bi, ans):
            cand = ans | lax.shift_left(jnp.int32(1), 30 - bi)
            cb = bcast(cand)
            cnt = count(lambda kt, s: kt >= cb)
            return jnp.where(cnt >= k_sel, cand, ans)

        ans = lax.fori_loop(0, 31, bit_body, ans0)
        ab = bcast(ans)
        cnt_ge = count(lambda kt, s: kt >= ab)
        cnt_gt = count(lambda kt, s: kt > ab)
        need = k_sel - cnt_gt

        def tie_search():
            def idx_body(bi, x):
                cand = x | lax.shift_left(jnp.int32(1), (seq.bit_length() - 2) - bi)
                cb = bcast(cand)
                f = count(lambda kt, s: (kt == ab) & (s < cb))
                return jnp.where(f < need, cand, x)
            return lax.fori_loop(0, seq.bit_length() - 1, idx_body, jnp.zeros((1, LANE), I32))

        cut = lax.cond(jnp.max(cnt_ge) > k_sel, tie_search,
                       lambda: jnp.full((1, LANE), seq - 1, I32))
        return ans, cut

    ans, cut = lax.cond(t0 >= k_sel, search,
                        lambda: (jnp.full((1, LANE), INT_MIN, I32),
                                 jnp.full((1, LANE), seq - 1, I32)))
    ans_b = bcast(ans)
    cut_b = bcast(cut)

    def mask_tile(j, carry):
        s0 = pl.multiple_of(j * kt2, kt2)
        kt = key_ref[pl.ds(s0, kt2), :]
        srow = s0 + srow_2
        sel = (kt > ans_b) | ((kt == ans_b) & (srow <= cut_b))
        am_ref[pl.ds(s0, kt2), :] = jnp.where(sel & (srow <= tcol), 0.0, NEG_INF)
        return carry

    lax.fori_loop(0, n2, mask_tile, 0)

    scale = jnp.asarray(A_HEAD_DIM ** -0.5, BF16)
    q_stacks = [jnp.concatenate([q_ref[:, (g * rep + r) * LANE:(g * rep + r + 1) * LANE]
                                 for r in range(rep)], axis=0) * scale for g in range(A_KV_HEADS)]
    acc_ref[...] = jnp.zeros(acc_ref.shape, F32)

    def logits_stage(j):
        slot = j & 1
        s0 = pl.multiple_of(j * kt2, kt2)
        am = am_ref[pl.ds(s0, kt2), :]
        band0 = jnp.clip(i - 2 * j, 0, 2)
        band1 = jnp.clip(i - 2 * j - 1, 0, 2)
        for g in range(A_KV_HEADS):
            kt = k_ref[pl.ds(s0, kt2), g * LANE:(g + 1) * LANE]
            lg_all = _dot_nt(kt, q_stacks[g])
            for r in range(rep):
                h = g * rep + r
                cs = slice(r * LANE, (r + 1) * LANE)
                bias = jnp.concatenate([bias_ref[h, band0], bias_ref[h, band1]], axis=0)
                lg = lg_all[:, cs] + bias + am
                lg_ref[slot, g, :, cs] = lg
                tmax_ref[slot, h:h + 1, :] = jnp.max(lg, axis=0, keepdims=True)

    def softmax_stage(j, carry):
        ms, ls = carry
        slot = j & 1
        new_ms, new_ls = [], []
        for g in range(A_KV_HEADS):
            ps, alphas = [], []
            for r in range(rep):
                h = g * rep + r
                m_new = jnp.maximum(ms[h], tmax_ref[slot, h:h + 1, :])
                m_safe = jnp.where(m_new == NEG_INF, 0.0, m_new)
                alpha = jnp.exp(ms[h] - m_safe)
                p = jnp.exp(lg_ref[slot, g, :, r * LANE:(r + 1) * LANE] - m_safe)
                new_ms.append(m_new)
                new_ls.append(alpha * ls[h] + jnp.sum(p, axis=0, keepdims=True))
                ps.append(p.astype(BF16))
                alphas.append(alpha)
            vt = vt_ref[j, g * A_HEAD_DIM:(g + 1) * A_HEAD_DIM, :]
            pv = _dot(vt, jnp.concatenate(ps, axis=1))
            acc_ref[g] = acc_ref[g] * jnp.concatenate(alphas, axis=1) + pv
        return tuple(new_ms), tuple(new_ls)

    def tile_body(j, carry):
        carry = softmax_stage(j, carry)
        logits_stage(j + 1)
        return carry

    init = (tuple(jnp.full((1, LANE), NEG_INF, F32) for _ in range(A_HEADS)),
            tuple(jnp.zeros((1, LANE), F32) for _ in range(A_HEADS)))
    logits_stage(0)
    carry = lax.fori_loop(0, n2 - 1, tile_body, init)
    _, ls = softmax_stage(n2 - 1, carry)
    for g in range(A_KV_HEADS):
        for r in range(rep):
            h = g * rep + r
            ot_ref[h * A_HEAD_DIM:(h + 1) * A_HEAD_DIM, :] = (
                acc_ref[g, :, r * LANE:(r + 1) * LANE] / ls[h])
    o_ref[...] = ot_ref[...].T.astype(o_ref.dtype)


def _dsa(h_a, h_f, bias_tiles, bsz, seq):
    n = bsz * seq
    nq = seq // LANE
    k_sel = min(TOPK_MAX, seq // 4)
    assert k_sel % LANE == 0 and seq % (2 * LANE) == 0
    qw = A_HEADS * LANE
    slab_blk = (h_f.shape[1] - LANE) // LANE
    return pl.pallas_call(
        functools.partial(_dsa_kernel, k_sel=k_sel, seq=seq),
        out_shape=jax.ShapeDtypeStruct((n, A_HEADS * A_HEAD_DIM), BF16),
        grid=(bsz, nq),
        in_specs=[
            pl.BlockSpec((LANE, qw), lambda b, i: (b * nq + i, 0)),
            pl.BlockSpec((seq, A_KV_HEADS * LANE), lambda b, i: (b, qw // (A_KV_HEADS * LANE))),
            pl.BlockSpec((seq, LANE), lambda b, i: (b, (qw + A_KV_HEADS * LANE) // LANE)),
            pl.BlockSpec((LANE, IDX_HEADS * LANE), lambda b, i: (b * nq + i, 0)),
            pl.BlockSpec((seq, LANE), lambda b, i: (b, slab_blk)),
            pl.BlockSpec((LANE, LANE), lambda b, i: (b * nq + i, slab_blk)),
            pl.BlockSpec((A_HEADS, 3, LANE, LANE), lambda b, i: (0, 0, 0, 0)),
        ],
        out_specs=pl.BlockSpec((LANE, A_HEADS * A_HEAD_DIM), lambda b, i: (b * nq + i, 0)),
        scratch_shapes=[
            pltpu.VMEM((seq + LANE, LANE), I32),
            pltpu.VMEM((seq, LANE), F32),
            pltpu.VMEM((nq // 2, LANE, 2 * LANE), BF16),
            pltpu.VMEM((A_KV_HEADS, A_HEAD_DIM, (A_HEADS // A_KV_HEADS) * LANE), F32),
            pltpu.VMEM((2, A_KV_HEADS, 2 * LANE, (A_HEADS // A_KV_HEADS) * LANE), F32),
            pltpu.VMEM((2, A_HEADS, LANE), F32),
            pltpu.VMEM((A_HEADS * A_HEAD_DIM, LANE), F32),
        ],
        compiler_params=_params(("arbitrary", "arbitrary")),
        name="dsa_attention",
    )(h_a, h_a, h_a, h_f, h_f, h_f, bias_tiles)


def _gla_kernel(bq_ref, bk_ref, bv_ref, br_ref, slab_ref, w2_ref, gb_ref, ng_ref, o_ref,
                st_ref, *, n_chunks):
    @pl.when(pl.program_id(1) == 0)
    def _():
        st_ref[...] = jnp.zeros(st_ref.shape, F32)

    ch = GLA_CHUNK
    rws = n_chunks * ch
    shift = ch.bit_length() - 1
    assert 1 << shift == ch
    row = lax.broadcasted_iota(I32, (rws, rws), 0)
    col = lax.broadcasted_iota(I32, (rws, rws), 1)
    tril = (lax.shift_right_logical(row, shift) == lax.shift_right_logical(col, shift)) & (row >= col)
    tri = jnp.where(tril, 1.0, 0.0).astype(BF16)
    gate = _dot(slab_ref[...].astype(BF16), w2_ref[...]) + gb_ref[...]
    log_a = (jnp.minimum(gate, 0.0) - jnp.log1p(jnp.exp(-jnp.abs(gate)))) / GATE_TAU
    hi, mid, lo = _split3(log_a)
    cum = _dot(tri, hi) + _dot(tri, mid) + _dot(tri, lo)
    lasts = [cum[(c + 1) * ch - 1:(c + 1) * ch, :] for c in range(n_chunks)]
    last = jnp.concatenate([jnp.broadcast_to(l, (ch, l.shape[1])) for l in lasts], axis=0)
    q = bq_ref[...] * (B_KEY_DIM ** -0.5)
    k = bk_ref[...]
    q_dec = (q * jnp.exp(cum)).astype(BF16)
    k_inv = (k * jnp.exp(-cum)).astype(BF16)
    k_end = (k * jnp.exp(last - cum)).astype(BF16)
    decays = [jnp.exp(l) for l in lasts]
    for h in range(B_HEADS):
        cs = slice(h * LANE, (h + 1) * LANE)
        v = bv_ref[:, cs].astype(BF16)
        sc = jnp.where(tril, _dot_nt(q_dec[:, cs], k_inv[:, cs]), 0.0)
        o_intra = _dot(sc.astype(BF16), v)
        st = st_ref[h]
        parts = []
        for c in range(n_chunks):
            rows = slice(c * ch, (c + 1) * ch)
            parts.append(o_intra[rows] + _dot_nt(q_dec[rows, cs], st.astype(BF16)))
            st = decays[c][:, cs] * st + _dot_tn(v[rows], k_end[rows, cs])
        st_ref[h] = st
        o = jnp.concatenate(parts, axis=0)
        mu = jnp.mean(o, axis=-1, keepdims=True)
        dlt = o - mu
        var = jnp.mean(dlt * dlt, axis=-1, keepdims=True)
        on = dlt * lax.rsqrt(var + LN_EPS) * ng_ref[:, cs]
        r = br_ref[:, cs]
        o_ref[:, cs] = (on * (r * _sigmoid(r))).astype(o_ref.dtype)


def _gla(h_f, w2p, gbp, norm_g, bsz, seq, rows_per_step=256):
    n = bsz * seq
    hw = B_HEADS * LANE
    steps = seq // rows_per_step
    slab_blk = (h_f.shape[1] - LANE) // LANE
    blk = lambda cb: pl.BlockSpec((rows_per_step, hw), lambda b, s, cb=cb: (b * steps + s, cb))
    return pl.pallas_call(
        functools.partial(_gla_kernel, n_chunks=rows_per_step // GLA_CHUNK),
        out_shape=jax.ShapeDtypeStruct((n, hw), BF16),
        grid=(bsz, steps),
        in_specs=[blk(1), blk(2), blk(3), blk(4),
                  pl.BlockSpec((rows_per_step, LANE), lambda b, s: (b * steps + s, slab_blk)),
                  pl.BlockSpec((LANE, hw), lambda b, s: (0, 0)),
                  pl.BlockSpec((1, hw), lambda b, s: (0, 0)),
                  pl.BlockSpec((1, hw), lambda b, s: (0, 0))],
        out_specs=pl.BlockSpec((rows_per_step, hw), lambda b, s: (b * steps + s, 0)),
        scratch_shapes=[pltpu.VMEM((B_HEADS, LANE, LANE), F32)],
        compiler_params=_params(("arbitrary", "arbitrary")),
        name="gla_attention",
    )(h_f, h_f, h_f, h_f, h_f, w2p, gbp, norm_g)


def _outproj_ln_kernel(oa_ref, ob_ref, x_ref, w_ref, g_ref, b_ref, out_ref):
    ka = oa_ref.shape[1]
    m = _dot(oa_ref[...], w_ref[0:ka, :]) + _dot(ob_ref[...], w_ref[ka:, :])
    out_ref[...] = _layer_norm(DN_ALPHA * x_ref[...] + m, g_ref[...], b_ref[...])


def _outproj_ln(o_a, o_b, x2, w, g, b, tm=512):
    n, d = x2.shape
    row = lambda width: pl.BlockSpec((tm, width), lambda i: (i, 0))
    full = lambda a: pl.BlockSpec(a.shape, lambda i: (0, 0))
    return pl.pallas_call(
        _outproj_ln_kernel,
        out_shape=jax.ShapeDtypeStruct((n, d), F32),
        grid=(n // tm,),
        in_specs=[row(o_a.shape[1]), row(o_b.shape[1]), row(d), full(w), full(g), full(b)],
        out_specs=row(d),
        compiler_params=_params(("arbitrary",)),
        name="out_proj_ln",
    )(o_a, o_b, x2, w, g, b)


def _route(lt):
    gl = [lt[g:g + 1, :] for g in range(MOE_GROUPS)]
    best, gsel = gl[0], jnp.zeros_like(gl[0], dtype=I32)
    for g in range(1, MOE_GROUPS):
        better = gl[g] > best
        gsel = jnp.where(better, g, gsel)
        best = jnp.where(better, gl[g], best)
    denom = sum(jnp.exp(x - best) for x in gl)
    g_w = 1.0 / denom
    fl = []
    for e in range(MOE_EXPERTS):
        acc = jnp.zeros_like(best)
        for g in range(MOE_GROUPS):
            r = MOE_GROUPS + g * MOE_EXPERTS + e
            acc = jnp.where(gsel == g, lt[r:r + 1, :], acc)
        fl.append(acc)
    v1, i1 = fl[0], jnp.zeros_like(gsel)
    for e in range(1, MOE_EXPERTS):
        better = fl[e] > v1
        i1 = jnp.where(better, e, i1)
        v1 = jnp.where(better, fl[e], v1)
    v2, i2 = jnp.full_like(v1, NEG_INF), jnp.zeros_like(gsel)
    for e in range(MOE_EXPERTS):
        better = (fl[e] > v2) & (i1 != e)
        i2 = jnp.where(better, e, i2)
        v2 = jnp.where(better, fl[e], v2)
    e2 = jnp.exp(v2 - v1)
    w1 = (1.0 / (1.0 + e2)) * g_w
    w2 = (e2 / (1.0 + e2)) * g_w
    e_gate = [jnp.where(i1 == e, w1, 0.0) + jnp.where(i2 == e, w2, 0.0)
              for e in range(MOE_EXPERTS)]
    return gsel, e_gate


def _moe_route_kernel(x_ref, wr_ref, rb_ref, xa_ref, rt_ref, cnt_ref, tri_ref, carry_ref):
    i = pl.program_id(0)
    tm, d = x_ref.shape

    @pl.when(i == 0)
    def _():
        carry_ref[...] = jnp.zeros(carry_ref.shape, F32)
        before = (lax.broadcasted_iota(I32, (tm, tm), 0) < lax.broadcasted_iota(I32, (tm, tm), 1))
        tri_ref[...] = jnp.where(before, 1.0, 0.0).astype(BF16)

    x = x_ref[...]
    xs = _split3(x)
    ws = _split3(wr_ref[...])
    lt = jnp.zeros((wr_ref.shape[0], tm), F32)
    for a, b in ((2, 0), (0, 2), (1, 1), (1, 0), (0, 1), (0, 0)):
        lt = lt + _dot_nt(ws[a], xs[b])
    lt = lt + rb_ref[...]
    gsel, e_gate = _route(lt)

    row8 = lax.broadcasted_iota(I32, (8, tm), 0)
    onehot = jnp.where(row8 == gsel, 1.0, 0.0)
    earlier = _dot(onehot.astype(BF16), tri_ref[...])
    rank = jnp.sum(onehot * (earlier + carry_ref[:, 0:1]), axis=0, keepdims=True)
    rt_ref[...] = jnp.where(row8 == 0, gsel, jnp.where(row8 == 1, rank.astype(I32), 0))
    carry_ref[...] = carry_ref[...] + jnp.sum(onehot, axis=1, keepdims=True)
    cnt_ref[...] = carry_ref[...].astype(I32)

    erow = lax.broadcasted_iota(I32, (LANE, tm), 0)
    gt = jnp.zeros((LANE, tm), F32)
    for e in range(MOE_EXPERTS):
        gt = jnp.where(erow == e, e_gate[e], gt)
    xa_ref[:, :d] = x
    xa_ref[:, d:] = gt.T


def _moe_route(h, wr, rb, tm=1024):
    n, d = h.shape
    tm = min(tm, n)
    full = lambda a: pl.BlockSpec(a.shape, lambda i: (0,) * a.ndim)
    return pl.pallas_call(
        _moe_route_kernel,
        out_shape=[jax.ShapeDtypeStruct((n, d + LANE), F32), jax.ShapeDtypeStruct((8, n), I32),
                   jax.ShapeDtypeStruct((8, LANE), I32)],
        grid=(n // tm,),
        in_specs=[pl.BlockSpec((tm, d), lambda i: (i, 0)), full(wr), full(rb)],
        out_specs=[pl.BlockSpec((tm, d + LANE), lambda i: (i, 0)),
                   pl.BlockSpec((8, tm), lambda i: (0, i)),
                   pl.BlockSpec((8, LANE), lambda i: (0, 0))],
        scratch_shapes=[pltpu.VMEM((tm, tm), BF16), pltpu.VMEM((8, LANE), F32)],
        compiler_params=_params(("arbitrary",)),
        name="moe_route",
    )(h, wr, rb)


def _row_copy_waves(n_rows, wave, start_row, wait_wave):
    n_waves = n_rows // wave
    for w in range(n_waves + 1):
        if w < n_waves:
            lax.fori_loop(w * wave, (w + 1) * wave, lambda r, c, w=w: (start_row(r, w % 2), c)[1], 0,
                          unroll=8)
        if w >= 1:
            wait_wave((w - 1) % 2)


def _moe_dispatch_kernel(pos_ref, fill_ref, xa_ref, xs_hbm, zero_ref, sem, *, wave):
    i = pl.program_id(0)
    tm = pos_ref.shape[1]
    tme = zero_ref.shape[0]

    def copy(r, s):
        return pltpu.make_async_copy(xa_ref.at[r], xs_hbm.at[pos_ref[0, r]], sem.at[s])

    def wait_wave(s):
        rows = pl.ds(0, wave)
        pltpu.make_async_copy(xa_ref.at[rows], xs_hbm.at[rows], sem.at[s]).wait()

    _row_copy_waves(tm, wave, lambda r, s: copy(r, s).start(), wait_wave)

    @pl.when(i == pl.num_programs(0) - 1)
    def _():
        zero_ref[...] = jnp.zeros(zero_ref.shape, F32)
        for g in range(MOE_GROUPS):
            first, count = fill_ref[0, g], fill_ref[1, g]

            def zcopy(r, first=first):
                return pltpu.make_async_copy(zero_ref.at[0], xs_hbm.at[first + r], sem.at[0])

            lax.fori_loop(0, count, lambda r, c: (zcopy(r).start(), c)[1], 0)
            lax.fori_loop(0, count, lambda r, c: (zcopy(r).wait(), c)[1], 0)
        n_tiles = fill_ref[2, 0]
        for k in range(MOE_GROUPS):
            @pl.when(k < fill_ref[2, 1])
            def _(k=k):
                tail = xs_hbm.at[pl.ds(pl.multiple_of((n_tiles + k) * tme, tme), tme)]
                cp = pltpu.make_async_copy(zero_ref, tail, sem.at[0])
                cp.start()
                cp.wait()


def _moe_dispatch(xa, pos, fill, n_sorted, tme, tm=1024, wave=256):
    n, da = xa.shape
    tm = min(tm, n)
    wave = min(wave, tm)
    return pl.pallas_call(
        functools.partial(_moe_dispatch_kernel, wave=wave),
        out_shape=jax.ShapeDtypeStruct((n_sorted, da), F32),
        grid=(n // tm,),
        in_specs=[pl.BlockSpec((1, tm), lambda i: (0, i), memory_space=pltpu.SMEM),
                  pl.BlockSpec(memory_space=pltpu.SMEM),
                  pl.BlockSpec((tm, da), lambda i: (i, 0))],
        out_specs=pl.BlockSpec(memory_space=pl.ANY),
        scratch_shapes=[pltpu.VMEM((tme, da), F32), pltpu.SemaphoreType.DMA((2,))],
        compiler_params=_params(("arbitrary",)),
        name="moe_dispatch",
    )(pos, fill, xa)


def _moe_expert_kernel(tg_ref, tj_ref, nt_ref, xs_ref, wg_ref, wu_ref, wd_ref, lg_ref, lb_ref,
                       ys_ref, wgb_ref, wub_ref, wdb_ref):
    d = ys_ref.shape[1]
    i = pl.program_id(0)

    @pl.when((i == 0) | (tg_ref[i] != tg_ref[jnp.maximum(i - 1, 0)]))
    def _():
        for e in range(MOE_EXPERTS):
            wgb_ref[e] = wg_ref[e].astype(BF16)
            wub_ref[e] = wu_ref[e].astype(BF16)
            wdb_ref[e] = wd_ref[e].astype(BF16)

    @pl.when(i < nt_ref[0])
    def _():
        x = xs_ref[:, :d]
        gates = xs_ref[:, d:]
        xb = x.astype(BF16)
        y = jnp.zeros(x.shape, F32)
        for e in range(MOE_EXPERTS):
            hg = _dot(xb, wgb_ref[e])
            hu = _dot(xb, wub_ref[e])
            hid = hg * _sigmoid(hg) * hu * gates[:, e:e + 1]
            y = y + _dot(hid.astype(BF16), wdb_ref[e])
        ys_ref[...] = _layer_norm(DN_ALPHA * x + y, lg_ref[...], lb_ref[...])

    @pl.when(pl.program_id(0) >= nt_ref[0])
    def _():
        ys_ref[...] = jnp.zeros(ys_ref.shape, F32)


def _moe_experts(xs, tile_g, tile_j, n_tiles, layer, wg, wu, wd, lg, lb, tme):
    p, da = xs.shape
    d = da - LANE
    grp = lambda w: pl.BlockSpec((None, None) + w.shape[2:],
                                 lambda i, tg, tj, nt: (layer, tg[i], 0, 0, 0),
                                 pipeline_mode=pl.Buffered(1))
    full = lambda a: pl.BlockSpec(a.shape, lambda i, tg, tj, nt: (0,) * a.ndim)
    grid_spec = pltpu.PrefetchScalarGridSpec(
        num_scalar_prefetch=3,
        grid=(p // tme,),
        in_specs=[pl.BlockSpec((tme, da), lambda i, tg, tj, nt: (tj[i], 0)),
                  grp(wg), grp(wu), grp(wd), full(lg), full(lb)],
        out_specs=pl.BlockSpec((tme, d), lambda i, tg, tj, nt: (i, 0)),
        scratch_shapes=[pltpu.VMEM(w.shape[2:], BF16) for w in (wg, wu, wd)],
    )
    return pl.pallas_call(
        _moe_expert_kernel,
        out_shape=jax.ShapeDtypeStruct((p, d), F32),
        grid_spec=grid_spec,
        compiler_params=pltpu.CompilerParams(dimension_semantics=("arbitrary",),
                                             vmem_limit_bytes=MOE_VMEM_LIMIT),
        name="moe_experts",
    )(tile_g, tile_j, n_tiles, xs, wg, wu, wd, lg, lb)


def _moe_combine_kernel(pos_ref, ys_hbm, o_ref, sem, *, wave):
    tm = o_ref.shape[0]

    def copy(r, s):
        return pltpu.make_async_copy(ys_hbm.at[pos_ref[0, r]], o_ref.at[r], sem.at[s])

    def wait_wave(s):
        rows = pl.ds(0, wave)
        pltpu.make_async_copy(ys_hbm.at[rows], o_ref.at[rows], sem.at[s]).wait()

    _row_copy_waves(tm, wave, lambda r, s: copy(r, s).start(), wait_wave)


def _moe_combine(ys, pos, n, tm=1024, wave=256):
    d = ys.shape[1]
    tm = min(tm, n)
    wave = min(wave, tm)
    return pl.pallas_call(
        functools.partial(_moe_combine_kernel, wave=wave),
        out_shape=jax.ShapeDtypeStruct((n, d), F32),
        grid=(n // tm,),
        in_specs=[pl.BlockSpec((1, tm), lambda i: (0, i), memory_space=pltpu.SMEM),
                  pl.BlockSpec(memory_space=pl.ANY)],
        out_specs=pl.BlockSpec((tm, d), lambda i: (i, 0)),
        scratch_shapes=[pltpu.SemaphoreType.DMA((2,))],
        compiler_params=_params(("arbitrary",)),
        name="moe_combine",
    )(pos, ys)


def _moe(h, wr, rb, layer, wg, wu, wd, lg, lb, regroup=None, tme=512):
    n, d = h.shape
    tme = min(tme, n)
    xa, rt, cnt = _moe_route(h, wr, rb)
    counts = cnt[:MOE_GROUPS, 0]
    tiles = (counts + tme - 1) // tme
    first_tile = jnp.cumsum(tiles) - tiles
    gid, rank = rt[0], rt[1]
    pos = (first_tile[gid] * tme + rank)[None, :]
    n_slots = n // tme + MOE_GROUPS
    n_tiles = jnp.sum(tiles)
    fill = jnp.stack([first_tile * tme + counts, tiles * tme - counts,
                      jnp.zeros((MOE_GROUPS,), I32).at[0].set(n_tiles).at[1].set(n_slots - n_tiles)])
    slot = jnp.minimum(jnp.arange(n_slots, dtype=I32), n_tiles - 1)
    tile_g = jnp.sum(slot[:, None] >= (first_tile + tiles)[None, :], axis=1).astype(I32)
    xs = _moe_dispatch(xa, pos, fill, n_slots * tme, tme)
    ys = _moe_experts(xs, tile_g, slot, n_tiles[None].astype(I32), layer, wg, wu, wd, lg, lb, tme)
    if regroup is not None:
        pos = pos.reshape(regroup).T.reshape(1, n)
    return _moe_combine(ys, pos, n)


def _s5_prep_kernel(lre_ref, lim_ref, ldt_ref, bre_ref, bim_ref, are_ref, aim_ref, bbre_ref,
                    bbim_ref):
    lr = jnp.minimum(lre_ref[...], -1e-4)
    li = lim_ref[...]
    dt = jnp.exp(ldt_ref[...])
    mag = jnp.exp(lr * dt)
    ab_re = mag * jnp.cos(li * dt)
    ab_im = mag * jnp.sin(li * dt)
    den = lr * lr + li * li
    nr = ab_re - 1.0
    coef_re = (nr * lr + ab_im * li) / den
    coef_im = (ab_im * lr - nr * li) / den
    are_ref[...] = ab_re
    aim_ref[...] = ab_im
    bbre_ref[...] = coef_re * bre_ref[...] - coef_im * bim_ref[...]
    bbim_ref[...] = coef_re * bim_ref[...] + coef_im * bre_ref[...]


def _s5_prep(lam_re, lam_im, log_dt, b_re, b_im):
    gp = C_GROUPS * C_STATE
    col = lambda a: a.reshape(gp, 1)
    ldt = jnp.broadcast_to(log_dt[:, None], (C_GROUPS, C_STATE))
    outs = pl.pallas_call(
        _s5_prep_kernel,
        out_shape=[jax.ShapeDtypeStruct((gp, 1), F32), jax.ShapeDtypeStruct((gp, 1), F32),
                   jax.ShapeDtypeStruct((gp, C_GROUP), F32), jax.ShapeDtypeStruct((gp, C_GROUP), F32)],
        name="s5_discretise",
    )(col(lam_re), col(lam_im), col(ldt), b_re.reshape(gp, C_GROUP), b_im.reshape(gp, C_GROUP))
    ab_re, ab_im, bb_re, bb_im = outs
    return (ab_re.reshape(C_GROUPS, C_STATE), ab_im.reshape(C_GROUPS, C_STATE),
            bb_re.reshape(C_GROUPS, C_STATE, C_GROUP), bb_im.reshape(C_GROUPS, C_STATE, C_GROUP))


def _s5_uproj_kernel(x_ref, w_ref, o_ref):
    o_ref[...] = _dot(x_ref[...].astype(BF16), w_ref[...])


def _s5_uproj(h2, w, tm=512):
    n, d = h2.shape
    return pl.pallas_call(
        _s5_uproj_kernel,
        out_shape=jax.ShapeDtypeStruct((n, d), F32),
        grid=(n // tm,),
        in_specs=[pl.BlockSpec((tm, d), lambda i: (i, 0)), pl.BlockSpec(w.shape, lambda i: (0, 0))],
        out_specs=pl.BlockSpec((tm, d), lambda i: (i, 0)),
        compiler_params=_params(("arbitrary",)),
        name="s5_in_proj",
    )(h2, w)


def _gelu_tanh(x):
    return 0.5 * x * (1.0 + jnp.tanh(math.sqrt(2.0 / math.pi) * (x + 0.044715 * (x * x * x))))


def _s5_scan_kernel(u_ref, bbre_ref, bbim_ref, are_ref, aim_ref, cm_ref, d_ref, g_ref,
                    st_ref, bure_ref, buim_ref, s_ref, *, bsz, tl, n_cb, sub):
    @pl.when(pl.program_id(0) == 0)
    def _():
        st_ref[...] = jnp.zeros(st_ref.shape, F32)

    cw = D_MODEL // n_cb
    sw = bure_ref.shape[2]

    def input_states(cb):
        ub = u_ref[:, cb * cw:(cb + 1) * cw].astype(BF16)
        bure_ref[cb % 2] = _dot(ub, bbre_ref[cb])
        buim_ref[cb % 2] = _dot(ub, bbim_ref[cb])

    input_states(0)
    for cb in range(n_cb):
        buf = cb % 2
        if cb + 1 < n_cb:
            input_states(cb + 1)
        for s0 in range(0, sw, sub):
            cs = slice(s0, s0 + sub)
            a_re = jnp.broadcast_to(are_ref[cb, :, cs], (bsz, sub))
            a_im = jnp.broadcast_to(aim_ref[cb, :, cs], (bsz, sub))
            s_re, s_im = st_ref[cb, 0, :, cs], st_ref[cb, 1, :, cs]
            for t in range(tl):
                rows = slice(t * bsz, (t + 1) * bsz)
                s_re, s_im = (a_re * s_re - a_im * s_im + bure_ref[buf, rows, cs],
                              a_re * s_im + a_im * s_re + buim_ref[buf, rows, cs])
                s_ref[buf, rows, cs] = s_re.astype(BF16)
                s_ref[buf, rows, sw + s0:sw + s0 + sub] = s_im.astype(BF16)
            st_ref[cb, 0, :, cs] = s_re
            st_ref[cb, 1, :, cs] = s_im
        ch = slice(cb * cw, (cb + 1) * cw)
        y = _dot(s_ref[buf], cm_ref[cb]) + d_ref[:, ch] * u_ref[:, ch]
        g_ref[:, ch] = _gelu_tanh(y).astype(g_ref.dtype)


def _s5_scan(u_lb, bbre, bbim, a_re, a_im, cmat, d_skip, bsz, seq, tl=32, sub=512):
    n, d = u_lb.shape
    n_cb, cw, sw = bbre.shape
    rows = tl * bsz
    full = lambda a: pl.BlockSpec(a.shape, lambda t: (0,) * a.ndim)
    return pl.pallas_call(
        functools.partial(_s5_scan_kernel, bsz=bsz, tl=tl, n_cb=n_cb, sub=sub),
        out_shape=jax.ShapeDtypeStruct((n, d), BF16),
        grid=(seq // tl,),
        in_specs=[pl.BlockSpec((rows, d), lambda t: (t, 0)), full(bbre), full(bbim), full(a_re),
                  full(a_im), full(cmat), full(d_skip)],
        out_specs=pl.BlockSpec((rows, d), lambda t: (t, 0)),
        scratch_shapes=[pltpu.VMEM((n_cb, 2, bsz, sw), F32), pltpu.VMEM((2, rows, sw), F32),
                        pltpu.VMEM((2, rows, sw), F32), pltpu.VMEM((2, rows, 2 * sw), BF16)],
        compiler_params=_params(("arbitrary",)),
        name="s5_scan",
    )(u_lb, bbre, bbim, a_re, a_im, cmat, d_skip)


def _s5_glu_kernel(g_ref, x_ref, w1_ref, w2_ref, wo_ref, lg_ref, lb_ref, out_ref):
    g = g_ref[...]
    z = _dot(g, w1_ref[...]) * _sigmoid(_dot(g, w2_ref[...]))
    m = _dot(z.astype(BF16), wo_ref[...])
    out_ref[...] = _layer_norm(DN_ALPHA * x_ref[...] + m, lg_ref[...], lb_ref[...])


def _s5_glu(g, h2, w1, w2, wo, lg, lb, tm=512):
    n, d = h2.shape
    row = pl.BlockSpec((tm, d), lambda i: (i, 0))
    full = lambda a: pl.BlockSpec(a.shape, lambda i: (0, 0))
    return pl.pallas_call(
        _s5_glu_kernel,
        out_shape=jax.ShapeDtypeStruct((n, d), F32),
        grid=(n // tm,),
        in_specs=[row, row, full(w1), full(w2), full(wo), full(lg), full(lb)],
        out_specs=row,
        compiler_params=_params(("arbitrary",)),
        name="s5_glu_out_ln",
    )(g, h2, w1, w2, wo, lg, lb)


def _pad_heads(w, heads, dim):
    d = w.shape[0]
    w = w.reshape(d, heads, dim)
    return jnp.pad(w, ((0, 0), (0, 0), (0, LANE - dim))).reshape(d, heads * LANE)


def _even_layer_weights(w_in, gate_w2, gate_b):
    splits = (A_HEADS * A_HEAD_DIM, A_KV_HEADS * A_HEAD_DIM, A_KV_HEADS * A_HEAD_DIM,
              IDX_HEADS * IDX_DIM, IDX_DIM, IDX_HEADS,
              B_HEADS * B_KEY_DIM, B_HEADS * B_KEY_DIM, B_HEADS * B_VAL_DIM, GATE_RANK,
              B_HEADS * B_VAL_DIM)
    offs = np.cumsum((0,) + splits)
    aq, ak, av, iq, ik, iw, bq, bk, bv, bg, br = [w_in[:, offs[k]:offs[k + 1]] for k in range(11)]
    d = w_in.shape[0]
    slab = jnp.concatenate(
        [ik, iw, bg, jnp.zeros((d, LANE - IDX_DIM - IDX_HEADS - GATE_RANK), w_in.dtype)], axis=1)
    w_a = jnp.concatenate([_pad_heads(aq, A_HEADS, A_HEAD_DIM),
                           _pad_heads(ak, A_KV_HEADS, A_HEAD_DIM), av], axis=1)
    w_f = jnp.concatenate([_pad_heads(iq, IDX_HEADS, IDX_DIM), _pad_heads(bq, B_HEADS, B_KEY_DIM),
                           _pad_heads(bk, B_HEADS, B_KEY_DIM), bv, br, slab], axis=1)
    w_all = jnp.concatenate([w_a, w_f], axis=1).astype(BF16)
    g0 = IDX_DIM + IDX_HEADS
    w2p = jnp.zeros((LANE, B_HEADS * LANE), F32).at[g0:g0 + GATE_RANK].set(
        _pad_heads(gate_w2, B_HEADS, B_KEY_DIM)).astype(BF16)
    gbp = _pad_heads(gate_b[None, :], B_HEADS, B_KEY_DIM)
    return w_all, (w_a.shape[1], w_f.shape[1]), w2p, gbp


def _even_mixer_ln(x2, bsz, seq, w_in, bias_tiles, gate_w2, gate_b, norm_g, w_out, ln_g, ln_b):
    w_all, widths, w2p, gbp = _even_layer_weights(w_in, gate_w2, gate_b)
    h_a, h_f = _in_proj(x2, w_all, widths, (BF16, F32))
    o_a = _dsa(h_a, h_f, bias_tiles, bsz, seq)
    o_b = _gla(h_f, w2p, gbp, norm_g[None, :], bsz, seq)
    return _outproj_ln(o_a, o_b, x2, w_out.astype(BF16), ln_g[None, :], ln_b[None, :])


def _block_diag(m, per):
    g, r, c = m.shape
    eye = jnp.eye(per, dtype=m.dtype)
    m = m.reshape(g // per, per, r, c)
    bd = m[:, :, :, None, :] * eye[None, :, None, :, None]
    return bd.reshape(g // per, per * r, per * c)


def _s5_mixer_ln(h2, bsz, seq, w_in, lam_re, lam_im, log_dt, b_re, b_im, c_re, c_im, d_skip,
                 glu_w1, glu_w2, w_out, ln_g, ln_b, groups_per_block=16):
    ab_re, ab_im, bb_re, bb_im = _s5_prep(lam_re, lam_im, log_dt, b_re, b_im)
    per = groups_per_block
    n_cb = C_GROUPS // per
    bbre = _block_diag(jnp.swapaxes(bb_re, 1, 2), per).astype(BF16)
    bbim = _block_diag(jnp.swapaxes(bb_im, 1, 2), per).astype(BF16)
    a_re = ab_re.reshape(n_cb, 1, per * C_STATE)
    a_im = ab_im.reshape(n_cb, 1, per * C_STATE)
    cre = _block_diag(jnp.swapaxes(c_re, 1, 2), per)
    cim = _block_diag(jnp.swapaxes(c_im, 1, 2), per)
    cmat = jnp.concatenate([cre, -cim], axis=1).astype(BF16)
    u = _s5_uproj(h2, w_in.astype(BF16))
    g = _s5_scan(u, bbre, bbim, a_re, a_im, cmat, d_skip[None, :], bsz, seq)
    return _s5_glu(g, h2, glu_w1.astype(BF16), glu_w2.astype(BF16), w_out.astype(BF16),
                   ln_g[None, :], ln_b[None, :])


def _moe_ln(h, layer, r_coarse, rb_coarse, r_fine, rb_fine, w_gate, w_up, w_down, ln_g, ln_b,
            regroup=None):
    d = h.shape[1]
    rows = MOE_GROUPS + N_EXPERTS
    wr = jnp.concatenate([r_coarse.T, jnp.transpose(r_fine, (0, 2, 1)).reshape(N_EXPERTS, d),
                          jnp.zeros((32 - rows, d), F32)], axis=0)
    rb = jnp.concatenate([rb_coarse, rb_fine.reshape(N_EXPERTS), jnp.zeros((32 - rows,), F32)])
    return _moe(h, wr, rb[:, None], layer, w_gate, w_up, w_down, ln_g[None, :], ln_b[None, :],
                regroup)


def kernel(x, rel_bias, ab_w_in, gla_gate_w2, gla_gate_b, gla_norm_g, ab_w_out, s5_w_in, s5_lam_re, s5_lam_im, s5_log_dt, s5_b_re, s5_b_im, s5_c_re, s5_c_im, s5_d, s5_glu_w1, s5_glu_w2, s5_w_out, ln_mix_g, ln_mix_b, ln_ffn_g, ln_ffn_b, moe_r_coarse, moe_rb_coarse, moe_r_fine, moe_rb_fine, moe_w_gate, moe_w_up, moe_w_down):
    bsz, seq, d = x.shape
    h = x.reshape(bsz * seq, d)
    bias_tiles = _bias_tiles(rel_bias)
    time_major = False
    for layer in range(DEPTH):
        i = layer // 2
        assert time_major == (layer % 2 == 1)
        if layer % 2 == 0:
            h = _even_mixer_ln(h, bsz, seq, ab_w_in[i], bias_tiles, gla_gate_w2[i], gla_gate_b[i],
                               gla_norm_g[i], ab_w_out[i], ln_mix_g[layer], ln_mix_b[layer])
        else:
            h = _s5_mixer_ln(h, bsz, seq, s5_w_in[i], s5_lam_re[i], s5_lam_im[i], s5_log_dt[i],
                             s5_b_re[i], s5_b_im[i], s5_c_re[i], s5_c_im[i], s5_d[i],
                             s5_glu_w1[i], s5_glu_w2[i], s5_w_out[i],
                             ln_mix_g[layer], ln_mix_b[layer])
        want_time_major = layer + 1 < DEPTH and (layer + 1) % 2 == 1
        regroup = None
        if want_time_major != time_major:
            regroup = (seq, bsz) if time_major else (bsz, seq)
        h = _moe_ln(h, layer, moe_r_coarse[layer], moe_rb_coarse[layer], moe_r_fine[layer],
                    moe_rb_fine[layer], moe_w_gate, moe_w_up, moe_w_down,
                    ln_ffn_g[layer], ln_ffn_b[layer], regroup)
        time_major = want_time_major
    return h.reshape(bsz, seq, d).astype(x.dtype)
```

```python
import functools
import math

import numpy as np
import jax
import jax.numpy as jnp
from jax import lax
from jax.experimental import pallas as pl
from jax.experimental.pallas import tpu as pltpu

F32 = jnp.float32
BF16 = jnp.bfloat16
I32 = jnp.int32

D_MODEL = 1024
DEPTH = 2
DN_ALPHA = (2.0 * DEPTH) ** 0.25
LN_EPS = 1e-5
A_HEAD_DIM = 64
A_HEADS = 8
A_KV_HEADS = 2
IDX_HEADS = 4
IDX_DIM = 64
TOPK_MAX = 256
REL_BUCKETS = 32
REL_MAX_DIST = 128
B_HEADS = 4
B_VAL_DIM = 128
B_KEY_DIM = 64
GATE_RANK = 16
GATE_TAU = 16.0
GLA_CHUNK = 64
C_GROUP = 16
C_GROUPS = 64
C_STATE = 64
MOE_GROUPS = 4
MOE_EXPERTS = 4
MOE_HIDDEN = 512
N_EXPERTS = MOE_GROUPS * MOE_EXPERTS

LANE = 128
VMEM_LIMIT = 52 * 1024 * 1024
MOE_VMEM_LIMIT = 58 * 1024 * 1024
NEG_INF = float("-inf")
INT_MIN = -(2 ** 31)

_NT = (((1,), (1,)), ((), ()))
_TN = (((0,), (0,)), ((), ()))


def _dot(a, b):
    return jnp.dot(a, b, preferred_element_type=F32)


def _dot_nt(a, b):
    return lax.dot_general(a, b, _NT, preferred_element_type=F32)


def _dot_tn(a, b):
    return lax.dot_general(a, b, _TN, preferred_element_type=F32)


def _split3(x):
    hi = x.astype(BF16)
    r1 = x - hi.astype(F32)
    mid = r1.astype(BF16)
    lo = (r1 - mid.astype(F32)).astype(BF16)
    return hi, mid, lo


def _params(sem):
    return pltpu.CompilerParams(dimension_semantics=sem, vmem_limit_bytes=VMEM_LIMIT)


def _layer_norm(y, g, b):
    mu = jnp.mean(y, axis=-1, keepdims=True)
    d = y - mu
    var = jnp.mean(d * d, axis=-1, keepdims=True)
    return d * lax.rsqrt(var + LN_EPS) * g + b


def _sigmoid(x):
    return 1.0 / (1.0 + jnp.exp(-x))


def _bucket_thresholds():
    max_exact = REL_BUCKETS // 2
    nf = np.arange(max_exact, 4 * REL_MAX_DIST).astype(np.float32)
    large = max_exact + (np.log(nf / np.float32(max_exact))
                         / np.float32(math.log(REL_MAX_DIST / max_exact))
                         * np.float32(REL_BUCKETS - max_exact)).astype(np.int32)
    large = np.minimum(large, REL_BUCKETS - 1)
    return [int(nf[np.argmax(large >= max_exact + j)]) for j in range(1, REL_BUCKETS - max_exact)]


def _bias_kernel(rb_ref, out_ref):
    max_exact = REL_BUCKETS // 2
    srow = lax.broadcasted_iota(I32, (LANE, LANE), 0)
    tcol = lax.broadcasted_iota(I32, (LANE, LANE), 1)
    thr = _bucket_thresholds()
    for band in range(2):
        n = jnp.maximum(band * LANE + tcol - srow, 0)
        large = jnp.full((LANE, LANE), max_exact, I32)
        for t in thr:
            large = large + jnp.where(n >= t, 1, 0)
        bucket = jnp.where(n < max_exact, n, large)
        for h in range(A_HEADS):
            acc = jnp.zeros((LANE, LANE), F32)
            for bk in range(REL_BUCKETS):
                acc = acc + jnp.where(bucket == bk, rb_ref[bk, h], 0.0)
            out_ref[h, band] = acc
    for h in range(A_HEADS):
        out_ref[h, 2] = jnp.full((LANE, LANE), rb_ref[REL_BUCKETS - 1, h], F32)


def _bias_tiles(rel_bias):
    return pl.pallas_call(
        _bias_kernel,
        out_shape=jax.ShapeDtypeStruct((A_HEADS, 3, LANE, LANE), F32),
        in_specs=[pl.BlockSpec(memory_space=pltpu.SMEM)],
        name="rel_bias_tiles",
    )(rel_bias)


def _proj_kernel(x_ref, w_ref, *out_refs, widths, chunk):
    xb = x_ref[...].astype(BF16)
    c0 = 0
    for o_ref, width in zip(out_refs, widths):
        for c in range(0, width, chunk):
            ce = min(c + chunk, width)
            o_ref[:, c:ce] = _dot(xb, w_ref[:, c0 + c:c0 + ce]).astype(o_ref.dtype)
        c0 += width


def _in_proj(x2, w, widths, dtypes, tm=512, chunk=256):
    n, d = x2.shape
    return pl.pallas_call(
        functools.partial(_proj_kernel, widths=widths, chunk=chunk),
        out_shape=[jax.ShapeDtypeStruct((n, wd), dt) for wd, dt in zip(widths, dtypes)],
        grid=(n // tm,),
        in_specs=[pl.BlockSpec((tm, d), lambda i: (i, 0)),
                  pl.BlockSpec(w.shape, lambda i: (0, 0))],
        out_specs=[pl.BlockSpec((tm, wd), lambda i: (i, 0)) for wd in widths],
        compiler_params=_params(("arbitrary",)),
        name="in_proj",
    )(x2, w)


def _sortable(x):
    bits = lax.bitcast_convert_type(x, I32)
    return jnp.where(bits < 0, bits ^ jnp.int32(0x7FFFFFFF), bits)


def _dsa_kernel(q_ref, k_ref, v_ref, iq_ref, slab_ref, slabq_ref, bias_ref, o_ref,
                key_ref, am_ref, vt_ref, acc_ref, lg_ref, tmax_ref, ot_ref, *, k_sel, seq):
    i = pl.program_id(1)
    t0 = i * LANE
    nkt = i + 1
    kt2 = 2 * LANE
    n2 = lax.shift_right_logical(nkt + 1, 1)
    rep = A_HEADS // A_KV_HEADS
    srow_2 = lax.broadcasted_iota(I32, (kt2, LANE), 0)
    tcol = t0 + lax.broadcasted_iota(I32, (kt2, LANE), 1)

    @pl.when(i == 0)
    def _():
        for jt in range(seq // kt2):
            vt_ref[jt] = v_ref[jt * kt2:(jt + 1) * kt2, :].astype(F32).T.astype(BF16)

    slab_t = slabq_ref[...].T
    w_t = slab_t[IDX_DIM:IDX_DIM + IDX_HEADS, :] * (IDX_HEADS ** -0.5) * (IDX_DIM ** -0.5)
    iqb = iq_ref[...].astype(BF16)
    iq_stack = jnp.concatenate([iqb[:, h * LANE:(h + 1) * LANE] for h in range(IDX_HEADS)], axis=0)

    def score_tile(j, carry):
        s0 = pl.multiple_of(j * kt2, kt2)
        ikt = slab_ref[pl.ds(s0, kt2), :].astype(BF16)
        d = _dot_nt(ikt, iq_stack)
        acc = jnp.zeros((kt2, LANE), F32)
        for h in range(IDX_HEADS):
            acc = acc + jnp.maximum(d[:, h * LANE:(h + 1) * LANE], 0.0) * w_t[h:h + 1, :]
        acc = jnp.where(s0 + srow_2 <= tcol, acc, NEG_INF)
        key_ref[pl.ds(s0, kt2), :] = _sortable(acc)
        return carry

    def score_tiles(n_tiles):
        def run():
            for j in range(n_tiles):
                score_tile(j, 0)
        return run

    lax.switch(n2 - 1, [score_tiles(m) for m in range(1, seq // kt2 + 1)])
    key_ref[pl.ds(pl.multiple_of(nkt * LANE, LANE), LANE), :] = jnp.full((LANE, LANE), INT_MIN, I32)

    def count(pred_fn):
        def variant(n_chunks):
            def run():
                c = jnp.zeros((8, LANE), I32)
                for j in range(n_chunks):
                    kt = key_ref[j * kt2:(j + 1) * kt2, :]
                    m = jnp.where(pred_fn(kt, j * kt2 + srow_2), 1, 0)
                    c = c + jnp.sum(m.reshape(kt2 // 8, 8, LANE), axis=0)
                return jnp.sum(c, axis=0, keepdims=True)
            return run
        return lax.switch(n2 - 1, [variant(m) for m in range(1, seq // kt2 + 1)])

    def bcast(v):
        return jnp.broadcast_to(v, (kt2, LANE))

    def search():
        c0 = count(lambda kt, s: kt >= 0)
        ans0 = jnp.where(c0 >= k_sel, 0, INT_MIN).astype(I32)

        def bit_body(bi, ans):
            cand = ans | lax.shift_left(jnp.int32(1), 30 - bi)
            cb = bcast(cand)
            cnt = count(lambda kt, s: kt >= cb)
            return jnp.where(cnt >= k_sel, cand, ans)

        ans = lax.fori_loop(0, 31, bit_body, ans0)
        ab = bcast(ans)
        cnt_ge = count(lambda kt, s: kt >= ab)
        cnt_gt = count(lambda kt, s: kt > ab)
        need = k_sel - cnt_gt

        def tie_search():
            def idx_body(bi, x):
                cand = x | lax.shift_left(jnp.int32(1), (seq.bit_length() - 2) - bi)
                cb = bcast(cand)
                f = count(lambda kt, s: (kt == ab) & (s < cb))
                return jnp.where(f < need, cand, x)
            return lax.fori_loop(0, seq.bit_length() - 1, idx_body, jnp.zeros((1, LANE), I32))

        cut = lax.cond(jnp.max(cnt_ge) > k_sel, tie_search,
                       lambda: jnp.full((1, LANE), seq - 1, I32))
        return ans, cut

    ans, cut = lax.cond(t0 >= k_sel, search,
                        lambda: (jnp.full((1, LANE), INT_MIN, I32),
                                 jnp.full((1, LANE), seq - 1, I32)))
    ans_b = bcast(ans)
    cut_b = bcast(cut)

    def mask_tile(j, carry):
        s0 = pl.multiple_of(j * kt2, kt2)
        kt = key_ref[pl.ds(s0, kt2), :]
        srow = s0 + srow_2
        sel = (kt > ans_b) | ((kt == ans_b) & (srow <= cut_b))
        am_ref[pl.ds(s0, kt2), :] = jnp.where(sel & (srow <= tcol), 0.0, NEG_INF)
        return carry

    lax.fori_loop(0, n2, mask_tile, 0)

    scale = jnp.asarray(A_HEAD_DIM ** -0.5, BF16)
    q_stacks = [jnp.concatenate([q_ref[:, (g * rep + r) * LANE:(g * rep + r + 1) * LANE]
                                 for r in range(rep)], axis=0) * scale for g in range(A_KV_HEADS)]
    acc_ref[...] = jnp.zeros(acc_ref.shape, F32)

    def logits_stage(j):
        slot = j & 1
        s0 = pl.multiple_of(j * kt2, kt2)
        am = am_ref[pl.ds(s0, kt2), :]
        band0 = jnp.clip(i - 2 * j, 0, 2)
        band1 = jnp.clip(i - 2 * j - 1, 0, 2)
        for g in range(A_KV_HEADS):
            kt = k_ref[pl.ds(s0, kt2), g * LANE:(g + 1) * LANE]
            lg_all = _dot_nt(kt, q_stacks[g])
            for r in range(rep):
                h = g * rep + r
                cs = slice(r * LANE, (r + 1) * LANE)
                bias = jnp.concatenate([bias_ref[h, band0], bias_ref[h, band1]], axis=0)
                lg = lg_all[:, cs] + bias + am
                lg_ref[slot, g, :, cs] = lg
                tmax_ref[slot, h:h + 1, :] = jnp.max(lg, axis=0, keepdims=True)

    def softmax_stage(j, carry):
        ms, ls = carry
        slot = j & 1
        new_ms, new_ls = [], []
        for g in range(A_KV_HEADS):
            ps, alphas = [], []
            for r in range(rep):
                h = g * rep + r
                m_new = jnp.maximum(ms[h], tmax_ref[slot, h:h + 1, :])
                m_safe = jnp.where(m_new == NEG_INF, 0.0, m_new)
                alpha = jnp.exp(ms[h] - m_safe)
                p = jnp.exp(lg_ref[slot, g, :, r * LANE:(r + 1) * LANE] - m_safe)
                new_ms.append(m_new)
                new_ls.append(alpha * ls[h] + jnp.sum(p, axis=0, keepdims=True))
                ps.append(p.astype(BF16))
                alphas.append(alpha)
            vt = vt_ref[j, g * A_HEAD_DIM:(g + 1) * A_HEAD_DIM, :]
            pv = _dot(vt, jnp.concatenate(ps, axis=1))
            acc_ref[g] = acc_ref[g] * jnp.concatenate(alphas, axis=1) + pv
        return tuple(new_ms), tuple(new_ls)

    def tile_body(j, carry):
        carry = softmax_stage(j, carry)
        logits_stage(j + 1)
        return carry

    init = (tuple(jnp.full((1, LANE), NEG_INF, F32) for _ in range(A_HEADS)),
            tuple(jnp.zeros((1, LANE), F32) for _ in range(A_HEADS)))
    logits_stage(0)
    carry = lax.fori_loop(0, n2 - 1, tile_body, init)
    _, ls = softmax_stage(n2 - 1, carry)
    for g in range(A_KV_HEADS):
        for r in range(rep):
            h = g * rep + r
            ot_ref[h * A_HEAD_DIM:(h + 1) * A_HEAD_DIM, :] = (
                acc_ref[g, :, r * LANE:(r + 1) * LANE] / ls[h])
    o_ref[...] = ot_ref[...].T.astype(o_ref.dtype)


def _dsa(h_a, h_f, bias_tiles, bsz, seq):
    n = bsz * seq
    nq = seq // LANE
    k_sel = min(TOPK_MAX, seq // 4)
    assert k_sel % LANE == 0 and seq % (2 * LANE) == 0
    qw = A_HEADS * LANE
    slab_blk = (h_f.shape[1] - LANE) // LANE
    return pl.pallas_call(
        functools.partial(_dsa_kernel, k_sel=k_sel, seq=seq),
        out_shape=jax.ShapeDtypeStruct((n, A_HEADS * A_HEAD_DIM), BF16),
        grid=(bsz, nq),
        in_specs=[
            pl.BlockSpec((LANE, qw), lambda b, i: (b * nq + i, 0)),
            pl.BlockSpec((seq, A_KV_HEADS * LANE), lambda b, i: (b, qw // (A_KV_HEADS * LANE))),
            pl.BlockSpec((seq, LANE), lambda b, i: (b, (qw + A_KV_HEADS * LANE) // LANE)),
            pl.BlockSpec((LANE, IDX_HEADS * LANE), lambda b, i: (b * nq + i, 0)),
            pl.BlockSpec((seq, LANE), lambda b, i: (b, slab_blk)),
            pl.BlockSpec((LANE, LANE), lambda b, i: (b * nq + i, slab_blk)),
            pl.BlockSpec((A_HEADS, 3, LANE, LANE), lambda b, i: (0, 0, 0, 0)),
        ],
        out_specs=pl.BlockSpec((LANE, A_HEADS * A_HEAD_DIM), lambda b, i: (b * nq + i, 0)),
        scratch_shapes=[
            pltpu.VMEM((seq + LANE, LANE), I32),
            pltpu.VMEM((seq, LANE), F32),
            pltpu.VMEM((nq // 2, LANE, 2 * LANE), BF16),
            pltpu.VMEM((A_KV_HEADS, A_HEAD_DIM, (A_HEADS // A_KV_HEADS) * LANE), F32),
            pltpu.VMEM((2, A_KV_HEADS, 2 * LANE, (A_HEADS // A_KV_HEADS) * LANE), F32),
            pltpu.VMEM((2, A_HEADS, LANE), F32),
            pltpu.VMEM((A_HEADS * A_HEAD_DIM, LANE), F32),
        ],
        compiler_params=_params(("arbitrary", "arbitrary")),
        name="dsa_attention",
    )(h_a, h_a, h_a, h_f, h_f, h_f, bias_tiles)


def _gla_kernel(bq_ref, bk_ref, bv_ref, br_ref, slab_ref, w2_ref, gb_ref, ng_ref, o_ref,
                st_ref, *, n_chunks):
    @pl.when(pl.program_id(1) == 0)
    def _():
        st_ref[...] = jnp.zeros(st_ref.shape, F32)

    ch = GLA_CHUNK
    rws = n_chunks * ch
    shift = ch.bit_length() - 1
    assert 1 << shift == ch
    row = lax.broadcasted_iota(I32, (rws, rws), 0)
    col = lax.broadcasted_iota(I32, (rws, rws), 1)
    tril = (lax.shift_right_logical(row, shift) == lax.shift_right_logical(col, shift)) & (row >= col)
    tri = jnp.where(tril, 1.0, 0.0).astype(BF16)
    gate = _dot(slab_ref[...].astype(BF16), w2_ref[...]) + gb_ref[...]
    log_a = (jnp.minimum(gate, 0.0) - jnp.log1p(jnp.exp(-jnp.abs(gate)))) / GATE_TAU
    hi, mid, lo = _split3(log_a)
    cum = _dot(tri, hi) + _dot(tri, mid) + _dot(tri, lo)
    lasts = [cum[(c + 1) * ch - 1:(c + 1) * ch, :] for c in range(n_chunks)]
    last = jnp.concatenate([jnp.broadcast_to(l, (ch, l.shape[1])) for l in lasts], axis=0)
    q = bq_ref[...] * (B_KEY_DIM ** -0.5)
    k = bk_ref[...]
    q_dec = (q * jnp.exp(cum)).astype(BF16)
    k_inv = (k * jnp.exp(-cum)).astype(BF16)
    k_end = (k * jnp.exp(last - cum)).astype(BF16)
    decays = [jnp.exp(l) for l in lasts]
    for h in range(B_HEADS):
        cs = slice(h * LANE, (h + 1) * LANE)
        v = bv_ref[:, cs].astype(BF16)
        sc = jnp.where(tril, _dot_nt(q_dec[:, cs], k_inv[:, cs]), 0.0)
        o_intra = _dot(sc.astype(BF16), v)
        st = st_ref[h]
        parts = []
        for c in range(n_chunks):
            rows = slice(c * ch, (c + 1) * ch)
            parts.append(o_intra[rows] + _dot_nt(q_dec[rows, cs], st.astype(BF16)))
            st = decays[c][:, cs] * st + _dot_tn(v[rows], k_end[rows, cs])
        st_ref[h] = st
        o = jnp.concatenate(parts, axis=0)
        mu = jnp.mean(o, axis=-1, keepdims=True)
        dlt = o - mu
        var = jnp.mean(dlt * dlt, axis=-1, keepdims=True)
        on = dlt * lax.rsqrt(var + LN_EPS) * ng_ref[:, cs]
        r = br_ref[:, cs]
        o_ref[:, cs] = (on * (r * _sigmoid(r))).astype(o_ref.dtype)


def _gla(h_f, w2p, gbp, norm_g, bsz, seq, rows_per_step=256):
    n = bsz * seq
    hw = B_HEADS * LANE
    steps = seq // rows_per_step
    slab_blk = (h_f.shape[1] - LANE) // LANE
    blk = lambda cb: pl.BlockSpec((rows_per_step, hw), lambda b, s, cb=cb: (b * steps + s, cb))
    return pl.pallas_call(
        functools.partial(_gla_kernel, n_chunks=rows_per_step // GLA_CHUNK),
        out_shape=jax.ShapeDtypeStruct((n, hw), BF16),
        grid=(bsz, steps),
        in_specs=[blk(1), blk(2), blk(3), blk(4),
                  pl.BlockSpec((rows_per_step, LANE), lambda b, s: (b * steps + s, slab_blk)),
                  pl.BlockSpec((LANE, hw), lambda b, s: (0, 0)),
                  pl.BlockSpec((1, hw), lambda b, s: (0, 0)),
                  pl.BlockSpec((1, hw), lambda b, s: (0, 0))],
        out_specs=pl.BlockSpec((rows_per_step, hw), lambda b, s: (b * steps + s, 0)),
        scratch_shapes=[pltpu.VMEM((B_HEADS, LANE, LANE), F32)],
        compiler_params=_params(("arbitrary", "arbitrary")),
        name="gla_attention",
    )(h_f, h_f, h_f, h_f, h_f, w2p, gbp, norm_g)


def _outproj_ln_kernel(oa_ref, ob_ref, x_ref, w_ref, g_ref, b_ref, out_ref):
    ka = oa_ref.shape[1]
    m = _dot(oa_ref[...], w_ref[0:ka, :]) + _dot(ob_ref[...], w_ref[ka:, :])
    out_ref[...] = _layer_norm(DN_ALPHA * x_ref[...] + m, g_ref[...], b_ref[...])


def _outproj_ln(o_a, o_b, x2, w, g, b, tm=512):
    n, d = x2.shape
    row = lambda width: pl.BlockSpec((tm, width), lambda i: (i, 0))
    full = lambda a: pl.BlockSpec(a.shape, lambda i: (0, 0))
    return pl.pallas_call(
        _outproj_ln_kernel,
        out_shape=jax.ShapeDtypeStruct((n, d), F32),
        grid=(n // tm,),
        in_specs=[row(o_a.shape[1]), row(o_b.shape[1]), row(d), full(w), full(g), full(b)],
        out_specs=row(d),
        compiler_params=_params(("arbitrary",)),
        name="out_proj_ln",
    )(o_a, o_b, x2, w, g, b)


def _route(lt):
    gl = [lt[g:g + 1, :] for g in range(MOE_GROUPS)]
    best, gsel = gl[0], jnp.zeros_like(gl[0], dtype=I32)
    for g in range(1, MOE_GROUPS):
        better = gl[g] > best
        gsel = jnp.where(better, g, gsel)
        best = jnp.where(better, gl[g], best)
    denom = sum(jnp.exp(x - best) for x in gl)
    g_w = 1.0 / denom
    fl = []
    for e in range(MOE_EXPERTS):
        acc = jnp.zeros_like(best)
        for g in range(MOE_GROUPS):
            r = MOE_GROUPS + g * MOE_EXPERTS + e
            acc = jnp.where(gsel == g, lt[r:r + 1, :], acc)
        fl.append(acc)
    v1, i1 = fl[0], jnp.zeros_like(gsel)
    for e in range(1, MOE_EXPERTS):
        better = fl[e] > v1
        i1 = jnp.where(better, e, i1)
        v1 = jnp.where(better, fl[e], v1)
    v2, i2 = jnp.full_like(v1, NEG_INF), jnp.zeros_like(gsel)
    for e in range(MOE_EXPERTS):
        better = (fl[e] > v2) & (i1 != e)
        i2 = jnp.where(better, e, i2)
        v2 = jnp.where(better, fl[e], v2)
    e2 = jnp.exp(v2 - v1)
    w1 = (1.0 / (1.0 + e2)) * g_w
    w2 = (e2 / (1.0 + e2)) * g_w
    e_gate = [jnp.where(i1 == e, w1, 0.0) + jnp.where(i2 == e, w2, 0.0)
              for e in range(MOE_EXPERTS)]
    return gsel, e_gate


def _moe_route_kernel(x_ref, wr_ref, rb_ref, xa_ref, rt_ref, cnt_ref, tri_ref, carry_ref):
    i = pl.program_id(0)
    tm, d = x_ref.shape

    @pl.when(i == 0)
    def _():
        carry_ref[...] = jnp.zeros(carry_ref.shape, F32)
        before = (lax.broadcasted_iota(I32, (tm, tm), 0) < lax.broadcasted_iota(I32, (tm, tm), 1))
        tri_ref[...] = jnp.where(before, 1.0, 0.0).astype(BF16)

    x = x_ref[...]
    xs = _split3(x)
    ws = _split3(wr_ref[...])
    lt = jnp.zeros((wr_ref.shape[0], tm), F32)
    for a, b in ((2, 0), (0, 2), (1, 1), (1, 0), (0, 1), (0, 0)):
        lt = lt + _dot_nt(ws[a], xs[b])
    lt = lt + rb_ref[...]
    gsel, e_gate = _route(lt)

    row8 = lax.broadcasted_iota(I32, (8, tm), 0)
    onehot = jnp.where(row8 == gsel, 1.0, 0.0)
    earlier = _dot(onehot.astype(BF16), tri_ref[...])
    rank = jnp.sum(onehot * (earlier + carry_ref[:, 0:1]), axis=0, keepdims=True)
    rt_ref[...] = jnp.where(row8 == 0, gsel, jnp.where(row8 == 1, rank.astype(I32), 0))
    carry_ref[...] = carry_ref[...] + jnp.sum(onehot, axis=1, keepdims=True)
    cnt_ref[...] = carry_ref[...].astype(I32)

    erow = lax.broadcasted_iota(I32, (LANE, tm), 0)
    gt = jnp.zeros((LANE, tm), F32)
    for e in range(MOE_EXPERTS):
        gt = jnp.where(erow == e, e_gate[e], gt)
    xa_ref[:, :d] = x
    xa_ref[:, d:] = gt.T


def _moe_route(h, wr, rb, tm=1024):
    n, d = h.shape
    tm = min(tm, n)
    full = lambda a: pl.BlockSpec(a.shape, lambda i: (0,) * a.ndim)
    return pl.pallas_call(
        _moe_route_kernel,
        out_shape=[jax.ShapeDtypeStruct((n, d + LANE), F32), jax.ShapeDtypeStruct((8, n), I32),
                   jax.ShapeDtypeStruct((8, LANE), I32)],
        grid=(n // tm,),
        in_specs=[pl.BlockSpec((tm, d), lambda i: (i, 0)), full(wr), full(rb)],
        out_specs=[pl.BlockSpec((tm, d + LANE), lambda i: (i, 0)),
                   pl.BlockSpec((8, tm), lambda i: (0, i)),
                   pl.BlockSpec((8, LANE), lambda i: (0, 0))],
        scratch_shapes=[pltpu.VMEM((tm, tm), BF16), pltpu.VMEM((8, LANE), F32)],
        compiler_params=_params(("arbitrary",)),
        name="moe_route",
    )(h, wr, rb)


def _row_copy_waves(n_rows, wave, start_row, wait_wave):
    n_waves = n_rows // wave
    for w in range(n_waves + 1):
        if w < n_waves:
            lax.fori_loop(w * wave, (w + 1) * wave, lambda r, c, w=w: (start_row(r, w % 2), c)[1], 0,
                          unroll=8)
        if w >= 1:
            wait_wave((w - 1) % 2)


def _moe_dispatch_kernel(pos_ref, fill_ref, xa_ref, xs_hbm, zero_ref, sem, *, wave):
    i = pl.program_id(0)
    tm = pos_ref.shape[1]
    tme = zero_ref.shape[0]

    def copy(r, s):
        return pltpu.make_async_copy(xa_ref.at[r], xs_hbm.at[pos_ref[0, r]], sem.at[s])

    def wait_wave(s):
        rows = pl.ds(0, wave)
        pltpu.make_async_copy(xa_ref.at[rows], xs_hbm.at[rows], sem.at[s]).wait()

    _row_copy_waves(tm, wave, lambda r, s: copy(r, s).start(), wait_wave)

    @pl.when(i == pl.num_programs(0) - 1)
    def _():
        zero_ref[...] = jnp.zeros(zero_ref.shape, F32)
        for g in range(MOE_GROUPS):
            first, count = fill_ref[0, g], fill_ref[1, g]

            def zcopy(r, first=first):
                return pltpu.make_async_copy(zero_ref.at[0], xs_hbm.at[first + r], sem.at[0])

            lax.fori_loop(0, count, lambda r, c: (zcopy(r).start(), c)[1], 0)
            lax.fori_loop(0, count, lambda r, c: (zcopy(r).wait(), c)[1], 0)
        n_tiles = fill_ref[2, 0]
        for k in range(MOE_GROUPS):
            @pl.when(k < fill_ref[2, 1])
            def _(k=k):
                tail = xs_hbm.at[pl.ds(pl.multiple_of((n_tiles + k) * tme, tme), tme)]
                cp = pltpu.make_async_copy(zero_ref, tail, sem.at[0])
                cp.start()
                cp.wait()


def _moe_dispatch(xa, pos, fill, n_sorted, tme, tm=1024, wave=256):
    n, da = xa.shape
    tm = min(tm, n)
    wave = min(wave, tm)
    return pl.pallas_call(
        functools.partial(_moe_dispatch_kernel, wave=wave),
        out_shape=jax.ShapeDtypeStruct((n_sorted, da), F32),
        grid=(n // tm,),
        in_specs=[pl.BlockSpec((1, tm), lambda i: (0, i), memory_space=pltpu.SMEM),
                  pl.BlockSpec(memory_space=pltpu.SMEM),
                  pl.BlockSpec((tm, da), lambda i: (i, 0))],
        out_specs=pl.BlockSpec(memory_space=pl.ANY),
        scratch_shapes=[pltpu.VMEM((tme, da), F32), pltpu.SemaphoreType.DMA((2,))],
        compiler_params=_params(("arbitrary",)),
        name="moe_dispatch",
    )(pos, fill, xa)


def _moe_expert_kernel(tg_ref, tj_ref, nt_ref, xs_ref, wg_ref, wu_ref, wd_ref, lg_ref, lb_ref,
                       ys_ref, wgb_ref, wub_ref, wdb_ref):
    d = ys_ref.shape[1]
    i = pl.program_id(0)

    @pl.when((i == 0) | (tg_ref[i] != tg_ref[jnp.maximum(i - 1, 0)]))
    def _():
        for e in range(MOE_EXPERTS):
            wgb_ref[e] = wg_ref[e].astype(BF16)
            wub_ref[e] = wu_ref[e].astype(BF16)
            wdb_ref[e] = wd_ref[e].astype(BF16)

    @pl.when(i < nt_ref[0])
    def _():
        x = xs_ref[:, :d]
        gates = xs_ref[:, d:]
        xb = x.astype(BF16)
        y = jnp.zeros(x.shape, F32)
        for e in range(MOE_EXPERTS):
            hg = _dot(xb, wgb_ref[e])
            hu = _dot(xb, wub_ref[e])
            hid = hg * _sigmoid(hg) * hu * gates[:, e:e + 1]
            y = y + _dot(hid.astype(BF16), wdb_ref[e])
        ys_ref[...] = _layer_norm(DN_ALPHA * x + y, lg_ref[...], lb_ref[...])

    @pl.when(pl.program_id(0) >= nt_ref[0])
    def _():
        ys_ref[...] = jnp.zeros(ys_ref.shape, F32)


def _moe_experts(xs, tile_g, tile_j, n_tiles, layer, wg, wu, wd, lg, lb, tme):
    p, da = xs.shape
    d = da - LANE
    grp = lambda w: pl.BlockSpec((None, None) + w.shape[2:],
                                 lambda i, tg, tj, nt: (layer, tg[i], 0, 0, 0),
                                 pipeline_mode=pl.Buffered(1))
    full = lambda a: pl.BlockSpec(a.shape, lambda i, tg, tj, nt: (0,) * a.ndim)
    grid_spec = pltpu.PrefetchScalarGridSpec(
        num_scalar_prefetch=3,
        grid=(p // tme,),
        in_specs=[pl.BlockSpec((tme, da), lambda i, tg, tj, nt: (tj[i], 0)),
                  grp(wg), grp(wu), grp(wd), full(lg), full(lb)],
        out_specs=pl.BlockSpec((tme, d), lambda i, tg, tj, nt: (i, 0)),
        scratch_shapes=[pltpu.VMEM(w.shape[2:], BF16) for w in (wg, wu, wd)],
    )
    return pl.pallas_call(
        _moe_expert_kernel,
        out_shape=jax.ShapeDtypeStruct((p, d), F32),
        grid_spec=grid_spec,
        compiler_params=pltpu.CompilerParams(dimension_semantics=("arbitrary",),
                                             vmem_limit_bytes=MOE_VMEM_LIMIT),
        name="moe_experts",
    )(tile_g, tile_j, n_tiles, xs, wg, wu, wd, lg, lb)


def _moe_combine_kernel(pos_ref, ys_hbm, o_ref, sem, *, wave):
    tm = o_ref.shape[0]

    def copy(r, s):
        return pltpu.make_async_copy(ys_hbm.at[pos_ref[0, r]], o_ref.at[r], sem.at[s])

    def wait_wave(s):
        rows = pl.ds(0, wave)
        pltpu.make_async_copy(ys_hbm.at[rows], o_ref.at[rows], sem.at[s]).wait()

    _row_copy_waves(tm, wave, lambda r, s: copy(r, s).start(), wait_wave)


def _moe_combine(ys, pos, n, tm=1024, wave=256):
    d = ys.shape[1]
    tm = min(tm, n)
    wave = min(wave, tm)
    return pl.pallas_call(
        functools.partial(_moe_combine_kernel, wave=wave),
        out_shape=jax.ShapeDtypeStruct((n, d), F32),
        grid=(n // tm,),
        in_specs=[pl.BlockSpec((1, tm), lambda i: (0, i), memory_space=pltpu.SMEM),
                  pl.BlockSpec(memory_space=pl.ANY)],
        out_specs=pl.BlockSpec((tm, d), lambda i: (i, 0)),
        scratch_shapes=[pltpu.SemaphoreType.DMA((2,))],
        compiler_params=_params(("arbitrary",)),
        name="moe_combine",
    )(pos, ys)


def _moe(h, wr, rb, layer, wg, wu, wd, lg, lb, regroup=None, tme=512):
    n, d = h.shape
    tme = min(tme, n)
    xa, rt, cnt = _moe_route(h, wr, rb)
    counts = cnt[:MOE_GROUPS, 0]
    tiles = (counts + tme - 1) // tme
    first_tile = jnp.cumsum(tiles) - tiles
    gid, rank = rt[0], rt[1]
    pos = (first_tile[gid] * tme + rank)[None, :]
    n_slots = n // tme + MOE_GROUPS
    n_tiles = jnp.sum(tiles)
    fill = jnp.stack([first_tile * tme + counts, tiles * tme - counts,
                      jnp.zeros((MOE_GROUPS,), I32).at[0].set(n_tiles).at[1].set(n_slots - n_tiles)])
    slot = jnp.minimum(jnp.arange(n_slots, dtype=I32), n_tiles - 1)
    tile_g = jnp.sum(slot[:, None] >= (first_tile + tiles)[None, :], axis=1).astype(I32)
    xs = _moe_dispatch(xa, pos, fill, n_slots * tme, tme)
    ys = _moe_experts(xs, tile_g, slot, n_tiles[None].astype(I32), layer, wg, wu, wd, lg, lb, tme)
    if regroup is not None:
        pos = pos.reshape(regroup).T.reshape(1, n)
    return _moe_combine(ys, pos, n)


def _s5_prep_kernel(lre_ref, lim_ref, ldt_ref, bre_ref, bim_ref, are_ref, aim_ref, bbre_ref,
                    bbim_ref):
    lr = jnp.minimum(lre_ref[...], -1e-4)
    li = lim_ref[...]
    dt = jnp.exp(ldt_ref[...])
    mag = jnp.exp(lr * dt)
    ab_re = mag * jnp.cos(li * dt)
    ab_im = mag * jnp.sin(li * dt)
    den = lr * lr + li * li
    nr = ab_re - 1.0
    coef_re = (nr * lr + ab_im * li) / den
    coef_im = (ab_im * lr - nr * li) / den
    are_ref[...] = ab_re
    aim_ref[...] = ab_im
    bbre_ref[...] = coef_re * bre_ref[...] - coef_im * bim_ref[...]
    bbim_ref[...] = coef_re * bim_ref[...] + coef_im * bre_ref[...]


def _s5_prep(lam_re, lam_im, log_dt, b_re, b_im):
    gp = C_GROUPS * C_STATE
    col = lambda a: a.reshape(gp, 1)
    ldt = jnp.broadcast_to(log_dt[:, None], (C_GROUPS, C_STATE))
    outs = pl.pallas_call(
        _s5_prep_kernel,
        out_shape=[jax.ShapeDtypeStruct((gp, 1), F32), jax.ShapeDtypeStruct((gp, 1), F32),
                   jax.ShapeDtypeStruct((gp, C_GROUP), F32), jax.ShapeDtypeStruct((gp, C_GROUP), F32)],
        name="s5_discretise",
    )(col(lam_re), col(lam_im), col(ldt), b_re.reshape(gp, C_GROUP), b_im.reshape(gp, C_GROUP))
    ab_re, ab_im, bb_re, bb_im = outs
    return (ab_re.reshape(C_GROUPS, C_STATE), ab_im.reshape(C_GROUPS, C_STATE),
            bb_re.reshape(C_GROUPS, C_STATE, C_GROUP), bb_im.reshape(C_GROUPS, C_STATE, C_GROUP))


def _s5_uproj_kernel(x_ref, w_ref, o_ref):
    o_ref[...] = _dot(x_ref[...].astype(BF16), w_ref[...])


def _s5_uproj(h2, w, tm=512):
    n, d = h2.shape
    return pl.pallas_call(
        _s5_uproj_kernel,
        out_shape=jax.ShapeDtypeStruct((n, d), F32),
        grid=(n // tm,),
        in_specs=[pl.BlockSpec((tm, d), lambda i: (i, 0)), pl.BlockSpec(w.shape, lambda i: (0, 0))],
        out_specs=pl.BlockSpec((tm, d), lambda i: (i, 0)),
        compiler_params=_params(("arbitrary",)),
        name="s5_in_proj",
    )(h2, w)


def _gelu_tanh(x):
    return 0.5 * x * (1.0 + jnp.tanh(math.sqrt(2.0 / math.pi) * (x + 0.044715 * (x * x * x))))


def _s5_scan_kernel(u_ref, bbre_ref, bbim_ref, are_ref, aim_ref, cm_ref, d_ref, g_ref,
                    st_ref, bure_ref, buim_ref, s_ref, *, bsz, tl, n_cb, sub):
    @pl.when(pl.program_id(0) == 0)
    def _():
        st_ref[...] = jnp.zeros(st_ref.shape, F32)

    cw = D_MODEL // n_cb
    sw = bure_ref.shape[2]

    def input_states(cb):
        ub = u_ref[:, cb * cw:(cb + 1) * cw].astype(BF16)
        bure_ref[cb % 2] = _dot(ub, bbre_ref[cb])
        buim_ref[cb % 2] = _dot(ub, bbim_ref[cb])

    input_states(0)
    for cb in range(n_cb):
        buf = cb % 2
        if cb + 1 < n_cb:
            input_states(cb + 1)
        for s0 in range(0, sw, sub):
            cs = slice(s0, s0 + sub)
            a_re = jnp.broadcast_to(are_ref[cb, :, cs], (bsz, sub))
            a_im = jnp.broadcast_to(aim_ref[cb, :, cs], (bsz, sub))
            s_re, s_im = st_ref[cb, 0, :, cs], st_ref[cb, 1, :, cs]
            for t in range(tl):
                rows = slice(t * bsz, (t + 1) * bsz)
                s_re, s_im = (a_re * s_re - a_im * s_im + bure_ref[buf, rows, cs],
                              a_re * s_im + a_im * s_re + buim_ref[buf, rows, cs])
                s_ref[buf, rows, cs] = s_re.astype(BF16)
                s_ref[buf, rows, sw + s0:sw + s0 + sub] = s_im.astype(BF16)
            st_ref[cb, 0, :, cs] = s_re
            st_ref[cb, 1, :, cs] = s_im
        ch = slice(cb * cw, (cb + 1) * cw)
        y = _dot(s_ref[buf], cm_ref[cb]) + d_ref[:, ch] * u_ref[:, ch]
        g_ref[:, ch] = _gelu_tanh(y).astype(g_ref.dtype)


def _s5_scan(u_lb, bbre, bbim, a_re, a_im, cmat, d_skip, bsz, seq, tl=32, sub=512):
    n, d = u_lb.shape
    n_cb, cw, sw = bbre.shape
    rows = tl * bsz
    full = lambda a: pl.BlockSpec(a.shape, lambda t: (0,) * a.ndim)
    return pl.pallas_call(
        functools.partial(_s5_scan_kernel, bsz=bsz, tl=tl, n_cb=n_cb, sub=sub),
        out_shape=jax.ShapeDtypeStruct((n, d), BF16),
        grid=(seq // tl,),
        in_specs=[pl.BlockSpec((rows, d), lambda t: (t, 0)), full(bbre), full(bbim), full(a_re),
                  full(a_im), full(cmat), full(d_skip)],
        out_specs=pl.BlockSpec((rows, d), lambda t: (t, 0)),
        scratch_shapes=[pltpu.VMEM((n_cb, 2, bsz, sw), F32), pltpu.VMEM((2, rows, sw), F32),
                        pltpu.VMEM((2, rows, sw), F32), pltpu.VMEM((2, rows, 2 * sw), BF16)],
        compiler_params=_params(("arbitrary",)),
        name="s5_scan",
    )(u_lb, bbre, bbim, a_re, a_im, cmat, d_skip)


def _s5_glu_kernel(g_ref, x_ref, w1_ref, w2_ref, wo_ref, lg_ref, lb_ref, out_ref):
    g = g_ref[...]
    z = _dot(g, w1_ref[...]) * _sigmoid(_dot(g, w2_ref[...]))
    m = _dot(z.astype(BF16), wo_ref[...])
    out_ref[...] = _layer_norm(DN_ALPHA * x_ref[...] + m, lg_ref[...], lb_ref[...])


def _s5_glu(g, h2, w1, w2, wo, lg, lb, tm=512):
    n, d = h2.shape
    row = pl.BlockSpec((tm, d), lambda i: (i, 0))
    full = lambda a: pl.BlockSpec(a.shape, lambda i: (0, 0))
    return pl.pallas_call(
        _s5_glu_kernel,
        out_shape=jax.ShapeDtypeStruct((n, d), F32),
        grid=(n // tm,),
        in_specs=[row, row, full(w1), full(w2), full(wo), full(lg), full(lb)],
        out_specs=row,
        compiler_params=_params(("arbitrary",)),
        name="s5_glu_out_ln",
    )(g, h2, w1, w2, wo, lg, lb)


def _pad_heads(w, heads, dim):
    d = w.shape[0]
    w = w.reshape(d, heads, dim)
    return jnp.pad(w, ((0, 0), (0, 0), (0, LANE - dim))).reshape(d, heads * LANE)


def _even_layer_weights(w_in, gate_w2, gate_b):
    splits = (A_HEADS * A_HEAD_DIM, A_KV_HEADS * A_HEAD_DIM, A_KV_HEADS * A_HEAD_DIM,
              IDX_HEADS * IDX_DIM, IDX_DIM, IDX_HEADS,
              B_HEADS * B_KEY_DIM, B_HEADS * B_KEY_DIM, B_HEADS * B_VAL_DIM, GATE_RANK,
              B_HEADS * B_VAL_DIM)
    offs = np.cumsum((0,) + splits)
    aq, ak, av, iq, ik, iw, bq, bk, bv, bg, br = [w_in[:, offs[k]:offs[k + 1]] for k in range(11)]
    d = w_in.shape[0]
    slab = jnp.concatenate(
        [ik, iw, bg, jnp.zeros((d, LANE - IDX_DIM - IDX_HEADS - GATE_RANK), w_in.dtype)], axis=1)
    w_a = jnp.concatenate([_pad_heads(aq, A_HEADS, A_HEAD_DIM),
                           _pad_heads(ak, A_KV_HEADS, A_HEAD_DIM), av], axis=1)
    w_f = jnp.concatenate([_pad_heads(iq, IDX_HEADS, IDX_DIM), _pad_heads(bq, B_HEADS, B_KEY_DIM),
                           _pad_heads(bk, B_HEADS, B_KEY_DIM), bv, br, slab], axis=1)
    w_all = jnp.concatenate([w_a, w_f], axis=1).astype(BF16)
    g0 = IDX_DIM + IDX_HEADS
    w2p = jnp.zeros((LANE, B_HEADS * LANE), F32).at[g0:g0 + GATE_RANK].set(
        _pad_heads(gate_w2, B_HEADS, B_KEY_DIM)).astype(BF16)
    gbp = _pad_heads(gate_b[None, :], B_HEADS, B_KEY_DIM)
    return w_all, (w_a.shape[1], w_f.shape[1]), w2p, gbp


def _even_mixer_ln(x2, bsz, seq, w_in, bias_tiles, gate_w2, gate_b, norm_g, w_out, ln_g, ln_b):
    w_all, widths, w2p, gbp = _even_layer_weights(w_in, gate_w2, gate_b)
    h_a, h_f = _in_proj(x2, w_all, widths, (BF16, F32))
    o_a = _dsa(h_a, h_f, bias_tiles, bsz, seq)
    o_b = _gla(h_f, w2p, gbp, norm_g[None, :], bsz, seq)
    return _outproj_ln(o_a, o_b, x2, w_out.astype(BF16), ln_g[None, :], ln_b[None, :])


def _block_diag(m, per):
    g, r, c = m.shape
    eye = jnp.eye(per, dtype=m.dtype)
    m = m.reshape(g // per, per, r, c)
    bd = m[:, :, :, None, :] * eye[None, :, None, :, None]
    return bd.reshape(g // per, per * r, per * c)


def _s5_mixer_ln(h2, bsz, seq, w_in, lam_re, lam_im, log_dt, b_re, b_im, c_re, c_im, d_skip,
                 glu_w1, glu_w2, w_out, ln_g, ln_b, groups_per_block=16):
    ab_re, ab_im, bb_re, bb_im = _s5_prep(lam_re, lam_im, log_dt, b_re, b_im)
    per = groups_per_block
    n_cb = C_GROUPS // per
    bbre = _block_diag(jnp.swapaxes(bb_re, 1, 2), per).astype(BF16)
    bbim = _block_diag(jnp.swapaxes(bb_im, 1, 2), per).astype(BF16)
    a_re = ab_re.reshape(n_cb, 1, per * C_STATE)
    a_im = ab_im.reshape(n_cb, 1, per * C_STATE)
    cre = _block_diag(jnp.swapaxes(c_re, 1, 2), per)
    cim = _block_diag(jnp.swapaxes(c_im, 1, 2), per)
    cmat = jnp.concatenate([cre, -cim], axis=1).astype(BF16)
    u = _s5_uproj(h2, w_in.astype(BF16))
    g = _s5_scan(u, bbre, bbim, a_re, a_im, cmat, d_skip[None, :], bsz, seq)
    return _s5_glu(g, h2, glu_w1.astype(BF16), glu_w2.astype(BF16), w_out.astype(BF16),
                   ln_g[None, :], ln_b[None, :])


def _moe_ln(h, layer, r_coarse, rb_coarse, r_fine, rb_fine, w_gate, w_up, w_down, ln_g, ln_b,
            regroup=None):
    d = h.shape[1]
    rows = MOE_GROUPS + N_EXPERTS
    wr = jnp.concatenate([r_coarse.T, jnp.transpose(r_fine, (0, 2, 1)).reshape(N_EXPERTS, d),
                          jnp.zeros((32 - rows, d), F32)], axis=0)
    rb = jnp.concatenate([rb_coarse, rb_fine.reshape(N_EXPERTS), jnp.zeros((32 - rows,), F32)])
    return _moe(h, wr, rb[:, None], layer, w_gate, w_up, w_down, ln_g[None, :], ln_b[None, :],
                regroup)


def kernel(x, rel_bias, ab_w_in, gla_gate_w2, gla_gate_b, gla_norm_g, ab_w_out, s5_w_in, s5_lam_re, s5_lam_im, s5_log_dt, s5_b_re, s5_b_im, s5_c_re, s5_c_im, s5_d, s5_glu_w1, s5_glu_w2, s5_w_out, ln_mix_g, ln_mix_b, ln_ffn_g, ln_ffn_b, moe_r_coarse, moe_rb_coarse, moe_r_fine, moe_rb_fine, moe_w_gate, moe_w_up, moe_w_down):
    bsz, seq, d = x.shape
    h = x.reshape(bsz * seq, d)
    bias_tiles = _bias_tiles(rel_bias)
    time_major = False
    for layer in range(DEPTH):
        i = layer // 2
        assert time_major == (layer % 2 == 1)
        if layer % 2 == 0:
            h = _even_mixer_ln(h, bsz, seq, ab_w_in[i], bias_tiles, gla_gate_w2[i], gla_gate_b[i],
                               gla_norm_g[i], ab_w_out[i], ln_mix_g[layer], ln_mix_b[layer])
        else:
            h = _s5_mixer_ln(h, bsz, seq, s5_w_in[i], s5_lam_re[i], s5_lam_im[i], s5_log_dt[i],
                             s5_b_re[i], s5_b_im[i], s5_c_re[i], s5_c_im[i], s5_d[i],
                             s5_glu_w1[i], s5_glu_w2[i], s5_w_out[i],
                             ln_mix_g[layer], ln_mix_b[layer])
        want_time_major = layer + 1 < DEPTH and (layer + 1) % 2 == 1
        regroup = None
        if want_time_major != time_major:
            regroup = (seq, bsz) if time_major else (bsz, seq)
        h = _moe_ln(h, layer, moe_r_coarse[layer], moe_rb_coarse[layer], moe_r_fine[layer],
                    moe_rb_fine[layer], moe_w_gate, moe_w_up, moe_w_down,
                    ln_ffn_g[layer], ln_ffn_b[layer], regroup)
        time_major = want_time_major
    return h.reshape(bsz, seq, d).astype(x.dtype)
```
